```python
import jax, jax.numpy as jnp
from jax import lax
import numpy as np


D_MODEL = 1024
BATCH = 8
SEQ = 4096
DEPTH = 2

CTX_LEN = 256
GRID_W = 64
BLOCK = 128
WINDOW = 128
ROPE_BASE = 10000.0
EPS = 1e-6
N_MOD = 6
D_FF = 4 * D_MODEL

A_HEADS = 4
A_HEAD_DIM = D_MODEL // 8
A_WIDTH = A_HEADS * A_HEAD_DIM
A_GATES = 4 * A_HEADS
B_HEADS = 8
B_KV_HEADS = 2
B_HEAD_DIM = D_MODEL // 16
B_WIDTH = B_HEADS * B_HEAD_DIM
B_KV_WIDTH = B_KV_HEADS * B_HEAD_DIM
AB_IN = 4 * A_WIDTH + A_GATES + B_WIDTH + 2 * B_KV_WIDTH
AB_OUT = A_WIDTH + B_WIDTH
C_HEADS = 8
C_KV_HEADS = 2
C_HEAD_DIM = D_MODEL // 8
C_WIDTH = C_HEADS * C_HEAD_DIM
C_KV_WIDTH = C_KV_HEADS * C_HEAD_DIM
C_IN = C_WIDTH + 2 * C_KV_WIDTH

kernel_name = 'hybrid_mlstm_swa_axial_gqa_dit'


def rms_norm(x, g):
    xf = x.astype(jnp.float32)
    y = xf * lax.rsqrt(jnp.mean(xf * xf, axis=-1, keepdims=True) + EPS)
    return (y * g.astype(jnp.float32)).astype(x.dtype)


def modulate(x, g, mod, i):
    return rms_norm(x, g) * (1 + mod[:, :, i + 1]) + mod[:, :, i]


def sq_relu_mlp(h, w1, w2):
    return jnp.square(jax.nn.relu(h @ w1)) @ w2


def axial_rope_tables(n_tokens, head_dim):
    rows = n_tokens // GRID_W
    row = jnp.broadcast_to(jnp.arange(rows)[:, None], (rows, GRID_W)).reshape(n_tokens).astype(jnp.float32)
    col = jnp.broadcast_to(jnp.arange(GRID_W)[None, :], (rows, GRID_W)).reshape(n_tokens).astype(jnp.float32)
    pairs_per_axis = head_dim // 4
    inv_freq = ROPE_BASE ** (-jnp.arange(pairs_per_axis, dtype=jnp.float32) / pairs_per_axis)
    ang = jnp.concatenate([row[:, None] * inv_freq, col[:, None] * inv_freq], axis=-1)
    return jnp.cos(ang), jnp.sin(ang)


def apply_rope(x, cos, sin):
    half = x.shape[-1] // 2
    xf = x.astype(jnp.float32)
    x1, x2 = xf[..., :half], xf[..., half:]
    cs, sn = cos[:, None, :], sin[:, None, :]
    return jnp.concatenate([x1 * cs - x2 * sn, x2 * cs + x1 * sn], axis=-1).astype(x.dtype)


def gqa_attend(q, k, v, sink=None):
    b, tq, hq, d = q.shape
    hkv = k.shape[2]
    g = hq // hkv
    s = jnp.einsum('bqhgd,bkhd->bhgqk', q.reshape(b, tq, hkv, g, d), k).astype(jnp.float32) * d ** -0.5
    if sink is None:
        p = jax.nn.softmax(s, axis=-1)
    else:
        s_sink = jnp.broadcast_to(sink.astype(jnp.float32).reshape(1, hkv, g, 1, 1), s.shape[:-1] + (1,))
        p = jax.nn.softmax(jnp.concatenate([s, s_sink], axis=-1), axis=-1)[..., :-1]
    o = jnp.einsum('bhgqk,bkhd->bqhgd', p.astype(v.dtype), v)
    return o.reshape(b, tq, hq * d)


def banded_window_attention(q, k, v, k_ctx, v_ctx, sink):
    b, t, hq, d = q.shape
    hkv = k.shape[2]
    g = hq // hkv
    nb = t // BLOCK
    n_ctx = k_ctx.shape[1]
    nw = 3 * BLOCK
    scale = d ** -0.5
    qb = jnp.moveaxis(q.reshape(b, nb, BLOCK, hkv, g, d), 1, 0)

    def band(a):
        a = jnp.pad(a, ((0, 0), (BLOCK, BLOCK), (0, 0), (0, 0))).reshape(b, nb + 2, BLOCK, hkv, d)
        a = jnp.concatenate([a[:, :-2], a[:, 1:-1], a[:, 2:]], axis=2)
        return jnp.moveaxis(a, 1, 0)

    kw, vw = band(k), band(v)
    qpos = jnp.arange(nb)[:, None, None] * BLOCK + jnp.arange(BLOCK)[None, :, None]
    kpos = (jnp.arange(nb)[:, None, None] - 1) * BLOCK + jnp.arange(nw)[None, None, :]
    valid = (jnp.abs(kpos - qpos) <= WINDOW) & (kpos >= 0) & (kpos < t)
    sink_l = sink.astype(jnp.float32).reshape(1, hkv, g, 1, 1)

    def one_block(args):
        qn, kn, vn, valid_n = args
        s_win = jnp.einsum('bqhgd,bkhd->bhgqk', qn, kn).astype(jnp.float32) * scale
        s_win = jnp.where(valid_n, s_win, -jnp.inf)
        s_ctx = jnp.einsum('bqhgd,bchd->bhgqc', qn, k_ctx).astype(jnp.float32) * scale
        s_sink = jnp.broadcast_to(sink_l, s_win.shape[:-1] + (1,))
        p = jax.nn.softmax(jnp.concatenate([s_win, s_ctx, s_sink], axis=-1), axis=-1).astype(v.dtype)
        o = (jnp.einsum('bhgqk,bkhd->bqhgd', p[..., :nw], vn)
             + jnp.einsum('bhgqc,bchd->bqhgd', p[..., nw:nw + n_ctx], v_ctx))
        return o.reshape(b, BLOCK, hq * d)

    o = lax.map(one_block, (qb, kw, vw, valid))
    return jnp.moveaxis(o, 0, 1).reshape(b, t, hq * d)


def mlstm_dir(q, k, v, log_i, log_f, state):
    n_chunks = q.shape[2] // BLOCK

    def to_chunks(a):
        a = a.reshape(a.shape[:2] + (n_chunks, BLOCK) + a.shape[3:])
        return jnp.moveaxis(a, 2, 0)

    lower = jnp.tril(jnp.ones((BLOCK, BLOCK), dtype=bool))

    def step(carry, inp):
        c_mat, n_vec, m = carry
        qc, kc, vc, ic, fc = inp
        bcum = jnp.cumsum(fc, axis=-1)
        d_log = jnp.where(lower, bcum[..., :, None] - bcum[..., None, :] + ic[..., None, :], -jnp.inf)
        inter_log = bcum + m[..., None]
        m_t = jnp.maximum(inter_log, jnp.max(d_log, axis=-1))
        inter_w = jnp.exp(inter_log - m_t)
        s = jnp.einsum('bhtd,bhsd->bhts', qc, kc) * jnp.exp(d_log - m_t[..., None])
        num = inter_w[..., None] * jnp.einsum('bhtd,bhde->bhte', qc, c_mat) + jnp.einsum('bhts,bhse->bhte', s, vc)
        den = inter_w * jnp.einsum('bhtd,bhd->bht', qc, n_vec) + jnp.sum(s, axis=-1)
        h = num / jnp.maximum(jnp.abs(den), jnp.exp(-m_t))[..., None]
        b_last = bcum[..., -1]
        w_log = b_last[..., None] - bcum + ic
        m_new = jnp.maximum(b_last + m, jnp.max(w_log, axis=-1))
        carry_w = jnp.exp(b_last + m - m_new)
        w = jnp.exp(w_log - m_new[..., None])
        c_new = carry_w[..., None, None] * c_mat + jnp.einsum('bhs,bhsd,bhse->bhde', w, kc, vc)
        n_new = carry_w[..., None] * n_vec + jnp.einsum('bhs,bhsd->bhd', w, kc)
        return (c_new, n_new, m_new), h

    state, h = lax.scan(step, state, (to_chunks(q), to_chunks(k), to_chunks(v), to_chunks(log_i), to_chunks(log_f)))
    h = jnp.moveaxis(h, 0, 2)
    return h.reshape(h.shape[:2] + (-1, h.shape[-1])), state


def mlstm_bidir(q, k, v, gates, state_f, state_b):
    li_f, lf_f, li_b, lf_b = gates
    h_f, st_f = mlstm_dir(q, k, v, li_f, lf_f, state_f)
    flip = lambda a: jnp.flip(a, axis=2)
    h_b, st_b = mlstm_dir(flip(q), flip(k), flip(v), flip(li_b), flip(lf_b), state_b)
    return h_f + flip(h_b), st_f, st_b


def ab_mixer(h_ctx, h_lat, w_in, gate_b, a_norm_g, q_norm_g, k_norm_g, sink, w_out, need_ctx):
    batch = h_lat.shape[0]
    sizes = [A_WIDTH] * 4 + [A_GATES, B_WIDTH, B_KV_WIDTH]
    bounds = [int(s) for s in np.cumsum(sizes)]

    def project(h):
        t = h.shape[1]
        qa, ka, va, oa, ga, qb, kb, vb = jnp.split(h @ w_in, bounds, axis=-1)
        heads_a = lambda a: a.reshape(batch, t, A_HEADS, A_HEAD_DIM).transpose(0, 2, 1, 3).astype(jnp.float32)
        qa, ka, va = heads_a(qa), heads_a(ka) * A_HEAD_DIM ** -0.5, heads_a(va)
        g = (ga + gate_b).astype(jnp.float32).reshape(batch, t, 4, A_HEADS).transpose(2, 0, 3, 1)
        gates = (g[0], jax.nn.log_sigmoid(g[1]), g[2], jax.nn.log_sigmoid(g[3]))
        qb = rms_norm(qb.reshape(batch, t, B_HEADS, B_HEAD_DIM), q_norm_g)
        kb = rms_norm(kb.reshape(batch, t, B_KV_HEADS, B_HEAD_DIM), k_norm_g)
        vb = vb.reshape(batch, t, B_KV_HEADS, B_HEAD_DIM)
        return (qa, ka, va, gates), oa, (qb, kb, vb)

    def mlstm_out(h, oa):
        h = rms_norm(h.transpose(0, 2, 1, 3), a_norm_g.reshape(A_HEADS, A_HEAD_DIM))
        return (h.reshape(h.shape[0], h.shape[1], A_WIDTH) * jax.nn.sigmoid(oa)).astype(oa.dtype)

    a_c, oa_c, (qb_c, kb_c, vb_c) = project(h_ctx)
    a_l, oa_l, (qb_l, kb_l, vb_l) = project(h_lat)
    zero = (jnp.zeros((batch, A_HEADS, A_HEAD_DIM, A_HEAD_DIM), jnp.float32),
            jnp.zeros((batch, A_HEADS, A_HEAD_DIM), jnp.float32),
            jnp.zeros((batch, A_HEADS), jnp.float32))
    ha_c, st_f, st_b = mlstm_bidir(*a_c, zero, zero)
    ha_l, _, _ = mlstm_bidir(*a_l, st_f, st_b)
    cos, sin = axial_rope_tables(h_lat.shape[1], B_HEAD_DIM)
    qb_l, kb_l = apply_rope(qb_l, cos, sin), apply_rope(kb_l, cos, sin)
    ob_l = banded_window_attention(qb_l, kb_l, vb_l, kb_c, vb_c, sink)
    y_lat = jnp.concatenate([mlstm_out(ha_l, oa_l), ob_l.astype(oa_l.dtype)], axis=-1) @ w_out
    y_ctx = None
    if need_ctx:
        ob_c = gqa_attend(qb_c, kb_c, vb_c, sink)
        y_ctx = jnp.concatenate([mlstm_out(ha_c, oa_c), ob_c.astype(oa_c.dtype)], axis=-1) @ w_out
    return y_ctx, y_lat


def c_mixer(h_ctx, h_lat, w_in, q_norm_g, k_norm_g, w_out, need_ctx):
    def project(h):
        b, t, _ = h.shape
        q, k, v = jnp.split(h @ w_in, [C_WIDTH, C_WIDTH + C_KV_WIDTH], axis=-1)
        q = rms_norm(q.reshape(b, t, C_HEADS, C_HEAD_DIM), q_norm_g)
        k = rms_norm(k.reshape(b, t, C_KV_HEADS, C_HEAD_DIM), k_norm_g)
        return q, k, v.reshape(b, t, C_KV_HEADS, C_HEAD_DIM)

    q_c, k_c, v_c = project(h_ctx)
    q_l, k_l, v_l = project(h_lat)
    b, t = h_lat.shape[:2]
    cos, sin = axial_rope_tables(t, C_HEAD_DIM)
    q_l, k_l = apply_rope(q_l, cos, sin), apply_rope(k_l, cos, sin)
    k_all = jnp.concatenate([k_c, k_l], axis=1)
    v_all = jnp.concatenate([v_c, v_l], axis=1)
    q_blocks = jnp.moveaxis(q_l.reshape(b, t // BLOCK, BLOCK, C_HEADS, C_HEAD_DIM), 1, 0)
    o = lax.map(lambda qb: gqa_attend(qb, k_all, v_all), q_blocks)
    y_lat = jnp.moveaxis(o, 0, 1).reshape(b, t, C_WIDTH) @ w_out
    y_ctx = gqa_attend(q_c, k_c, v_c) @ w_out if need_ctx else None
    return y_ctx, y_lat


def setup_inputs(seed: int = 0) -> dict:
    key = jax.random.key(seed)
    ks = jax.random.split(key, 24)
    n_even = (DEPTH + 1) // 2
    n_odd = DEPTH // 2
    f32 = jnp.float32
    nrm = lambda k, shape, scale: jax.random.normal(k, shape, f32) * scale
    gain = lambda k, shape: 1.0 + 0.05 * jax.random.normal(k, shape, f32)
    gate_b = jnp.concatenate([
        nrm(ks[9], (n_even, A_HEADS), 0.1),
        jax.random.uniform(ks[10], (n_even, A_HEADS), f32, 3.0, 6.0),
        nrm(ks[11], (n_even, A_HEADS), 0.1),
        jax.random.uniform(ks[12], (n_even, A_HEADS), f32, 3.0, 6.0)], axis=-1)
    return {
        'x': nrm(ks[0], (BATCH, SEQ, D_MODEL), 1.0),
        'c': nrm(ks[1], (BATCH, D_MODEL), 1.0),
        'ctx': nrm(ks[2], (BATCH, CTX_LEN, D_MODEL), 1.0),
        'c_ctx': nrm(ks[3], (D_MODEL,), 1.0),
        'ada_w': nrm(ks[4], (DEPTH, D_MODEL, N_MOD * D_MODEL), 0.5 * D_MODEL ** -0.5),
        'ada_b': nrm(ks[5], (DEPTH, N_MOD * D_MODEL), 0.02),
        'norm1_g': gain(ks[6], (DEPTH, D_MODEL)),
        'norm2_g': gain(ks[7], (DEPTH, D_MODEL)),
        'ab_w_in': nrm(ks[8], (n_even, D_MODEL, AB_IN), D_MODEL ** -0.5),
        'ab_gate_b': gate_b,
        'mlstm_norm_g': gain(ks[13], (n_even, A_WIDTH)),
        'swa_q_norm_g': gain(ks[14], (n_even, B_HEAD_DIM)),
        'swa_k_norm_g': gain(ks[15], (n_even, B_HEAD_DIM)),
        'swa_sink': nrm(ks[16], (n_even, B_HEADS), 0.5),
        'ab_w_out': nrm(ks[17], (n_even, AB_OUT, D_MODEL), AB_OUT ** -0.5),
        'c_w_in': nrm(ks[18], (n_odd, D_MODEL, C_IN), D_MODEL ** -0.5),
        'c_q_norm_g': gain(ks[19], (n_odd, C_HEAD_DIM)),
        'c_k_norm_g': gain(ks[20], (n_odd, C_HEAD_DIM)),
        'c_w_out': nrm(ks[21], (n_odd, C_WIDTH, D_MODEL), C_WIDTH ** -0.5),
        'mlp_w1': nrm(ks[22], (DEPTH, D_MODEL, D_FF), D_MODEL ** -0.5),
        'mlp_w2': nrm(ks[23], (DEPTH, D_FF, D_MODEL), D_FF ** -0.5),
    }


def reference(x, c, ctx, c_ctx, ada_w, ada_b, norm1_g, norm2_g, ab_w_in, ab_gate_b, mlstm_norm_g,
              swa_q_norm_g, swa_k_norm_g, swa_sink, ab_w_out, c_w_in, c_q_norm_g, c_k_norm_g, c_w_out,
              mlp_w1, mlp_w2):
    batch = x.shape[0]
    for layer in range(DEPTH):
        last = layer == DEPTH - 1
        mod_l = (jax.nn.silu(c) @ ada_w[layer] + ada_b[layer]).reshape(batch, 1, N_MOD, D_MODEL)
        mod_c = (jax.nn.silu(c_ctx) @ ada_w[layer] + ada_b[layer]).reshape(1, 1, N_MOD, D_MODEL)
        h_l = modulate(x, norm1_g[layer], mod_l, 0)
        h_c = modulate(ctx, norm1_g[layer], mod_c, 0)
        j = layer // 2
        if layer % 2 == 0:
            y_c, y_l = ab_mixer(h_c, h_l, ab_w_in[j], ab_gate_b[j], mlstm_norm_g[j], swa_q_norm_g[j],
                                swa_k_norm_g[j], swa_sink[j], ab_w_out[j], not last)
        else:
            y_c, y_l = c_mixer(h_c, h_l, c_w_in[j], c_q_norm_g[j], c_k_norm_g[j], c_w_out[j], not last)
        x = x + mod_l[:, :, 2] * y_l
        x = x + mod_l[:, :, 5] * sq_relu_mlp(modulate(x, norm2_g[layer], mod_l, 3), mlp_w1[layer], mlp_w2[layer])
        if not last:
            ctx = ctx + mod_c[:, :, 2] * y_c
            ctx = ctx + mod_c[:, :, 5] * sq_relu_mlp(modulate(ctx, norm2_g[layer], mod_c, 3), mlp_w1[layer], mlp_w2[layer])
    return x
```

```python
import functools

import jax
import jax.numpy as jnp
from jax import lax
from jax.experimental import pallas as pl
from jax.experimental.pallas import tpu as pltpu

F32 = jnp.float32
BF16 = jnp.bfloat16

GRID_W = 64
BLOCK = 128
WINDOW = 128
ROPE_BASE = 10000.0
EPS = 1e-6
N_MOD = 6
MOD_ROWS = 8
LANES = 128

A_HEADS = 4
A_HEAD_DIM = 128
A_WIDTH = A_HEADS * A_HEAD_DIM
A_GATES = 4 * A_HEADS
GATE_ROWS = 8
B_HEADS = 8
B_KV_HEADS = 2
B_HEAD_DIM = 64
B_WIDTH = B_HEADS * B_HEAD_DIM
B_KV_WIDTH = B_KV_HEADS * B_HEAD_DIM
C_HEADS = 8
C_KV_HEADS = 2
C_HEAD_DIM = 128
C_WIDTH = C_HEADS * C_HEAD_DIM
C_KV_WIDTH = C_KV_HEADS * C_HEAD_DIM

VMEM_LIMIT = 56 * 1024 * 1024

NT_DIMS = (((1,), (1,)), ((), ()))
TN_DIMS = (((0,), (0,)), ((), ()))


def _dot(a, b):
    return jnp.dot(a, b, preferred_element_type=F32)


def _dot_nt(a, b):
    return lax.dot_general(a, b, NT_DIMS, preferred_element_type=F32)


def _dot_tn(a, b):
    return lax.dot_general(a, b, TN_DIMS, preferred_element_type=F32)


def _params(*sem):
    return pltpu.CompilerParams(dimension_semantics=sem, vmem_limit_bytes=VMEM_LIMIT)


def _mods_kernel(c_ref, w_ref, b_ref, o_ref):
    cf = c_ref[...]
    s = (cf * jax.nn.sigmoid(cf)).astype(BF16)
    o_ref[0] = _dot(s, w_ref[0].astype(BF16)) + b_ref[0]


def _mods(c, c_ctx, ada_w, ada_b):
    depth, d, _ = ada_w.shape
    batch = c.shape[0]
    rows = -(-(batch + 1) // 8) * 8
    cc = jnp.zeros((rows, d), F32).at[:batch].set(c).at[batch].set(c_ctx)
    out = pl.pallas_call(
        _mods_kernel,
        grid=(depth, N_MOD),
        in_specs=[
            pl.BlockSpec((rows, d), lambda l, j: (0, 0)),
            pl.BlockSpec((1, d, d), lambda l, j: (l, 0, j)),
            pl.BlockSpec((1, 1, d), lambda l, j: (l, 0, j)),
        ],
        out_specs=pl.BlockSpec((1, rows, d), lambda l, j: (l, 0, j)),
        out_shape=jax.ShapeDtypeStruct((depth, rows, N_MOD * d), F32),
        compiler_params=_params("arbitrary", "arbitrary"),
        name="ada_mods",
    )(cc, ada_w, ada_b.reshape(depth, 1, N_MOD * d))
    out = out.reshape(depth, rows, N_MOD, d)
    return jnp.pad(out, ((0, 0), (0, 0), (0, MOD_ROWS - N_MOD), (0, 0)))


def _modulated_norm(x, g, shift, scale):
    ms = jnp.mean(x * x, axis=-1, keepdims=True)
    return (x * lax.rsqrt(ms + EPS) * g) * (1.0 + scale) + shift


def _group_ones(group):
    shift = group.bit_length() - 1
    r = lax.broadcasted_iota(jnp.int32, (LANES, LANES), 0) >> shift
    c = lax.broadcasted_iota(jnp.int32, (LANES, LANES), 1) >> shift
    return jnp.where(r == c, 1.0, 0.0).astype(BF16)


def _split2(a):
    hi = a.astype(BF16)
    lo = (a - hi.astype(F32)).astype(BF16)
    return hi, lo


def _split3(a):
    hi = a.astype(BF16)
    r = a - hi.astype(F32)
    mid = r.astype(BF16)
    lo = (r - mid.astype(F32)).astype(BF16)
    return hi, mid, lo


def _head_norm_rope(xh, gones, gain, cos, sin, group, out_scale):
    hi, lo = _split2(xh * xh)
    ssq = _dot(hi, gones) + _dot(lo, gones)
    xn = xh * lax.rsqrt(ssq * (1.0 / group) + EPS) * gain
    half = group // 2
    if group == LANES:
        rot = pltpu.roll(xn, half, 1)
    else:
        lane = lax.broadcasted_iota(jnp.int32, xn.shape, 1)
        first = (lane & (group - 1)) < half
        rot = jnp.where(first, pltpu.roll(xn, LANES - half, 1), pltpu.roll(xn, half, 1))
    y = xn * cos + rot * sin
    if out_scale != 1.0:
        y = y * out_scale
    return y


def _rope_tables(n_tokens, head_dim, use_rope):
    reps = LANES // head_dim
    if not use_rope:
        return jnp.ones((n_tokens, LANES), F32), jnp.zeros((n_tokens, LANES), F32)
    rows = n_tokens // GRID_W
    row = jnp.broadcast_to(jnp.arange(rows)[:, None], (rows, GRID_W)).reshape(n_tokens).astype(F32)
    col = jnp.broadcast_to(jnp.arange(GRID_W)[None, :], (rows, GRID_W)).reshape(n_tokens).astype(F32)
    pairs = head_dim // 4
    inv_freq = ROPE_BASE ** (-jnp.arange(pairs, dtype=F32) / pairs)
    ang = jnp.concatenate([row[:, None] * inv_freq, col[:, None] * inv_freq], axis=-1)
    cos, sin = jnp.cos(ang), jnp.sin(ang)
    cos_full = jnp.concatenate([cos, cos], axis=-1)
    sin_signed = jnp.concatenate([-sin, sin], axis=-1)
    return jnp.tile(cos_full, (1, reps)), jnp.tile(sin_signed, (1, reps))


def _row_tile(t, target):
    tm = min(t, target)
    assert t % tm == 0
    return tm


def _proj_ab_kernel(x_ref, mod_ref, g_ref, w_ref, wg_ref, gb_ref, cos_ref, sin_ref, qn_ref, kn_ref,
                    qa_ref, ka_ref, va_ref, oa_ref, qb_ref, kb_ref, vb_ref, gr_ref):
    mod = mod_ref[0]
    h = _modulated_norm(x_ref[0], g_ref[...], mod[0:1], mod[1:2]).astype(BF16)

    def mm(lo, n):
        return _dot(h, w_ref[:, lo:lo + n])

    qa_ref[0] = mm(0, A_WIDTH).astype(BF16)
    ka_ref[0] = (mm(A_WIDTH, A_WIDTH) * A_HEAD_DIM ** -0.5).astype(BF16)
    va_ref[0] = mm(2 * A_WIDTH, A_WIDTH).astype(BF16)
    oa_ref[0] = mm(3 * A_WIDTH, A_WIDTH)
    gones = _group_ones(B_HEAD_DIM)
    cos, sin = cos_ref[...], sin_ref[...]
    base = 4 * A_WIDTH
    for j in range(B_WIDTH // LANES):
        y = _head_norm_rope(mm(base + LANES * j, LANES), gones, qn_ref[...], cos, sin,
                            B_HEAD_DIM, B_HEAD_DIM ** -0.5)
        qb_ref[0, :, LANES * j:LANES * (j + 1)] = y.astype(BF16)
    base += B_WIDTH
    for j in range(B_KV_HEADS):
        y = _head_norm_rope(mm(base + LANES * j, LANES), gones, kn_ref[...], cos, sin, B_HEAD_DIM, 1.0)
        kb_ref[0, :, LANES * j:LANES * (j + 1)] = y.astype(BF16)
    base += B_KV_HEADS * LANES
    vb_ref[0] = mm(base, B_KV_HEADS * LANES).astype(BF16)
    gt = _dot_nt(wg_ref[...], h) + gb_ref[...]
    typ = lax.broadcasted_iota(jnp.int32, gt.shape, 0) & (GATE_ROWS - 1)
    log_sig = jnp.minimum(gt, 0.0) - jnp.log1p(jnp.exp(-jnp.abs(gt)))
    gr_ref[0] = jnp.where((typ == 1) | (typ == 3), log_sig, gt)


def _proj_ab(x, mod, g, w, wg, gb, qn, kn, use_rope):
    batch, t, d = x.shape
    tm = _row_tile(t, 256)
    cos, sin = _rope_tables(t, B_HEAD_DIM, use_rope)
    mod_map = (lambda b, i: (b, 0, 0)) if mod.shape[0] == batch else (lambda b, i: (0, 0, 0))
    const = lambda b, i: (0, 0)
    row = lambda width: pl.BlockSpec((1, tm, width), lambda b, i: (b, i, 0))
    kvw = B_KV_HEADS * LANES
    out_shape = [
        jax.ShapeDtypeStruct((batch, t, A_WIDTH), BF16),
        jax.ShapeDtypeStruct((batch, t, A_WIDTH), BF16),
        jax.ShapeDtypeStruct((batch, t, A_WIDTH), BF16),
        jax.ShapeDtypeStruct((batch, t, A_WIDTH), F32),
        jax.ShapeDtypeStruct((batch, t, B_WIDTH), BF16),
        jax.ShapeDtypeStruct((batch, t, kvw), BF16),
        jax.ShapeDtypeStruct((batch, t, kvw), BF16),
        jax.ShapeDtypeStruct((batch, A_HEADS * GATE_ROWS, t), F32),
    ]
    out_specs = [row(A_WIDTH), row(A_WIDTH), row(A_WIDTH), row(A_WIDTH), row(B_WIDTH), row(kvw), row(kvw),
                 pl.BlockSpec((1, A_HEADS * GATE_ROWS, tm), lambda b, i: (b, 0, i))]
    return pl.pallas_call(
        _proj_ab_kernel,
        grid=(batch, t // tm),
        in_specs=[
            row(d),
            pl.BlockSpec((1, MOD_ROWS, d), mod_map),
            pl.BlockSpec((1, d), const),
            pl.BlockSpec(w.shape, const),
            pl.BlockSpec(wg.shape, const),
            pl.BlockSpec(gb.shape, const),
            pl.BlockSpec((tm, LANES), lambda b, i: (i, 0)),
            pl.BlockSpec((tm, LANES), lambda b, i: (i, 0)),
            pl.BlockSpec((1, LANES), const),
            pl.BlockSpec((1, LANES), const),
        ],
        out_specs=out_specs,
        out_shape=out_shape,
        compiler_params=_params("arbitrary", "arbitrary"),
        name="proj_ab",
    )(x, mod, g, w, wg, gb, cos, sin, qn, kn)


def _ab_weights(w_in, gate_b):
    bounds = [A_WIDTH * 4, A_WIDTH * 4 + A_GATES, A_WIDTH * 4 + A_GATES + B_WIDTH,
              A_WIDTH * 4 + A_GATES + B_WIDTH + B_KV_WIDTH]
    wa, wgate, wqb, wkb, wvb = jnp.split(w_in, bounds, axis=1)

    def dup(wk):
        parts = []
        for hh in range(B_KV_HEADS):
            blk = wk[:, hh * B_HEAD_DIM:(hh + 1) * B_HEAD_DIM]
            parts += [blk] * (LANES // B_HEAD_DIM)
        return jnp.concatenate(parts, axis=1)

    w = jnp.concatenate([wa, wqb, dup(wkb), dup(wvb)], axis=1).astype(BF16)
    d = w_in.shape[0]
    wg = wgate.reshape(d, 4, A_HEADS).transpose(2, 1, 0)
    wg = jnp.pad(wg, ((0, 0), (0, GATE_ROWS - 4), (0, 0))).reshape(A_HEADS * GATE_ROWS, d).astype(BF16)
    gb = gate_b.reshape(4, A_HEADS).T
    gb = jnp.pad(gb, ((0, 0), (0, GATE_ROWS - 4))).reshape(A_HEADS * GATE_ROWS, 1).astype(F32)
    return w, wg, gb


def _proj_c_kernel(x_ref, mod_ref, g_ref, w_ref, cos_ref, sin_ref, qn_ref, kn_ref, q_ref, k_ref, v_ref):
    mod = mod_ref[0]
    h = _modulated_norm(x_ref[0], g_ref[...], mod[0:1], mod[1:2]).astype(BF16)

    def mm(lo, n):
        return _dot(h, w_ref[:, lo:lo + n])

    gones = _group_ones(C_HEAD_DIM)
    cos, sin = cos_ref[...], sin_ref[...]
    for j in range(C_HEADS):
        y = _head_norm_rope(mm(LANES * j, LANES), gones, qn_ref[...], cos, sin, C_HEAD_DIM, C_HEAD_DIM ** -0.5)
        q_ref[0, :, LANES * j:LANES * (j + 1)] = y.astype(BF16)
    for j in range(C_KV_HEADS):
        y = _head_norm_rope(mm(C_WIDTH + LANES * j, LANES), gones, kn_ref[...], cos, sin, C_HEAD_DIM, 1.0)
        k_ref[0, :, LANES * j:LANES * (j + 1)] = y.astype(BF16)
    v_ref[0] = mm(C_WIDTH + C_KV_WIDTH, C_KV_WIDTH).astype(BF16)


def _proj_c(x, mod, g, w, qn, kn, use_rope):
    batch, t, d = x.shape
    tm = _row_tile(t, 256)
    cos, sin = _rope_tables(t, C_HEAD_DIM, use_rope)
    mod_map = (lambda b, i: (b, 0, 0)) if mod.shape[0] == batch else (lambda b, i: (0, 0, 0))
    const = lambda b, i: (0, 0)
    row = lambda width: pl.BlockSpec((1, tm, width), lambda b, i: (b, i, 0))
    return pl.pallas_call(
        _proj_c_kernel,
        grid=(batch, t // tm),
        in_specs=[
            row(d),
            pl.BlockSpec((1, MOD_ROWS, d), mod_map),
            pl.BlockSpec((1, d), const),
            pl.BlockSpec(w.shape, const),
            pl.BlockSpec((tm, LANES), lambda b, i: (i, 0)),
            pl.BlockSpec((tm, LANES), lambda b, i: (i, 0)),
            pl.BlockSpec((1, LANES), const),
            pl.BlockSpec((1, LANES), const),
        ],
        out_specs=[row(C_WIDTH), row(C_KV_WIDTH), row(C_KV_WIDTH)],
        out_shape=[jax.ShapeDtypeStruct((batch, t, C_WIDTH), BF16),
                   jax.ShapeDtypeStruct((batch, t, C_KV_WIDTH), BF16),
                   jax.ShapeDtypeStruct((batch, t, C_KV_WIDTH), BF16)],
        compiler_params=_params("arbitrary", "arbitrary"),
        name="proj_c",
    )(x, mod, g, w, cos, sin, qn, kn)


def _mlstm_kernel(ql_ref, kl_ref, vl_ref, oal_ref, grl_ref, qc_ref, kc_ref, vc_ref, oac_ref, grc_ref, ng_ref,
                  hl_ref, hc_ref,
                  qs, ks, vs, gin, rows, cols, stats, cst, mst, cstate, *, t_ctx, t_lat):
    L = BLOCK
    ncc = t_ctx // L
    ncl = t_lat // L
    nc = ncc + ncl

    qs[0:t_ctx] = qc_ref[0]
    qs[t_ctx:] = ql_ref[0]
    ks[0:t_ctx] = kc_ref[0]
    ks[t_ctx:] = kl_ref[0]
    vs[0:t_ctx] = vc_ref[0]
    vs[t_ctx:] = vl_ref[0]
    gin[:, 0:t_ctx] = grc_ref[0]
    gin[:, t_ctx:] = grl_ref[0]

    ri = lax.broadcasted_iota(jnp.int32, (L, L), 0)
    ci = lax.broadcasted_iota(jnp.int32, (L, L), 1)
    lower = ci <= ri
    upper = ci >= ri
    cum_mat = jnp.concatenate([jnp.where(upper, 1.0, 0.0), jnp.where(lower, 1.0, 0.0)], axis=1).astype(BF16)
    ones_blk = jnp.ones((L, L), BF16)

    def chunk(i):
        return pl.ds(pl.multiple_of(i * L, L), L)

    def gate_stats(i, _):
        g = gin[:, chunk(i)]
        hi, mid, lo = _split3(g)
        cum = _dot(hi, cum_mat) + _dot(mid, cum_mat) + _dot(lo, cum_mat)
        li_f, li_b = g[0:1], g[2:3]
        bcum_f = cum[1:2, 0:L]
        bsuf_b = cum[3:4, L:2 * L]
        bl_f = bcum_f[:, L - 1:L]
        bl_b = bsuf_b[:, 0:1]
        wl_f = bl_f - bcum_f + li_f
        wl_b = bl_b - bsuf_b + li_b
        zero = jnp.zeros((1, L), F32)
        r8 = jnp.concatenate([bcum_f, bsuf_b, li_f - bcum_f, li_b - bsuf_b, wl_f, wl_b, zero, zero], axis=0)
        rows[:, chunk(i)] = r8
        cols[chunk(i), :] = jnp.concatenate([r8, jnp.zeros((L - 8, L), F32)], axis=0).T
        mx_f = jnp.max(wl_f, axis=1, keepdims=True)
        mx_b = jnp.max(wl_b, axis=1, keepdims=True)
        st = jnp.concatenate([jnp.broadcast_to(v, (1, L)) for v in (bl_f, bl_b, mx_f, mx_b)]
                             + [jnp.zeros((4, L), F32)], axis=0)
        stats[i] = st
        return 0

    lax.fori_loop(0, nc, gate_stats, 0)

    cstate[...] = jnp.zeros_like(cstate)

    def scan_step(j, carry):
        new = []
        for d in range(2):
            m = carry[d]
            if d == 0:
                i = j
            else:
                i = jnp.where(j < ncc, ncc - 1 - j, nc - 1 + ncc - j)
            c_old = cstate[d]
            cst[d, i] = c_old.astype(BF16)
            mst[d, i] = jnp.broadcast_to(m, (8, L))
            st = stats[i]
            bl = st[d:d + 1, 0:1]
            mx = st[2 + d:3 + d, 0:1]
            m_new = jnp.maximum(bl + m, mx)
            carry_w = jnp.exp(bl + m - m_new)
            wl = cols[chunk(i), 4 + d:5 + d]
            w = jnp.exp(wl - m_new)
            kw = (ks[chunk(i), :].astype(F32) * w).astype(BF16)
            vaug = jnp.concatenate([vs[chunk(i), :], ones_blk], axis=1)
            cstate[d] = carry_w * c_old + _dot_tn(kw, vaug)
            new.append(m_new)
        return tuple(new)

    m0 = jnp.zeros((1, 1), F32)
    lax.fori_loop(0, nc, scan_step, (m0, m0))

    ng = ng_ref[...]

    def emit(i, oa, out_ref, out_rows):
        q = qs[chunk(i), :]
        k = ks[chunk(i), :]
        vaug = jnp.concatenate([vs[chunk(i), :], ones_blk], axis=1)
        qk = _dot_nt(q, k)
        colblk = cols[chunk(i), :]
        rowblk = rows[:, chunk(i)]
        h = None
        for d in range(2):
            bc = colblk[:, d:d + 1]
            rb = rowblk[2 + d:3 + d, :]
            mask = lower if d == 0 else upper
            dl = bc + rb
            mintra = jnp.max(jnp.where(mask, dl, -jnp.inf), axis=1, keepdims=True)
            m = mst[d, i][0:1, 0:1]
            inter_log = bc + m
            mt = jnp.maximum(inter_log, mintra)
            dm = jnp.where(mask, jnp.exp(dl - mt), 0.0)
            s = (qk * dm).astype(BF16)
            iw = jnp.exp(inter_log - mt)
            nd = iw * _dot(q, cst[d, i]) + _dot(s, vaug)
            hd = nd[:, 0:L] / jnp.maximum(jnp.abs(nd[:, L:2 * L]), jnp.exp(-mt))
            h = hd if h is None else h + hd
        ms = jnp.mean(h * h, axis=-1, keepdims=True)
        y = (h * lax.rsqrt(ms + EPS) * ng) * jax.nn.sigmoid(oa)
        out_ref[0, out_rows, :] = y.astype(BF16)

    def emit_ctx(i, _):
        emit(i, oac_ref[0, chunk(i), :], hc_ref, chunk(i))
        return 0

    def emit_lat(i, _):
        emit(i + ncc, oal_ref[0, chunk(i), :], hl_ref, chunk(i))
        return 0

    lax.fori_loop(0, ncc, emit_ctx, 0)
    lax.fori_loop(0, ncl, emit_lat, 0)


def _mlstm(qa_l, ka_l, va_l, oa_l, gr_l, qa_c, ka_c, va_c, oa_c, gr_c, norm_g):
    batch, t_lat, _ = qa_l.shape
    t_ctx = qa_c.shape[1]
    t_all = t_ctx + t_lat
    nc = t_all // BLOCK
    head = lambda t: pl.BlockSpec((1, t, A_HEAD_DIM), lambda b, h: (b, 0, h))
    gates = lambda t: pl.BlockSpec((1, GATE_ROWS, t), lambda b, h: (b, h, 0))
    return pl.pallas_call(
        functools.partial(_mlstm_kernel, t_ctx=t_ctx, t_lat=t_lat),
        grid=(batch, A_HEADS),
        in_specs=[head(t_lat), head(t_lat), head(t_lat), head(t_lat), gates(t_lat),
                  head(t_ctx), head(t_ctx), head(t_ctx), head(t_ctx), gates(t_ctx),
                  pl.BlockSpec((1, A_HEAD_DIM), lambda b, h: (0, h))],
        out_specs=[head(t_lat), head(t_ctx)],
        out_shape=[jax.ShapeDtypeStruct((batch, t_lat, A_WIDTH), BF16),
                   jax.ShapeDtypeStruct((batch, t_ctx, A_WIDTH), BF16)],
        scratch_shapes=[
            pltpu.VMEM((t_all, A_HEAD_DIM), BF16),
            pltpu.VMEM((t_all, A_HEAD_DIM), BF16),
            pltpu.VMEM((t_all, A_HEAD_DIM), BF16),
            pltpu.VMEM((GATE_ROWS, t_all), F32),
            pltpu.VMEM((GATE_ROWS, t_all), F32),
            pltpu.VMEM((t_all, LANES), F32),
            pltpu.VMEM((nc, 8, LANES), F32),
            pltpu.VMEM((2, nc, A_HEAD_DIM, 2 * A_HEAD_DIM), BF16),
            pltpu.VMEM((2, nc, 8, LANES), F32),
            pltpu.VMEM((2, A_HEAD_DIM, 2 * A_HEAD_DIM), F32),
        ],
        compiler_params=_params("arbitrary", "arbitrary"),
        name="mlstm",
    )(qa_l, ka_l, va_l, oa_l, gr_l, qa_c, ka_c, va_c, oa_c, gr_c, norm_g.reshape(1, A_WIDTH))


def _swa_kernel(sink_ref, q_ref, *refs, t_lat, has_window):
    if has_window:
        kp_ref, kc_ref, kn_ref, vp_ref, vc_ref, vn_ref, kx_ref, vx_ref, o_ref = refs
    else:
        kx_ref, vx_ref, o_ref = refs
    L = BLOCK
    i = pl.program_id(1)
    group = B_HEADS // B_KV_HEADS
    rows = group * L
    q = q_ref[0]
    lane = lax.broadcasted_iota(jnp.int32, (L, LANES), 1)
    lo = lane < B_HEAD_DIM
    zero = jnp.zeros((L, LANES), BF16)
    if has_window:
        t = lax.broadcasted_iota(jnp.int32, (rows, 1), 0) & (L - 1)
        kpos = (i - 1) * L + lax.broadcasted_iota(jnp.int32, (1, 3 * L), 1)
        qpos = i * L + t
        valid = (jnp.abs(kpos - qpos) <= WINDOW) & (kpos >= 0) & (kpos < t_lat)
    for kvh in range(B_KV_HEADS):
        sl = slice(kvh * LANES, (kvh + 1) * LANES)
        qa = q[:, (2 * kvh) * LANES:(2 * kvh + 1) * LANES]
        qb = q[:, (2 * kvh + 1) * LANES:(2 * kvh + 2) * LANES]
        q4 = jnp.concatenate([jnp.where(lo, qa, zero), jnp.where(lo, zero, qa),
                              jnp.where(lo, qb, zero), jnp.where(lo, zero, qb)], axis=0)
        sink = jnp.concatenate([jnp.full((L, 1), sink_ref[kvh * group + g], F32) for g in range(group)], axis=0)
        s_x = _dot_nt(q4, kx_ref[0, :, sl])
        m = jnp.maximum(jnp.max(s_x, axis=1, keepdims=True), sink)
        if has_window:
            kw = jnp.concatenate([kp_ref[0, :, sl], kc_ref[0, :, sl], kn_ref[0, :, sl]], axis=0)
            vw = jnp.concatenate([vp_ref[0, :, sl], vc_ref[0, :, sl], vn_ref[0, :, sl]], axis=0)
            s_w = jnp.where(valid, _dot_nt(q4, kw), -jnp.inf)
            m = jnp.maximum(m, jnp.max(s_w, axis=1, keepdims=True))
            e_w = jnp.exp(s_w - m)
        e_x = jnp.exp(s_x - m)
        denom = jnp.sum(e_x, axis=1, keepdims=True) + jnp.exp(sink - m)
        if has_window:
            denom = denom + jnp.sum(e_w, axis=1, keepdims=True)
        o4 = _dot((e_x / denom).astype(BF16), vx_ref[0, :, sl])
        if has_window:
            o4 = o4 + _dot((e_w / denom).astype(BF16), vw)
        o_ref[0, :, (2 * kvh) * LANES:(2 * kvh + 1) * LANES] = jnp.where(lo, o4[0:L], o4[L:2 * L]).astype(BF16)
        o_ref[0, :, (2 * kvh + 1) * LANES:(2 * kvh + 2) * LANES] = (
            jnp.where(lo, o4[2 * L:3 * L], o4[3 * L:4 * L]).astype(BF16))


def _swa(sink, q, k, v, k_ctx, v_ctx, has_window):
    batch, t, _ = q.shape
    t_ctx = k_ctx.shape[1]
    nb = t // BLOCK
    kvw = B_KV_HEADS * LANES
    blk = lambda fn: pl.BlockSpec((1, BLOCK, kvw), fn)
    prev = lambda b, i: (b, jnp.maximum(i - 1, 0), 0)
    cur = lambda b, i: (b, i, 0)
    nxt = lambda b, i: (b, jnp.minimum(i + 1, nb - 1), 0)
    ctx_spec = pl.BlockSpec((1, t_ctx, kvw), lambda b, i: (b, 0, 0))
    in_specs = [pl.BlockSpec(memory_space=pltpu.SMEM), pl.BlockSpec((1, BLOCK, B_WIDTH), cur)]
    args = [sink, q]
    if has_window:
        in_specs += [blk(prev), blk(cur), blk(nxt), blk(prev), blk(cur), blk(nxt)]
        args += [k, k, k, v, v, v]
    in_specs += [ctx_spec, ctx_spec]
    args += [k_ctx, v_ctx]
    return pl.pallas_call(
        functools.partial(_swa_kernel, t_lat=t, has_window=has_window),
        grid=(batch, nb),
        in_specs=in_specs,
        out_specs=pl.BlockSpec((1, BLOCK, B_WIDTH), cur),
        out_shape=jax.ShapeDtypeStruct((batch, t, B_WIDTH), BF16),
        compiler_params=_params("arbitrary", "arbitrary"),
        name="swa" if has_window else "swa_ctx",
    )(*args)


def _flash_kernel(q_ref, k_ref, v_ref, o_ref, *, tk):
    tq = q_ref.shape[1]
    group = C_HEADS // C_KV_HEADS
    q = q_ref[0]
    q4 = jnp.concatenate([q[:, g * LANES:(g + 1) * LANES] for g in range(group)], axis=0)
    nk = k_ref.shape[1] // tk

    def step(j, carry):
        m, l, acc = carry
        ks = pl.ds(pl.multiple_of(j * tk, tk), tk)
        s = _dot_nt(q4, k_ref[0, ks, :])
        m_new = jnp.maximum(m, jnp.max(s, axis=1, keepdims=True))
        alpha = jnp.exp(m - m_new)
        p = jnp.exp(s - m_new)
        l = alpha * l + jnp.sum(p, axis=1, keepdims=True)
        acc = alpha * acc + _dot(p.astype(BF16), v_ref[0, ks, :])
        return m_new, l, acc

    rows = group * tq
    init = (jnp.full((rows, 1), -jnp.inf, F32), jnp.zeros((rows, 1), F32), jnp.zeros((rows, LANES), F32))
    _, l, acc = lax.fori_loop(0, nk, step, init)
    out = acc / l
    for g in range(group):
        o_ref[0, :, g * LANES:(g + 1) * LANES] = out[g * tq:(g + 1) * tq].astype(BF16)


def _flash(q, k, v):
    batch, t, _ = q.shape
    tk_all = k.shape[1]
    tq = _row_tile(t, 256)
    tk = 256
    assert tk_all % tk == 0
    gw = (C_HEADS // C_KV_HEADS) * C_HEAD_DIM
    return pl.pallas_call(
        functools.partial(_flash_kernel, tk=tk),
        grid=(batch, C_KV_HEADS, t // tq),
        in_specs=[pl.BlockSpec((1, tq, gw), lambda b, h, i: (b, i, h)),
                  pl.BlockSpec((1, tk_all, C_HEAD_DIM), lambda b, h, i: (b, 0, h)),
                  pl.BlockSpec((1, tk_all, C_HEAD_DIM), lambda b, h, i: (b, 0, h))],
        out_specs=pl.BlockSpec((1, tq, gw), lambda b, h, i: (b, i, h)),
        out_shape=jax.ShapeDtypeStruct((batch, t, C_WIDTH), BF16),
        compiler_params=_params("arbitrary", "arbitrary", "arbitrary"),
        name="flash_c",
    )(q, k, v)


def _out_mlp_kernel(*refs, n_mix, ff_chunk):
    x_ref, mod_ref, g_ref = refs[0:3]
    mix_refs = refs[3:3 + n_mix]
    wo_refs = refs[3 + n_mix:3 + 2 * n_mix]
    w1_ref, w2_ref, o_ref = refs[3 + 2 * n_mix:]
    mod = mod_ref[0]
    y = None
    for a_ref, w_ref in zip(mix_refs, wo_refs):
        part = _dot(a_ref[0], w_ref[...])
        y = part if y is None else y + part
    x1 = x_ref[0] + mod[2:3] * y
    h = _modulated_norm(x1, g_ref[...], mod[3:4], mod[4:5]).astype(BF16)
    d_ff = w1_ref.shape[1]
    acc = None
    for c in range(d_ff // ff_chunk):
        a = _dot(h, w1_ref[:, c * ff_chunk:(c + 1) * ff_chunk])
        a = jnp.square(jnp.maximum(a, 0.0)).astype(BF16)
        part = _dot(a, w2_ref[c * ff_chunk:(c + 1) * ff_chunk, :])
        acc = part if acc is None else acc + part
    o_ref[0] = x1 + mod[5:6] * acc


def _out_mlp(x, mod, g, mixes, w_outs, w1, w2):
    batch, t, d = x.shape
    tm = _row_tile(t, 512)
    mod_map = (lambda b, i: (b, 0, 0)) if mod.shape[0] == batch else (lambda b, i: (0, 0, 0))
    const = lambda b, i: (0, 0)
    resident = lambda a: pl.BlockSpec(a.shape, const, pipeline_mode=pl.Buffered(1))
    row = lambda width: pl.BlockSpec((1, tm, width), lambda b, i: (b, i, 0))
    return pl.pallas_call(
        functools.partial(_out_mlp_kernel, n_mix=len(mixes), ff_chunk=1024),
        grid=(batch, t // tm),
        in_specs=([row(d), pl.BlockSpec((1, MOD_ROWS, d), mod_map), pl.BlockSpec((1, d), const)]
                  + [row(a.shape[2]) for a in mixes] + [resident(w) for w in w_outs]
                  + [resident(w1), resident(w2)]),
        out_specs=row(d),
        out_shape=jax.ShapeDtypeStruct((batch, t, d), F32),
        compiler_params=_params("arbitrary", "arbitrary"),
        name="out_mlp",
    )(x, mod, g, *mixes, *w_outs, w1, w2)


def kernel(x, c, ctx, c_ctx, ada_w, ada_b, norm1_g, norm2_g, ab_w_in, ab_gate_b, mlstm_norm_g, swa_q_norm_g,
           swa_k_norm_g, swa_sink, ab_w_out, c_w_in, c_q_norm_g, c_k_norm_g, c_w_out, mlp_w1, mlp_w2):
    depth = ada_w.shape[0]
    batch, _, d = x.shape
    mods = _mods(c, c_ctx, ada_w, ada_b)
    for layer in range(depth):
        last = layer == depth - 1
        mod_l = mods[layer, :batch]
        mod_c = mods[layer, batch:batch + 1]
        g1 = norm1_g[layer].reshape(1, d)
        g2 = norm2_g[layer].reshape(1, d)
        w1 = mlp_w1[layer].astype(BF16)
        w2 = mlp_w2[layer].astype(BF16)
        j = layer // 2
        if layer % 2 == 0:
            w, wg, gb = _ab_weights(ab_w_in[j], ab_gate_b[j])
            qn = jnp.tile(swa_q_norm_g[j], LANES // B_HEAD_DIM).reshape(1, LANES)
            kn = jnp.tile(swa_k_norm_g[j], LANES // B_HEAD_DIM).reshape(1, LANES)
            qa_l, ka_l, va_l, oa_l, qb_l, kb_l, vb_l, gr_l = _proj_ab(x, mod_l, g1, w, wg, gb, qn, kn, True)
            qa_c, ka_c, va_c, oa_c, qb_c, kb_c, vb_c, gr_c = _proj_ab(ctx, mod_c, g1, w, wg, gb, qn, kn, False)
            ha_l, ha_c = _mlstm(qa_l, ka_l, va_l, oa_l, gr_l, qa_c, ka_c, va_c, oa_c, gr_c, mlstm_norm_g[j])
            ob_l = _swa(swa_sink[j], qb_l, kb_l, vb_l, kb_c, vb_c, True)
            w_out = ab_w_out[j].astype(BF16)
            w_outs = [w_out[:A_WIDTH], w_out[A_WIDTH:]]
            x = _out_mlp(x, mod_l, g2, [ha_l, ob_l], w_outs, w1, w2)
            if not last:
                ob_c = _swa(swa_sink[j], qb_c, kb_c, vb_c, kb_c, vb_c, False)
                ctx = _out_mlp(ctx, mod_c, g2, [ha_c, ob_c], w_outs, w1, w2)
        else:
            w = c_w_in[j].astype(BF16)
            qn = c_q_norm_g[j].reshape(1, LANES)
            kn = c_k_norm_g[j].reshape(1, LANES)
            q_l, k_l, v_l = _proj_c(x, mod_l, g1, w, qn, kn, True)
            q_c, k_c, v_c = _proj_c(ctx, mod_c, g1, w, qn, kn, False)
            k_all = jnp.concatenate([k_c, k_l], axis=1)
            v_all = jnp.concatenate([v_c, v_l], axis=1)
            w_out = c_w_out[j].astype(BF16)
            o_l = _flash(q_l, k_all, v_all)
            x = _out_mlp(x, mod_l, g2, [o_l], [w_out], w1, w2)
            if not last:
                o_c = _flash(q_c, k_c, v_c)
                ctx = _out_mlp(ctx, mod_c, g2, [o_c], [w_out], w1, w2)
    return x
```

```python
import functools

import jax
import jax.numpy as jnp
from jax import lax
from jax.experimental import pallas as pl
from jax.experimental.pallas import tpu as pltpu

F32 = jnp.float32
BF16 = jnp.bfloat16

GRID_W = 64
BLOCK = 128
WINDOW = 128
ROPE_BASE = 10000.0
EPS = 1e-6
LOG2_E = 1.4426950408889634
PROJ_ROWS = 512
MLP_ROWS = 512
FLASH_ROWS = 128
N_MOD = 6
MOD_ROWS = 8
LANES = 128

A_HEADS = 4
A_HEAD_DIM = 128
A_WIDTH = A_HEADS * A_HEAD_DIM
A_GATES = 4 * A_HEADS
GATE_ROWS = 8
B_HEADS = 8
B_KV_HEADS = 2
B_HEAD_DIM = 64
B_WIDTH = B_HEADS * B_HEAD_DIM
B_KV_WIDTH = B_KV_HEADS * B_HEAD_DIM
C_HEADS = 8
C_KV_HEADS = 2
C_HEAD_DIM = 128
C_WIDTH = C_HEADS * C_HEAD_DIM
C_KV_WIDTH = C_KV_HEADS * C_HEAD_DIM

VMEM_LIMIT = 56 * 1024 * 1024

NT_DIMS = (((1,), (1,)), ((), ()))
TN_DIMS = (((0,), (0,)), ((), ()))


def _dot(a, b):
    return jnp.dot(a, b, preferred_element_type=F32)


def _dot_nt(a, b):
    return lax.dot_general(a, b, NT_DIMS, preferred_element_type=F32)


def _dot_tn(a, b):
    return lax.dot_general(a, b, TN_DIMS, preferred_element_type=F32)


def _params(*sem):
    return pltpu.CompilerParams(dimension_semantics=sem, vmem_limit_bytes=VMEM_LIMIT)


def _mods_kernel(c_ref, w_ref, b_ref, o_ref):
    cf = c_ref[...]
    s = (cf * jax.nn.sigmoid(cf)).astype(BF16)
    o_ref[0] = _dot(s, w_ref[0].astype(BF16)) + b_ref[0]


def _mods(c, c_ctx, ada_w, ada_b):
    depth, d, _ = ada_w.shape
    batch = c.shape[0]
    rows = -(-(batch + 1) // 8) * 8
    cc = jnp.zeros((rows, d), F32).at[:batch].set(c).at[batch].set(c_ctx)
    out = pl.pallas_call(
        _mods_kernel,
        grid=(depth, N_MOD),
        in_specs=[
            pl.BlockSpec((rows, d), lambda l, j: (0, 0)),
            pl.BlockSpec((1, d, d), lambda l, j: (l, 0, j)),
            pl.BlockSpec((1, 1, d), lambda l, j: (l, 0, j)),
        ],
        out_specs=pl.BlockSpec((1, rows, d), lambda l, j: (l, 0, j)),
        out_shape=jax.ShapeDtypeStruct((depth, rows, N_MOD * d), F32),
        compiler_params=_params("arbitrary", "arbitrary"),
        name="ada_mods",
    )(cc, ada_w, ada_b.reshape(depth, 1, N_MOD * d))
    out = out.reshape(depth, rows, N_MOD, d)
    return jnp.pad(out, ((0, 0), (0, 0), (0, MOD_ROWS - N_MOD), (0, 0)))


def _modulated_norm(x, g, shift, scale):
    ms = jnp.mean(x * x, axis=-1, keepdims=True)
    return (x * lax.rsqrt(ms + EPS) * g) * (1.0 + scale) + shift


def _split3(a):
    hi = a.astype(BF16)
    r = a - hi.astype(F32)
    mid = r.astype(BF16)
    lo = (r - mid.astype(F32)).astype(BF16)
    return hi, mid, lo


def _head_norm_rope(xh, gain, cos, sin, group, out_scale):
    sq = xh * xh
    half = group // 2
    if group == LANES:
        ssq = jnp.sum(sq, axis=-1, keepdims=True)
    else:
        assert 2 * group == LANES
        lane = lax.broadcasted_iota(jnp.int32, xh.shape, 1)
        left = lane < group
        ssq = jnp.where(left, jnp.sum(jnp.where(left, sq, 0.0), axis=-1, keepdims=True),
                        jnp.sum(jnp.where(left, 0.0, sq), axis=-1, keepdims=True))
    xn = xh * lax.rsqrt(ssq * (1.0 / group) + EPS) * gain
    if group == LANES:
        rot = pltpu.roll(xn, half, 1)
    else:
        first = (lane & (group - 1)) < half
        rot = jnp.where(first, pltpu.roll(xn, LANES - half, 1), pltpu.roll(xn, half, 1))
    y = xn * cos + rot * sin
    if out_scale != 1.0:
        y = y * out_scale
    return y


def _rope_tables(n_tokens, head_dim, use_rope):
    reps = LANES // head_dim
    if not use_rope:
        return jnp.ones((n_tokens, LANES), F32), jnp.zeros((n_tokens, LANES), F32)
    rows = n_tokens // GRID_W
    row = jnp.broadcast_to(jnp.arange(rows)[:, None], (rows, GRID_W)).reshape(n_tokens).astype(F32)
    col = jnp.broadcast_to(jnp.arange(GRID_W)[None, :], (rows, GRID_W)).reshape(n_tokens).astype(F32)
    pairs = head_dim // 4
    inv_freq = ROPE_BASE ** (-jnp.arange(pairs, dtype=F32) / pairs)
    ang = jnp.concatenate([row[:, None] * inv_freq, col[:, None] * inv_freq], axis=-1)
    cos, sin = jnp.cos(ang), jnp.sin(ang)
    cos_full = jnp.concatenate([cos, cos], axis=-1)
    sin_signed = jnp.concatenate([-sin, sin], axis=-1)
    return jnp.tile(cos_full, (1, reps)), jnp.tile(sin_signed, (1, reps))


def _row_tile(t, target):
    tm = min(t, target)
    assert t % tm == 0
    return tm


def _proj_ab_kernel(x_ref, mod_ref, g_ref, w_ref, wg_ref, gb_ref, cos_ref, sin_ref, qn_ref, kn_ref,
                    qa_ref, ka_ref, va_ref, oa_ref, qb_ref, kb_ref, vb_ref, gr_ref):
    mod = mod_ref[0]
    h = _modulated_norm(x_ref[0], g_ref[...], mod[0:1], mod[1:2]).astype(BF16)

    def mm(lo, n):
        return _dot(h, w_ref[:, lo:lo + n])

    qa_ref[0] = mm(0, A_WIDTH).astype(BF16)
    ka_ref[0] = (mm(A_WIDTH, A_WIDTH) * A_HEAD_DIM ** -0.5).astype(BF16)
    va_ref[0] = mm(2 * A_WIDTH, A_WIDTH).astype(BF16)
    oa_ref[0] = mm(3 * A_WIDTH, A_WIDTH)
    cos, sin = cos_ref[...], sin_ref[...]
    base = 4 * A_WIDTH
    for j in range(B_WIDTH // LANES):
        y = _head_norm_rope(mm(base + LANES * j, LANES), qn_ref[...], cos, sin, B_HEAD_DIM, B_HEAD_DIM ** -0.5)
        qb_ref[0, :, LANES * j:LANES * (j + 1)] = y.astype(BF16)
    base += B_WIDTH
    for j in range(B_KV_HEADS):
        y = _head_norm_rope(mm(base + LANES * j, LANES), kn_ref[...], cos, sin, B_HEAD_DIM, 1.0)
        kb_ref[0, :, LANES * j:LANES * (j + 1)] = y.astype(BF16)
    base += B_KV_HEADS * LANES
    vb_ref[0] = mm(base, B_KV_HEADS * LANES).astype(BF16)
    gt = _dot_nt(wg_ref[...], h) + gb_ref[...]
    typ = lax.broadcasted_iota(jnp.int32, gt.shape, 0) & (GATE_ROWS - 1)
    log_sig = jnp.minimum(gt, 0.0) - jnp.log1p(jnp.exp(-jnp.abs(gt)))
    gr_ref[0] = jnp.where((typ == 1) | (typ == 3), log_sig, gt)


def _proj_ab(x, mod, g, w, wg, gb, qn, kn, use_rope):
    batch, t, d = x.shape
    tm = _row_tile(t, PROJ_ROWS)
    cos, sin = _rope_tables(t, B_HEAD_DIM, use_rope)
    mod_map = (lambda b, i: (b, 0, 0)) if mod.shape[0] == batch else (lambda b, i: (0, 0, 0))
    const = lambda b, i: (0, 0)
    row = lambda width: pl.BlockSpec((1, tm, width), lambda b, i: (b, i, 0))
    kvw = B_KV_HEADS * LANES
    out_shape = [
        jax.ShapeDtypeStruct((batch, t, A_WIDTH), BF16),
        jax.ShapeDtypeStruct((batch, t, A_WIDTH), BF16),
        jax.ShapeDtypeStruct((batch, t, A_WIDTH), BF16),
        jax.ShapeDtypeStruct((batch, t, A_WIDTH), F32),
        jax.ShapeDtypeStruct((batch, t, B_WIDTH), BF16),
        jax.ShapeDtypeStruct((batch, t, kvw), BF16),
        jax.ShapeDtypeStruct((batch, t, kvw), BF16),
        jax.ShapeDtypeStruct((batch, A_HEADS * GATE_ROWS, t), F32),
    ]
    out_specs = [row(A_WIDTH), row(A_WIDTH), row(A_WIDTH), row(A_WIDTH), row(B_WIDTH), row(kvw), row(kvw),
                 pl.BlockSpec((1, A_HEADS * GATE_ROWS, tm), lambda b, i: (b, 0, i))]
    return pl.pallas_call(
        _proj_ab_kernel,
        grid=(batch, t // tm),
        in_specs=[
            row(d),
            pl.BlockSpec((1, MOD_ROWS, d), mod_map),
            pl.BlockSpec((1, d), const),
            pl.BlockSpec(w.shape, const),
            pl.BlockSpec(wg.shape, const),
            pl.BlockSpec(gb.shape, const),
            pl.BlockSpec((tm, LANES), lambda b, i: (i, 0)),
            pl.BlockSpec((tm, LANES), lambda b, i: (i, 0)),
            pl.BlockSpec((1, LANES), const),
            pl.BlockSpec((1, LANES), const),
        ],
        out_specs=out_specs,
        out_shape=out_shape,
        compiler_params=_params("arbitrary", "arbitrary"),
        name="proj_ab",
    )(x, mod, g, w, wg, gb, cos, sin, qn, kn)


def _ab_weights(w_in, gate_b):
    bounds = [A_WIDTH * 4, A_WIDTH * 4 + A_GATES, A_WIDTH * 4 + A_GATES + B_WIDTH,
              A_WIDTH * 4 + A_GATES + B_WIDTH + B_KV_WIDTH]
    wa, wgate, wqb, wkb, wvb = jnp.split(w_in, bounds, axis=1)

    def dup(wk):
        parts = []
        for hh in range(B_KV_HEADS):
            blk = wk[:, hh * B_HEAD_DIM:(hh + 1) * B_HEAD_DIM]
            parts += [blk] * (LANES // B_HEAD_DIM)
        return jnp.concatenate(parts, axis=1)

    w = jnp.concatenate([wa, wqb, dup(wkb), dup(wvb)], axis=1).astype(BF16)
    d = w_in.shape[0]
    wg = wgate.reshape(d, 4, A_HEADS).transpose(2, 1, 0)
    wg = jnp.pad(wg, ((0, 0), (0, GATE_ROWS - 4), (0, 0))).reshape(A_HEADS * GATE_ROWS, d).astype(BF16)
    gb = gate_b.reshape(4, A_HEADS).T
    gb = jnp.pad(gb, ((0, 0), (0, GATE_ROWS - 4))).reshape(A_HEADS * GATE_ROWS, 1).astype(F32)
    return w, wg, gb


def _proj_c_kernel(x_ref, mod_ref, g_ref, w_ref, cos_ref, sin_ref, qn_ref, kn_ref, q_ref, k_ref, v_ref):
    mod = mod_ref[0]
    h = _modulated_norm(x_ref[0], g_ref[...], mod[0:1], mod[1:2]).astype(BF16)

    def mm(lo, n):
        return _dot(h, w_ref[:, lo:lo + n])

    cos, sin = cos_ref[...], sin_ref[...]
    q_scale = C_HEAD_DIM ** -0.5 * LOG2_E
    for j in range(C_HEADS):
        y = _head_norm_rope(mm(LANES * j, LANES), qn_ref[...], cos, sin, C_HEAD_DIM, q_scale)
        q_ref[0, :, LANES * j:LANES * (j + 1)] = y.astype(BF16)
    for j in range(C_KV_HEADS):
        y = _head_norm_rope(mm(C_WIDTH + LANES * j, LANES), kn_ref[...], cos, sin, C_HEAD_DIM, 1.0)
        k_ref[0, :, LANES * j:LANES * (j + 1)] = y.astype(BF16)
    v_ref[0] = mm(C_WIDTH + C_KV_WIDTH, C_KV_WIDTH).astype(BF16)


def _proj_c(x, mod, g, w, qn, kn, use_rope):
    batch, t, d = x.shape
    tm = _row_tile(t, PROJ_ROWS)
    cos, sin = _rope_tables(t, C_HEAD_DIM, use_rope)
    mod_map = (lambda b, i: (b, 0, 0)) if mod.shape[0] == batch else (lambda b, i: (0, 0, 0))
    const = lambda b, i: (0, 0)
    row = lambda width: pl.BlockSpec((1, tm, width), lambda b, i: (b, i, 0))
    return pl.pallas_call(
        _proj_c_kernel,
        grid=(batch, t // tm),
        in_specs=[
            row(d),
            pl.BlockSpec((1, MOD_ROWS, d), mod_map),
            pl.BlockSpec((1, d), const),
            pl.BlockSpec(w.shape, const),
            pl.BlockSpec((tm, LANES), lambda b, i: (i, 0)),
            pl.BlockSpec((tm, LANES), lambda b, i: (i, 0)),
            pl.BlockSpec((1, LANES), const),
            pl.BlockSpec((1, LANES), const),
        ],
        out_specs=[row(C_WIDTH), row(C_KV_WIDTH), row(C_KV_WIDTH)],
        out_shape=[jax.ShapeDtypeStruct((batch, t, C_WIDTH), BF16),
                   jax.ShapeDtypeStruct((batch, t, C_KV_WIDTH), BF16),
                   jax.ShapeDtypeStruct((batch, t, C_KV_WIDTH), BF16)],
        compiler_params=_params("arbitrary", "arbitrary"),
        name="proj_c",
    )(x, mod, g, w, cos, sin, qn, kn)


def _mlstm_kernel(ql_ref, kl_ref, vl_ref, oal_ref, grl_ref, qc_ref, kc_ref, vc_ref, oac_ref, grc_ref, ng_ref,
                  hl_ref, hc_ref,
                  qs, ks, vs, gin, rows, cols, stats, cst, mst, cstate, *, t_ctx, t_lat):
    L = BLOCK
    ncc = t_ctx // L
    ncl = t_lat // L
    nc = ncc + ncl

    qs[0:t_ctx] = qc_ref[0]
    qs[t_ctx:] = ql_ref[0]
    ks[0:t_ctx] = kc_ref[0]
    ks[t_ctx:] = kl_ref[0]
    vs[0:t_ctx] = vc_ref[0]
    vs[t_ctx:] = vl_ref[0]
    gin[:, 0:t_ctx] = grc_ref[0]
    gin[:, t_ctx:] = grl_ref[0]

    ri = lax.broadcasted_iota(jnp.int32, (L, L), 0)
    ci = lax.broadcasted_iota(jnp.int32, (L, L), 1)
    lower = ci <= ri
    upper = ci >= ri
    cum_mat = jnp.concatenate([jnp.where(upper, 1.0, 0.0), jnp.where(lower, 1.0, 0.0)], axis=1).astype(BF16)
    ones_blk = jnp.ones((L, L), BF16)

    def chunk(i):
        return pl.ds(pl.multiple_of(i * L, L), L)

    def gate_stats(i, _):
        g = gin[:, chunk(i)]
        hi, mid, lo = _split3(g)
        cum = _dot(hi, cum_mat) + _dot(mid, cum_mat) + _dot(lo, cum_mat)
        li_f, li_b = g[0:1], g[2:3]
        bcum_f = cum[1:2, 0:L]
        bsuf_b = cum[3:4, L:2 * L]
        bl_f = bcum_f[:, L - 1:L]
        bl_b = bsuf_b[:, 0:1]
        wl_f = bl_f - bcum_f + li_f
        wl_b = bl_b - bsuf_b + li_b
        zero = jnp.zeros((1, L), F32)
        r8 = jnp.concatenate([bcum_f, bsuf_b, li_f - bcum_f, li_b - bsuf_b, wl_f, wl_b, zero, zero], axis=0)
        rows[:, chunk(i)] = r8
        colblk = jnp.concatenate([r8, jnp.zeros((L - 8, L), F32)], axis=0).T
        cols[chunk(i), :] = colblk
        mx_f = jnp.max(wl_f, axis=1, keepdims=True)
        mx_b = jnp.max(wl_b, axis=1, keepdims=True)
        st = jnp.concatenate([jnp.broadcast_to(v, (1, L)) for v in (bl_f, bl_b, mx_f, mx_b)]
                             + [jnp.zeros((4, L), F32)], axis=0)
        stats[i] = st
        kf = ks[chunk(i), :].astype(F32)
        vaug = jnp.concatenate([vs[chunk(i), :], ones_blk], axis=1)
        for d, mx in ((0, mx_f), (1, mx_b)):
            w = jnp.exp(colblk[:, 4 + d:5 + d] - mx)
            cst[d, i] = _dot_tn((kf * w).astype(BF16), vaug)
        return 0

    lax.fori_loop(0, nc, gate_stats, 0, unroll=2)

    cstate[...] = jnp.zeros_like(cstate)

    def scan_step(j, carry):
        new = []
        for d in range(2):
            m = carry[d]
            if d == 0:
                i = j
            else:
                i = jnp.where(j < ncc, ncc - 1 - j, nc - 1 + ncc - j)
            c_old = cstate[d]
            kv = cst[d, i]
            cst[d, i] = c_old
            mst[d, i] = jnp.broadcast_to(m, (8, L))
            st = stats[i]
            bl = st[d:d + 1, 0:1]
            mx = st[2 + d:3 + d, 0:1]
            m_new = jnp.maximum(bl + m, mx)
            cstate[d] = jnp.exp(bl + m - m_new) * c_old + jnp.exp(mx - m_new) * kv
            new.append(m_new)
        return tuple(new)

    m0 = jnp.zeros((1, 1), F32)
    lax.fori_loop(0, nc, scan_step, (m0, m0))

    ng = ng_ref[...]

    def emit(i, oa, out_ref, out_rows):
        q = qs[chunk(i), :]
        k = ks[chunk(i), :]
        vaug = jnp.concatenate([vs[chunk(i), :], ones_blk], axis=1)
        qk = _dot_nt(q, k)
        colblk = cols[chunk(i), :]
        rowblk = rows[:, chunk(i)]
        h = None
        for d in range(2):
            bc = colblk[:, d:d + 1]
            rb = rowblk[2 + d:3 + d, :]
            mask = lower if d == 0 else upper
            mintra = bc + jnp.max(jnp.where(mask, rb, -jnp.inf), axis=1, keepdims=True)
            m = mst[d, i][0:1, 0:1]
            inter_log = bc + m
            mt = jnp.maximum(inter_log, mintra)
            dm = jnp.where(mask, jnp.exp((bc - mt) + rb), 0.0)
            s = (qk * dm).astype(BF16)
            iw = jnp.exp(inter_log - mt)
            nd = iw * _dot(q, cst[d, i].astype(BF16)) + _dot(s, vaug)
            hd = nd[:, 0:L] / jnp.maximum(jnp.abs(nd[:, L:2 * L]), jnp.exp(-mt))
            h = hd if h is None else h + hd
        ms = jnp.mean(h * h, axis=-1, keepdims=True)
        y = (h * lax.rsqrt(ms + EPS) * ng) * jax.nn.sigmoid(oa)
        out_ref[0, out_rows, :] = y.astype(BF16)

    def emit_ctx(i, _):
        emit(i, oac_ref[0, chunk(i), :], hc_ref, chunk(i))
        return 0

    def emit_lat(i, _):
        emit(i + ncc, oal_ref[0, chunk(i), :], hl_ref, chunk(i))
        return 0

    lax.fori_loop(0, ncc, emit_ctx, 0, unroll=2)
    lax.fori_loop(0, ncl, emit_lat, 0, unroll=4)


def _mlstm(qa_l, ka_l, va_l, oa_l, gr_l, qa_c, ka_c, va_c, oa_c, gr_c, norm_g):
    batch, t_lat, _ = qa_l.shape
    t_ctx = qa_c.shape[1]
    t_all = t_ctx + t_lat
    nc = t_all // BLOCK
    head = lambda t: pl.BlockSpec((1, t, A_HEAD_DIM), lambda b, h: (b, 0, h))
    gates = lambda t: pl.BlockSpec((1, GATE_ROWS, t), lambda b, h: (b, h, 0))
    return pl.pallas_call(
        functools.partial(_mlstm_kernel, t_ctx=t_ctx, t_lat=t_lat),
        grid=(batch, A_HEADS),
        in_specs=[head(t_lat), head(t_lat), head(t_lat), head(t_lat), gates(t_lat),
                  head(t_ctx), head(t_ctx), head(t_ctx), head(t_ctx), gates(t_ctx),
                  pl.BlockSpec((1, A_HEAD_DIM), lambda b, h: (0, h))],
        out_specs=[head(t_lat), head(t_ctx)],
        out_shape=[jax.ShapeDtypeStruct((batch, t_lat, A_WIDTH), BF16),
                   jax.ShapeDtypeStruct((batch, t_ctx, A_WIDTH), BF16)],
        scratch_shapes=[
            pltpu.VMEM((t_all, A_HEAD_DIM), BF16),
            pltpu.VMEM((t_all, A_HEAD_DIM), BF16),
            pltpu.VMEM((t_all, A_HEAD_DIM), BF16),
            pltpu.VMEM((GATE_ROWS, t_all), F32),
            pltpu.VMEM((GATE_ROWS, t_all), F32),
            pltpu.VMEM((t_all, LANES), F32),
            pltpu.VMEM((nc, 8, LANES), F32),
            pltpu.VMEM((2, nc, A_HEAD_DIM, 2 * A_HEAD_DIM), F32),
            pltpu.VMEM((2, nc, 8, LANES), F32),
            pltpu.VMEM((2, A_HEAD_DIM, 2 * A_HEAD_DIM), F32),
        ],
        compiler_params=_params("arbitrary", "arbitrary"),
        name="mlstm",
    )(qa_l, ka_l, va_l, oa_l, gr_l, qa_c, ka_c, va_c, oa_c, gr_c, norm_g.reshape(1, A_WIDTH))


def _swa_kernel(sink_ref, q_ref, *refs, t_lat, has_window):
    if has_window:
        kp_ref, kc_ref, kn_ref, vp_ref, vc_ref, vn_ref, kx_ref, vx_ref, o_ref = refs
    else:
        kx_ref, vx_ref, o_ref = refs
    L = BLOCK
    i = pl.program_id(1)
    group = B_HEADS // B_KV_HEADS
    rows = group * L
    q = q_ref[0]
    lane = lax.broadcasted_iota(jnp.int32, (L, LANES), 1)
    lo = lane < B_HEAD_DIM
    zero = jnp.zeros((L, LANES), BF16)
    if has_window:
        t = lax.broadcasted_iota(jnp.int32, (rows, 1), 0) & (L - 1)
        kpos = (i - 1) * L + lax.broadcasted_iota(jnp.int32, (1, 3 * L), 1)
        qpos = i * L + t
        valid = (jnp.abs(kpos - qpos) <= WINDOW) & (kpos >= 0) & (kpos < t_lat)
    for kvh in range(B_KV_HEADS):
        sl = slice(kvh * LANES, (kvh + 1) * LANES)
        qa = q[:, (2 * kvh) * LANES:(2 * kvh + 1) * LANES]
        qb = q[:, (2 * kvh + 1) * LANES:(2 * kvh + 2) * LANES]
        q4 = jnp.concatenate([jnp.where(lo, qa, zero), jnp.where(lo, zero, qa),
                              jnp.where(lo, qb, zero), jnp.where(lo, zero, qb)], axis=0)
        sink = jnp.concatenate([jnp.full((L, 1), sink_ref[kvh * group + g], F32) for g in range(group)], axis=0)
        s_x = _dot_nt(q4, kx_ref[0, :, sl])
        m = jnp.maximum(jnp.max(s_x, axis=1, keepdims=True), sink)
        if has_window:
            kw = jnp.concatenate([kp_ref[0, :, sl], kc_ref[0, :, sl], kn_ref[0, :, sl]], axis=0)
            vw = jnp.concatenate([vp_ref[0, :, sl], vc_ref[0, :, sl], vn_ref[0, :, sl]], axis=0)
            s_w = jnp.where(valid, _dot_nt(q4, kw), -jnp.inf)
            m = jnp.maximum(m, jnp.max(s_w, axis=1, keepdims=True))
            e_w = jnp.exp(s_w - m)
        e_x = jnp.exp(s_x - m)
        denom = jnp.sum(e_x, axis=1, keepdims=True) + jnp.exp(sink - m)
        if has_window:
            denom = denom + jnp.sum(e_w, axis=1, keepdims=True)
        o4 = _dot((e_x / denom).astype(BF16), vx_ref[0, :, sl])
        if has_window:
            o4 = o4 + _dot((e_w / denom).astype(BF16), vw)
        o_ref[0, :, (2 * kvh) * LANES:(2 * kvh + 1) * LANES] = jnp.where(lo, o4[0:L], o4[L:2 * L]).astype(BF16)
        o_ref[0, :, (2 * kvh + 1) * LANES:(2 * kvh + 2) * LANES] = (
            jnp.where(lo, o4[2 * L:3 * L], o4[3 * L:4 * L]).astype(BF16))


def _swa(sink, q, k, v, k_ctx, v_ctx, has_window):
    batch, t, _ = q.shape
    t_ctx = k_ctx.shape[1]
    nb = t // BLOCK
    kvw = B_KV_HEADS * LANES
    blk = lambda fn: pl.BlockSpec((1, BLOCK, kvw), fn)
    prev = lambda b, i: (b, jnp.maximum(i - 1, 0), 0)
    cur = lambda b, i: (b, i, 0)
    nxt = lambda b, i: (b, jnp.minimum(i + 1, nb - 1), 0)
    ctx_spec = pl.BlockSpec((1, t_ctx, kvw), lambda b, i: (b, 0, 0))
    in_specs = [pl.BlockSpec(memory_space=pltpu.SMEM), pl.BlockSpec((1, BLOCK, B_WIDTH), cur)]
    args = [sink, q]
    if has_window:
        in_specs += [blk(prev), blk(cur), blk(nxt), blk(prev), blk(cur), blk(nxt)]
        args += [k, k, k, v, v, v]
    in_specs += [ctx_spec, ctx_spec]
    args += [k_ctx, v_ctx]
    return pl.pallas_call(
        functools.partial(_swa_kernel, t_lat=t, has_window=has_window),
        grid=(batch, nb),
        in_specs=in_specs,
        out_specs=pl.BlockSpec((1, BLOCK, B_WIDTH), cur),
        out_shape=jax.ShapeDtypeStruct((batch, t, B_WIDTH), BF16),
        compiler_params=_params("arbitrary", "arbitrary"),
        name="swa" if has_window else "swa_ctx",
    )(*args)


def _flash_kernel(q_ref, k_ref, v_ref, o_ref, s0_scr, s1_scr, m0_scr, m1_scr, *, chunks, tq):
    group = C_HEADS // C_KV_HEADS
    nq = q_ref.shape[1] // tq

    def q_rows(t):
        return pl.ds(pl.multiple_of(t * tq, tq), tq)

    def load_q(t):
        q = q_ref[0, q_rows(t), :]
        return jnp.concatenate([q[:, g * LANES:(g + 1) * LANES] for g in range(group)], axis=0)

    def stage_a(q4, s_ref, m, start, size):
        s = _dot_nt(q4, k_ref[0, start:start + size, :])
        s_ref[:, start:start + size] = s
        cm = jnp.max(s, axis=1, keepdims=True)
        return cm if m is None else jnp.maximum(m, cm)

    def stage_b(s_ref, m, l, acc, start, size):
        p = jnp.exp2(s_ref[:, start:start + size] - m)
        ps = jnp.sum(p, axis=1, keepdims=True)
        pv = _dot(p.astype(BF16), v_ref[0, start:start + size, :])
        return (ps, pv) if l is None else (l + ps, acc + pv)

    def step(t, cur, nxt):
        q4 = load_q(jnp.minimum(t + 1, nq - 1))
        m_cur = cur[1][...]
        m_next, l, acc = None, None, None
        for start, size in chunks:
            m_next = stage_a(q4, nxt[0], m_next, start, size)
            l, acc = stage_b(cur[0], m_cur, l, acc, start, size)
        nxt[1][...] = m_next
        out = acc / l
        for g in range(group):
            o_ref[0, q_rows(t), g * LANES:(g + 1) * LANES] = out[g * tq:(g + 1) * tq].astype(BF16)

    bufs = ((s0_scr, m0_scr), (s1_scr, m1_scr))
    q4 = load_q(0)
    m = None
    for start, size in chunks:
        m = stage_a(q4, s0_scr, m, start, size)
    m0_scr[...] = m

    def body(u, _):
        step(2 * u, bufs[0], bufs[1])
        step(2 * u + 1, bufs[1], bufs[0])
        return 0

    lax.fori_loop(0, nq // 2, body, 0)


def _key_chunks(total, target):
    chunks = []
    start = 0
    while start < total:
        size = min(target, total - start)
        chunks.append((start, size))
        start += size
    return tuple(chunks)


def _flash(q, k, v, t_ctx):
    batch, t, _ = q.shape
    tk_all = k.shape[1]
    tq = _row_tile(t, FLASH_ROWS)
    assert (t // tq) % 2 == 0
    group = C_HEADS // C_KV_HEADS
    gw = group * C_HEAD_DIM
    chunks = _key_chunks(t_ctx, 1024) + tuple((t_ctx + a, n) for a, n in _key_chunks(tk_all - t_ctx, 1024))
    return pl.pallas_call(
        functools.partial(_flash_kernel, chunks=chunks, tq=tq),
        grid=(batch, C_KV_HEADS),
        in_specs=[pl.BlockSpec((1, t, gw), lambda b, h: (b, 0, h)),
                  pl.BlockSpec((1, tk_all, C_HEAD_DIM), lambda b, h: (b, 0, h)),
                  pl.BlockSpec((1, tk_all, C_HEAD_DIM), lambda b, h: (b, 0, h))],
        out_specs=pl.BlockSpec((1, t, gw), lambda b, h: (b, 0, h)),
        out_shape=jax.ShapeDtypeStruct((batch, t, C_WIDTH), BF16),
        scratch_shapes=[pltpu.VMEM((group * tq, tk_all), F32), pltpu.VMEM((group * tq, tk_all), F32),
                        pltpu.VMEM((group * tq, 1), F32), pltpu.VMEM((group * tq, 1), F32)],
        compiler_params=_params("arbitrary", "arbitrary"),
        name="flash_c",
    )(q, k, v)


def _out_mlp_kernel(*refs, n_mix, ff_chunk):
    x_ref, mod_ref, g_ref = refs[0:3]
    mix_refs = refs[3:3 + n_mix]
    wo_refs = refs[3 + n_mix:3 + 2 * n_mix]
    w1_ref, w2_ref, o_ref = refs[3 + 2 * n_mix:]
    mod = mod_ref[0]
    y = None
    for a_ref, w_ref in zip(mix_refs, wo_refs):
        part = _dot(a_ref[0], w_ref[...])
        y = part if y is None else y + part
    x1 = x_ref[0] + mod[2:3] * y
    h = _modulated_norm(x1, g_ref[...], mod[3:4], mod[4:5]).astype(BF16)
    d_ff = w1_ref.shape[1]
    acc = None
    for c in range(d_ff // ff_chunk):
        a = _dot(h, w1_ref[:, c * ff_chunk:(c + 1) * ff_chunk])
        a = jnp.square(jnp.maximum(a, 0.0)).astype(BF16)
        part = _dot(a, w2_ref[c * ff_chunk:(c + 1) * ff_chunk, :])
        acc = part if acc is None else acc + part
    o_ref[0] = x1 + mod[5:6] * acc


def _out_mlp(x, mod, g, mixes, w_outs, w1, w2):
    batch, t, d = x.shape
    tm = _row_tile(t, MLP_ROWS)
    mod_map = (lambda b, i: (b, 0, 0)) if mod.shape[0] == batch else (lambda b, i: (0, 0, 0))
    const = lambda b, i: (0, 0)
    resident = lambda a: pl.BlockSpec(a.shape, const, pipeline_mode=pl.Buffered(1))
    row = lambda width: pl.BlockSpec((1, tm, width), lambda b, i: (b, i, 0))
    return pl.pallas_call(
        functools.partial(_out_mlp_kernel, n_mix=len(mixes), ff_chunk=1024),
        grid=(batch, t // tm),
        in_specs=([row(d), pl.BlockSpec((1, MOD_ROWS, d), mod_map), pl.BlockSpec((1, d), const)]
                  + [row(a.shape[2]) for a in mixes] + [resident(w) for w in w_outs]
                  + [resident(w1), resident(w2)]),
        out_specs=row(d),
        out_shape=jax.ShapeDtypeStruct((batch, t, d), F32),
        compiler_params=_params("arbitrary", "arbitrary"),
        name="out_mlp",
    )(x, mod, g, *mixes, *w_outs, w1, w2)


def kernel(x, c, ctx, c_ctx, ada_w, ada_b, norm1_g, norm2_g, ab_w_in, ab_gate_b, mlstm_norm_g, swa_q_norm_g,
           swa_k_norm_g, swa_sink, ab_w_out, c_w_in, c_q_norm_g, c_k_norm_g, c_w_out, mlp_w1, mlp_w2):
    depth = ada_w.shape[0]
    batch, _, d = x.shape
    mods = _mods(c, c_ctx, ada_w, ada_b)
    for layer in range(depth):
        last = layer == depth - 1
        mod_l = mods[layer, :batch]
        mod_c = mods[layer, batch:batch + 1]
        g1 = norm1_g[layer].reshape(1, d)
        g2 = norm2_g[layer].reshape(1, d)
        w1 = mlp_w1[layer].astype(BF16)
        w2 = mlp_w2[layer].astype(BF16)
        j = layer // 2
        if layer % 2 == 0:
            w, wg, gb = _ab_weights(ab_w_in[j], ab_gate_b[j])
            qn = jnp.tile(swa_q_norm_g[j], LANES // B_HEAD_DIM).reshape(1, LANES)
            kn = jnp.tile(swa_k_norm_g[j], LANES // B_HEAD_DIM).reshape(1, LANES)
            qa_l, ka_l, va_l, oa_l, qb_l, kb_l, vb_l, gr_l = _proj_ab(x, mod_l, g1, w, wg, gb, qn, kn, True)
            qa_c, ka_c, va_c, oa_c, qb_c, kb_c, vb_c, gr_c = _proj_ab(ctx, mod_c, g1, w, wg, gb, qn, kn, False)
            ha_l, ha_c = _mlstm(qa_l, ka_l, va_l, oa_l, gr_l, qa_c, ka_c, va_c, oa_c, gr_c, mlstm_norm_g[j])
            ob_l = _swa(swa_sink[j], qb_l, kb_l, vb_l, kb_c, vb_c, True)
            w_out = ab_w_out[j].astype(BF16)
            w_outs = [w_out[:A_WIDTH], w_out[A_WIDTH:]]
            x = _out_mlp(x, mod_l, g2, [ha_l, ob_l], w_outs, w1, w2)
            if not last:
                ob_c = _swa(swa_sink[j], qb_c, kb_c, vb_c, kb_c, vb_c, False)
                ctx = _out_mlp(ctx, mod_c, g2, [ha_c, ob_c], w_outs, w1, w2)
        else:
            w = c_w_in[j].astype(BF16)
            qn = c_q_norm_g[j].reshape(1, LANES)
            kn = c_k_norm_g[j].reshape(1, LANES)
            q_l, k_l, v_l = _proj_c(x, mod_l, g1, w, qn, kn, True)
            q_c, k_c, v_c = _proj_c(ctx, mod_c, g1, w, qn, kn, False)
            k_all = jnp.concatenate([k_c, k_l], axis=1)
            v_all = jnp.concatenate([v_c, v_l], axis=1)
            w_out = c_w_out[j].astype(BF16)
            o_l = _flash(q_l, k_all, v_all, k_c.shape[1])
            x = _out_mlp(x, mod_l, g2, [o_l], [w_out], w1, w2)
            if not last:
                o_c = _flash(q_c, k_c, v_c, 0)
                ctx = _out_mlp(ctx, mod_c, g2, [o_c], [w_out], w1, w2)
    return x
```

```python
import functools

import jax
import jax.numpy as jnp
from jax import lax
from jax.experimental import pallas as pl
from jax.experimental.pallas import tpu as pltpu

F32 = jnp.float32
BF16 = jnp.bfloat16

GRID_W = 64
BLOCK = 128
WINDOW = 128
ROPE_BASE = 10000.0
EPS = 1e-6
LOG2_E = 1.4426950408889634
PROJ_ROWS = 512
MLP_ROWS = 512
SWA_BLOCKS = 4
FLASH_KEYS = 512
FLASH_ROWS = 128
N_MOD = 6
MOD_ROWS = 8
LANES = 128

A_HEADS = 4
A_HEAD_DIM = 128
A_WIDTH = A_HEADS * A_HEAD_DIM
A_GATES = 4 * A_HEADS
GATE_ROWS = 8
B_HEADS = 8
B_KV_HEADS = 2
B_HEAD_DIM = 64
B_WIDTH = B_HEADS * B_HEAD_DIM
B_KV_WIDTH = B_KV_HEADS * B_HEAD_DIM
C_HEADS = 8
C_KV_HEADS = 2
C_HEAD_DIM = 128
C_WIDTH = C_HEADS * C_HEAD_DIM
C_KV_WIDTH = C_KV_HEADS * C_HEAD_DIM

VMEM_LIMIT = 56 * 1024 * 1024

NT_DIMS = (((1,), (1,)), ((), ()))
TN_DIMS = (((0,), (0,)), ((), ()))


def _dot(a, b):
    return jnp.dot(a, b, preferred_element_type=F32)


def _dot_nt(a, b):
    return lax.dot_general(a, b, NT_DIMS, preferred_element_type=F32)


def _dot_tn(a, b):
    return lax.dot_general(a, b, TN_DIMS, preferred_element_type=F32)


def _params(*sem):
    return pltpu.CompilerParams(dimension_semantics=sem, vmem_limit_bytes=VMEM_LIMIT)


def _mods_kernel(c_ref, w_ref, b_ref, o_ref):
    cf = c_ref[...]
    s = (cf * jax.nn.sigmoid(cf)).astype(BF16)
    o_ref[0] = _dot(s, w_ref[0].astype(BF16)) + b_ref[0]


def _mods(c, c_ctx, ada_w, ada_b):
    depth, d, _ = ada_w.shape
    batch = c.shape[0]
    rows = -(-(batch + 1) // 8) * 8
    cc = jnp.zeros((rows, d), F32).at[:batch].set(c).at[batch].set(c_ctx)
    out = pl.pallas_call(
        _mods_kernel,
        grid=(depth, N_MOD),
        in_specs=[
            pl.BlockSpec((rows, d), lambda l, j: (0, 0)),
            pl.BlockSpec((1, d, d), lambda l, j: (l, 0, j)),
            pl.BlockSpec((1, 1, d), lambda l, j: (l, 0, j)),
        ],
        out_specs=pl.BlockSpec((1, rows, d), lambda l, j: (l, 0, j)),
        out_shape=jax.ShapeDtypeStruct((depth, rows, N_MOD * d), F32),
        compiler_params=_params("arbitrary", "arbitrary"),
        name="ada_mods",
    )(cc, ada_w, ada_b.reshape(depth, 1, N_MOD * d))
    out = out.reshape(depth, rows, N_MOD, d)
    return jnp.pad(out, ((0, 0), (0, 0), (0, MOD_ROWS - N_MOD), (0, 0)))


def _modulated_norm(x, g, shift, scale):
    ms = jnp.mean(x * x, axis=-1, keepdims=True)
    return (x * lax.rsqrt(ms + EPS) * g) * (1.0 + scale) + shift


def _split3(a):
    hi = a.astype(BF16)
    r = a - hi.astype(F32)
    mid = r.astype(BF16)
    lo = (r - mid.astype(F32)).astype(BF16)
    return hi, mid, lo


def _head_norm_rope(xh, gain, cos, sin, group, out_scale):
    sq = xh * xh
    half = group // 2
    if group == LANES:
        ssq = jnp.sum(sq, axis=-1, keepdims=True)
    else:
        assert 2 * group == LANES
        lane = lax.broadcasted_iota(jnp.int32, xh.shape, 1)
        left = lane < group
        ssq = jnp.where(left, jnp.sum(jnp.where(left, sq, 0.0), axis=-1, keepdims=True),
                        jnp.sum(jnp.where(left, 0.0, sq), axis=-1, keepdims=True))
    xn = xh * lax.rsqrt(ssq * (1.0 / group) + EPS) * gain
    if group == LANES:
        rot = pltpu.roll(xn, half, 1)
    else:
        first = (lane & (group - 1)) < half
        rot = jnp.where(first, pltpu.roll(xn, LANES - half, 1), pltpu.roll(xn, half, 1))
    y = xn * cos + rot * sin
    if out_scale != 1.0:
        y = y * out_scale
    return y


def _rope_tables(n_tokens, head_dim, use_rope):
    reps = LANES // head_dim
    if not use_rope:
        return jnp.ones((n_tokens, LANES), F32), jnp.zeros((n_tokens, LANES), F32)
    rows = n_tokens // GRID_W
    row = jnp.broadcast_to(jnp.arange(rows)[:, None], (rows, GRID_W)).reshape(n_tokens).astype(F32)
    col = jnp.broadcast_to(jnp.arange(GRID_W)[None, :], (rows, GRID_W)).reshape(n_tokens).astype(F32)
    pairs = head_dim // 4
    inv_freq = ROPE_BASE ** (-jnp.arange(pairs, dtype=F32) / pairs)
    ang = jnp.concatenate([row[:, None] * inv_freq, col[:, None] * inv_freq], axis=-1)
    cos, sin = jnp.cos(ang), jnp.sin(ang)
    cos_full = jnp.concatenate([cos, cos], axis=-1)
    sin_signed = jnp.concatenate([-sin, sin], axis=-1)
    return jnp.tile(cos_full, (1, reps)), jnp.tile(sin_signed, (1, reps))


def _row_tile(t, target):
    tm = min(t, target)
    assert t % tm == 0
    return tm


def _proj_ab_kernel(x_ref, mod_ref, g_ref, w_ref, wg_ref, gb_ref, cos_ref, sin_ref, qn_ref, kn_ref,
                    qa_ref, ka_ref, va_ref, oa_ref, qb_ref, kb_ref, vb_ref, gr_ref):
    mod = mod_ref[0]
    h = _modulated_norm(x_ref[0], g_ref[...], mod[0:1], mod[1:2]).astype(BF16)

    def mm(lo, n):
        return _dot(h, w_ref[:, lo:lo + n])

    qa_ref[0] = mm(0, A_WIDTH).astype(BF16)
    ka_ref[0] = (mm(A_WIDTH, A_WIDTH) * A_HEAD_DIM ** -0.5).astype(BF16)
    va_ref[0] = mm(2 * A_WIDTH, A_WIDTH).astype(BF16)
    oa_ref[0] = mm(3 * A_WIDTH, A_WIDTH)
    cos, sin = cos_ref[...], sin_ref[...]
    base = 4 * A_WIDTH
    for j in range(B_WIDTH // LANES):
        y = _head_norm_rope(mm(base + LANES * j, LANES), qn_ref[...], cos, sin, B_HEAD_DIM,
                            B_HEAD_DIM ** -0.5 * LOG2_E)
        qb_ref[0, :, LANES * j:LANES * (j + 1)] = y.astype(BF16)
    base += B_WIDTH
    for j in range(B_KV_HEADS):
        y = _head_norm_rope(mm(base + LANES * j, LANES), kn_ref[...], cos, sin, B_HEAD_DIM, 1.0)
        kb_ref[0, :, LANES * j:LANES * (j + 1)] = y.astype(BF16)
    base += B_KV_HEADS * LANES
    vb_ref[0] = mm(base, B_KV_HEADS * LANES).astype(BF16)
    gt = _dot_nt(wg_ref[...], h) + gb_ref[...]
    typ = lax.broadcasted_iota(jnp.int32, gt.shape, 0) & (GATE_ROWS - 1)
    log_sig = jnp.minimum(gt, 0.0) - jnp.log1p(jnp.exp(-jnp.abs(gt)))
    gr_ref[0] = jnp.where((typ == 1) | (typ == 3), log_sig, gt)


def _proj_ab(x, mod, g, w, wg, gb, qn, kn, use_rope):
    batch, t, d = x.shape
    tm = _row_tile(t, PROJ_ROWS)
    cos, sin = _rope_tables(t, B_HEAD_DIM, use_rope)
    mod_map = (lambda b, i: (b, 0, 0)) if mod.shape[0] == batch else (lambda b, i: (0, 0, 0))
    const = lambda b, i: (0, 0)
    row = lambda width: pl.BlockSpec((1, tm, width), lambda b, i: (b, i, 0))
    kvw = B_KV_HEADS * LANES
    out_shape = [
        jax.ShapeDtypeStruct((batch, t, A_WIDTH), BF16),
        jax.ShapeDtypeStruct((batch, t, A_WIDTH), BF16),
        jax.ShapeDtypeStruct((batch, t, A_WIDTH), BF16),
        jax.ShapeDtypeStruct((batch, t, A_WIDTH), F32),
        jax.ShapeDtypeStruct((batch, t, B_WIDTH), BF16),
        jax.ShapeDtypeStruct((batch, t, kvw), BF16),
        jax.ShapeDtypeStruct((batch, t, kvw), BF16),
        jax.ShapeDtypeStruct((batch, A_HEADS * GATE_ROWS, t), F32),
    ]
    out_specs = [row(A_WIDTH), row(A_WIDTH), row(A_WIDTH), row(A_WIDTH), row(B_WIDTH), row(kvw), row(kvw),
                 pl.BlockSpec((1, A_HEADS * GATE_ROWS, tm), lambda b, i: (b, 0, i))]
    return pl.pallas_call(
        _proj_ab_kernel,
        grid=(batch, t // tm),
        in_specs=[
            row(d),
            pl.BlockSpec((1, MOD_ROWS, d), mod_map),
            pl.BlockSpec((1, d), const),
            pl.BlockSpec(w.shape, const),
            pl.BlockSpec(wg.shape, const),
            pl.BlockSpec(gb.shape, const),
            pl.BlockSpec((tm, LANES), lambda b, i: (i, 0)),
            pl.BlockSpec((tm, LANES), lambda b, i: (i, 0)),
            pl.BlockSpec((1, LANES), const),
            pl.BlockSpec((1, LANES), const),
        ],
        out_specs=out_specs,
        out_shape=out_shape,
        compiler_params=_params("arbitrary", "arbitrary"),
        name="proj_ab",
    )(x, mod, g, w, wg, gb, cos, sin, qn, kn)


def _ab_weights(w_in, gate_b):
    bounds = [A_WIDTH * 4, A_WIDTH * 4 + A_GATES, A_WIDTH * 4 + A_GATES + B_WIDTH,
              A_WIDTH * 4 + A_GATES + B_WIDTH + B_KV_WIDTH]
    wa, wgate, wqb, wkb, wvb = jnp.split(w_in, bounds, axis=1)

    def dup(wk):
        parts = []
        for hh in range(B_KV_HEADS):
            blk = wk[:, hh * B_HEAD_DIM:(hh + 1) * B_HEAD_DIM]
            parts += [blk] * (LANES // B_HEAD_DIM)
        return jnp.concatenate(parts, axis=1)

    w = jnp.concatenate([wa, wqb, dup(wkb), dup(wvb)], axis=1).astype(BF16)
    d = w_in.shape[0]
    wg = wgate.reshape(d, 4, A_HEADS).transpose(2, 1, 0)
    wg = jnp.pad(wg, ((0, 0), (0, GATE_ROWS - 4), (0, 0))).reshape(A_HEADS * GATE_ROWS, d).astype(BF16)
    gb = gate_b.reshape(4, A_HEADS).T
    gb = jnp.pad(gb, ((0, 0), (0, GATE_ROWS - 4))).reshape(A_HEADS * GATE_ROWS, 1).astype(F32)
    return w, wg, gb


def _proj_c_kernel(x_ref, mod_ref, g_ref, w_ref, cos_ref, sin_ref, qn_ref, kn_ref, q_ref, k_ref, v_ref):
    mod = mod_ref[0]
    h = _modulated_norm(x_ref[0], g_ref[...], mod[0:1], mod[1:2]).astype(BF16)

    def mm(lo, n):
        return _dot(h, w_ref[:, lo:lo + n])

    cos, sin = cos_ref[...], sin_ref[...]
    q_scale = C_HEAD_DIM ** -0.5 * LOG2_E
    for j in range(C_HEADS):
        y = _head_norm_rope(mm(LANES * j, LANES), qn_ref[...], cos, sin, C_HEAD_DIM, q_scale)
        q_ref[0, :, LANES * j:LANES * (j + 1)] = y.astype(BF16)
    for j in range(C_KV_HEADS):
        y = _head_norm_rope(mm(C_WIDTH + LANES * j, LANES), kn_ref[...], cos, sin, C_HEAD_DIM, 1.0)
        k_ref[0, :, LANES * j:LANES * (j + 1)] = y.astype(BF16)
    v_ref[0] = mm(C_WIDTH + C_KV_WIDTH, C_KV_WIDTH).astype(BF16)


def _proj_c(x, mod, g, w, qn, kn, use_rope):
    batch, t, d = x.shape
    tm = _row_tile(t, PROJ_ROWS)
    cos, sin = _rope_tables(t, C_HEAD_DIM, use_rope)
    mod_map = (lambda b, i: (b, 0, 0)) if mod.shape[0] == batch else (lambda b, i: (0, 0, 0))
    const = lambda b, i: (0, 0)
    row = lambda width: pl.BlockSpec((1, tm, width), lambda b, i: (b, i, 0))
    return pl.pallas_call(
        _proj_c_kernel,
        grid=(batch, t // tm),
        in_specs=[
            row(d),
            pl.BlockSpec((1, MOD_ROWS, d), mod_map),
            pl.BlockSpec((1, d), const),
            pl.BlockSpec(w.shape, const),
            pl.BlockSpec((tm, LANES), lambda b, i: (i, 0)),
            pl.BlockSpec((tm, LANES), lambda b, i: (i, 0)),
            pl.BlockSpec((1, LANES), const),
            pl.BlockSpec((1, LANES), const),
        ],
        out_specs=[row(C_WIDTH), row(C_KV_WIDTH), row(C_KV_WIDTH)],
        out_shape=[jax.ShapeDtypeStruct((batch, t, C_WIDTH), BF16),
                   jax.ShapeDtypeStruct((batch, t, C_KV_WIDTH), BF16),
                   jax.ShapeDtypeStruct((batch, t, C_KV_WIDTH), BF16)],
        compiler_params=_params("arbitrary", "arbitrary"),
        name="proj_c",
    )(x, mod, g, w, cos, sin, qn, kn)


def _mlstm_kernel(ql_ref, kl_ref, vl_ref, oal_ref, grl_ref, qc_ref, kc_ref, vc_ref, oac_ref, grc_ref, ng_ref,
                  hl_ref, hc_ref,
                  qs, ks, vs, gin, rows, cols, stats, cst, mst, cstate, *, t_ctx, t_lat):
    L = BLOCK
    ncc = t_ctx // L
    ncl = t_lat // L
    nc = ncc + ncl

    qs[0:t_ctx] = qc_ref[0]
    qs[t_ctx:] = ql_ref[0]
    ks[0:t_ctx] = kc_ref[0]
    ks[t_ctx:] = kl_ref[0]
    vs[0:t_ctx] = vc_ref[0]
    vs[t_ctx:] = vl_ref[0]
    gin[:, 0:t_ctx] = grc_ref[0]
    gin[:, t_ctx:] = grl_ref[0]

    ri = lax.broadcasted_iota(jnp.int32, (L, L), 0)
    ci = lax.broadcasted_iota(jnp.int32, (L, L), 1)
    lower = ci <= ri
    upper = ci >= ri
    cum_mat = jnp.concatenate([jnp.where(upper, 1.0, 0.0), jnp.where(lower, 1.0, 0.0)], axis=1).astype(BF16)
    ones_blk = jnp.ones((L, L), BF16)

    def chunk(i):
        return pl.ds(pl.multiple_of(i * L, L), L)

    def gate_stats(i, _):
        g = gin[:, chunk(i)]
        hi, mid, lo = _split3(g)
        cum = _dot(hi, cum_mat) + _dot(mid, cum_mat) + _dot(lo, cum_mat)
        li_f, li_b = g[0:1], g[2:3]
        bcum_f = cum[1:2, 0:L]
        bsuf_b = cum[3:4, L:2 * L]
        bl_f = bcum_f[:, L - 1:L]
        bl_b = bsuf_b[:, 0:1]
        wl_f = bl_f - bcum_f + li_f
        wl_b = bl_b - bsuf_b + li_b
        zero = jnp.zeros((1, L), F32)
        r8 = jnp.concatenate([bcum_f, bsuf_b, li_f - bcum_f, li_b - bsuf_b, wl_f, wl_b, zero, zero], axis=0)
        rows[:, chunk(i)] = r8
        colblk = jnp.concatenate([r8, jnp.zeros((L - 8, L), F32)], axis=0).T
        cols[chunk(i), :] = colblk
        mx_f = jnp.max(wl_f, axis=1, keepdims=True)
        mx_b = jnp.max(wl_b, axis=1, keepdims=True)
        st = jnp.concatenate([jnp.broadcast_to(v, (1, L)) for v in (bl_f, bl_b, mx_f, mx_b)]
                             + [jnp.zeros((4, L), F32)], axis=0)
        stats[i] = st
        kf = ks[chunk(i), :].astype(F32)
        vaug = jnp.concatenate([vs[chunk(i), :], ones_blk], axis=1)
        for d, mx in ((0, mx_f), (1, mx_b)):
            w = jnp.exp(colblk[:, 4 + d:5 + d] - mx)
            cst[d, i] = _dot_tn((kf * w).astype(BF16), vaug)
        return 0

    lax.fori_loop(0, nc, gate_stats, 0, unroll=2)

    cstate[...] = jnp.zeros_like(cstate)

    def scan_step(j, carry):
        new = []
        for d in range(2):
            m = carry[d]
            if d == 0:
                i = j
            else:
                i = jnp.where(j < ncc, ncc - 1 - j, nc - 1 + ncc - j)
            c_old = cstate[d]
            kv = cst[d, i]
            cst[d, i] = c_old
            mst[d, i] = jnp.broadcast_to(m, (8, L))
            st = stats[i]
            bl = st[d:d + 1, 0:1]
            mx = st[2 + d:3 + d, 0:1]
            m_new = jnp.maximum(bl + m, mx)
            cstate[d] = jnp.exp(bl + m - m_new) * c_old + jnp.exp(mx - m_new) * kv
            new.append(m_new)
        return tuple(new)

    m0 = jnp.zeros((1, 1), F32)
    lax.fori_loop(0, nc, scan_step, (m0, m0))

    ng = ng_ref[...]

    def emit(i, oa, out_ref, out_rows):
        q = qs[chunk(i), :]
        k = ks[chunk(i), :]
        vaug = jnp.concatenate([vs[chunk(i), :], ones_blk], axis=1)
        qk = _dot_nt(q, k)
        colblk = cols[chunk(i), :]
        rowblk = rows[:, chunk(i)]
        h = None
        for d in range(2):
            bc = colblk[:, d:d + 1]
            rb = rowblk[2 + d:3 + d, :]
            mask = lower if d == 0 else upper
            mintra = bc + jnp.max(jnp.where(mask, rb, -jnp.inf), axis=1, keepdims=True)
            m = mst[d, i][0:1, 0:1]
            inter_log = bc + m
            mt = jnp.maximum(inter_log, mintra)
            dm = jnp.where(mask, jnp.exp((bc - mt) + rb), 0.0)
            s = (qk * dm).astype(BF16)
            iw = jnp.exp(inter_log - mt)
            nd = iw * _dot(q, cst[d, i].astype(BF16)) + _dot(s, vaug)
            hd = nd[:, 0:L] / jnp.maximum(jnp.abs(nd[:, L:2 * L]), jnp.exp(-mt))
            h = hd if h is None else h + hd
        ms = jnp.mean(h * h, axis=-1, keepdims=True)
        y = (h * lax.rsqrt(ms + EPS) * ng) * jax.nn.sigmoid(oa)
        out_ref[0, out_rows, :] = y.astype(BF16)

    def emit_ctx(i, _):
        emit(i, oac_ref[0, chunk(i), :], hc_ref, chunk(i))
        return 0

    def emit_lat(i, _):
        emit(i + ncc, oal_ref[0, chunk(i), :], hl_ref, chunk(i))
        return 0

    lax.fori_loop(0, ncc, emit_ctx, 0, unroll=2)
    lax.fori_loop(0, ncl, emit_lat, 0, unroll=4)


def _mlstm(qa_l, ka_l, va_l, oa_l, gr_l, qa_c, ka_c, va_c, oa_c, gr_c, norm_g):
    batch, t_lat, _ = qa_l.shape
    t_ctx = qa_c.shape[1]
    t_all = t_ctx + t_lat
    nc = t_all // BLOCK
    head = lambda t: pl.BlockSpec((1, t, A_HEAD_DIM), lambda b, h: (b, 0, h))
    gates = lambda t: pl.BlockSpec((1, GATE_ROWS, t), lambda b, h: (b, h, 0))
    return pl.pallas_call(
        functools.partial(_mlstm_kernel, t_ctx=t_ctx, t_lat=t_lat),
        grid=(batch, A_HEADS),
        in_specs=[head(t_lat), head(t_lat), head(t_lat), head(t_lat), gates(t_lat),
                  head(t_ctx), head(t_ctx), head(t_ctx), head(t_ctx), gates(t_ctx),
                  pl.BlockSpec((1, A_HEAD_DIM), lambda b, h: (0, h))],
        out_specs=[head(t_lat), head(t_ctx)],
        out_shape=[jax.ShapeDtypeStruct((batch, t_lat, A_WIDTH), BF16),
                   jax.ShapeDtypeStruct((batch, t_ctx, A_WIDTH), BF16)],
        scratch_shapes=[
            pltpu.VMEM((t_all, A_HEAD_DIM), BF16),
            pltpu.VMEM((t_all, A_HEAD_DIM), BF16),
            pltpu.VMEM((t_all, A_HEAD_DIM), BF16),
            pltpu.VMEM((GATE_ROWS, t_all), F32),
            pltpu.VMEM((GATE_ROWS, t_all), F32),
            pltpu.VMEM((t_all, LANES), F32),
            pltpu.VMEM((nc, 8, LANES), F32),
            pltpu.VMEM((2, nc, A_HEAD_DIM, 2 * A_HEAD_DIM), F32),
            pltpu.VMEM((2, nc, 8, LANES), F32),
            pltpu.VMEM((2, A_HEAD_DIM, 2 * A_HEAD_DIM), F32),
        ],
        compiler_params=_params("arbitrary", "arbitrary"),
        name="mlstm",
    )(qa_l, ka_l, va_l, oa_l, gr_l, qa_c, ka_c, va_c, oa_c, gr_c, norm_g.reshape(1, A_WIDTH))


def _swa_kernel(sink_ref, q_ref, *refs, t_lat, has_window, blocks):
    if has_window:
        bias_ref, k_ref, v_ref, kx_ref, vx_ref, o_ref, s_scr = refs
    else:
        kx_ref, vx_ref, o_ref, s_scr = refs
    L = BLOCK
    nb = t_lat // L
    group = B_HEADS // B_KV_HEADS
    lane = lax.broadcasted_iota(jnp.int32, (L, LANES), 1)
    lo = lane < B_HEAD_DIM
    zero = jnp.zeros((L, LANES), BF16)
    def window(u):
        qblk = pl.program_id(1) * blocks + u
        start = pl.multiple_of(jnp.clip((qblk - 1) * L, 0, t_lat - 3 * L), L)
        return qblk, pl.ds(start, 3 * L)

    def logits(c, u, kvh):
        sl = slice(kvh * LANES, (kvh + 1) * LANES)
        q = q_ref[0, u * L:(u + 1) * L, :]
        qa = q[:, (2 * kvh) * LANES:(2 * kvh + 1) * LANES]
        qb = q[:, (2 * kvh + 1) * LANES:(2 * kvh + 2) * LANES]
        q4 = jnp.concatenate([jnp.where(lo, qa, zero), jnp.where(lo, zero, qa),
                              jnp.where(lo, qb, zero), jnp.where(lo, zero, qb)], axis=0)
        sink = jnp.concatenate([jnp.full((L, 1), sink_ref[kvh * group + g] * LOG2_E, F32)
                                for g in range(group)], axis=0)
        if has_window:
            qblk, win = window(u)
            bias = bias_ref[jnp.where(qblk == 0, 0, jnp.where(qblk == nb - 1, 2, 1))]
            keys = jnp.concatenate([k_ref[0, win, sl], kx_ref[0, :, sl]], axis=0)
            s = _dot_nt(q4, keys)
            s = (s.reshape(group, L, s.shape[1]) + bias[None]).reshape(s.shape)
        else:
            s = _dot_nt(q4, kx_ref[0, :, sl])
        s_scr[c] = s
        return jnp.maximum(jnp.max(s, axis=1, keepdims=True), sink), sink

    def attend(c, u, kvh, m, sink):
        sl = slice(kvh * LANES, (kvh + 1) * LANES)
        if has_window:
            vals = jnp.concatenate([v_ref[0, window(u)[1], sl], vx_ref[0, :, sl]], axis=0)
        else:
            vals = vx_ref[0, :, sl]
        e = jnp.exp2(s_scr[c] - m)
        denom = jnp.sum(e, axis=1, keepdims=True) + jnp.exp2(sink - m)
        o4 = _dot(e.astype(BF16), vals) / denom
        rows = slice(u * L, (u + 1) * L)
        o_ref[0, rows, (2 * kvh) * LANES:(2 * kvh + 1) * LANES] = (
            jnp.where(lo, o4[0:L], o4[L:2 * L]).astype(BF16))
        o_ref[0, rows, (2 * kvh + 1) * LANES:(2 * kvh + 2) * LANES] = (
            jnp.where(lo, o4[2 * L:3 * L], o4[3 * L:4 * L]).astype(BF16))

    chains = [(u, kvh) for u in range(blocks) for kvh in range(B_KV_HEADS)]
    stats = [logits(0, *chains[0])]
    for c, (u, kvh) in enumerate(chains):
        if c + 1 < len(chains):
            stats.append(logits(c + 1, *chains[c + 1]))
        attend(c, u, kvh, *stats[c])


def _band_bias(t_ctx):
    L = BLOCK
    t = jnp.arange(L)[:, None]
    j = jnp.arange(3 * L)[None, :]
    tables = []
    for rel in (0, -L, -2 * L):
        ok = jnp.abs(j - t + rel) <= WINDOW
        tables.append(jnp.concatenate([jnp.where(ok, 0.0, -jnp.inf), jnp.zeros((L, t_ctx))], axis=1))
    return jnp.stack(tables).astype(F32)


def _swa(sink, q, k, v, k_ctx, v_ctx, has_window):
    batch, t, _ = q.shape
    t_ctx = k_ctx.shape[1]
    nb = t // BLOCK
    blocks = min(SWA_BLOCKS, nb)
    assert nb % blocks == 0 and (nb >= 3 or not has_window)
    kvw = B_KV_HEADS * LANES
    cur = lambda b, i: (b, i, 0)
    whole = lambda b, i: (b, 0, 0)
    ctx_spec = pl.BlockSpec((1, t_ctx, kvw), whole)
    in_specs = [pl.BlockSpec(memory_space=pltpu.SMEM), pl.BlockSpec((1, blocks * BLOCK, B_WIDTH), cur)]
    args = [sink, q]
    if has_window:
        bias = _band_bias(t_ctx)
        in_specs += [pl.BlockSpec(bias.shape, lambda b, i: (0, 0, 0)),
                     pl.BlockSpec((1, t, kvw), whole), pl.BlockSpec((1, t, kvw), whole)]
        args += [bias, k, v]
    in_specs += [ctx_spec, ctx_spec]
    args += [k_ctx, v_ctx]
    return pl.pallas_call(
        functools.partial(_swa_kernel, t_lat=t, has_window=has_window, blocks=blocks),
        grid=(batch, nb // blocks),
        in_specs=in_specs,
        out_specs=pl.BlockSpec((1, blocks * BLOCK, B_WIDTH), cur),
        scratch_shapes=[pltpu.VMEM((blocks * B_KV_HEADS, (B_HEADS // B_KV_HEADS) * BLOCK,
                                    (3 * BLOCK if has_window else 0) + t_ctx), F32)],
        out_shape=jax.ShapeDtypeStruct((batch, t, B_WIDTH), BF16),
        compiler_params=_params("arbitrary", "arbitrary"),
        name="swa" if has_window else "swa_ctx",
    )(*args)


def _flash_kernel(q_ref, k_ref, v_ref, o_ref, s0_scr, s1_scr, m0_scr, m1_scr, *, chunks, tq):
    group = C_HEADS // C_KV_HEADS
    nq = q_ref.shape[1] // tq

    def q_rows(t):
        return pl.ds(pl.multiple_of(t * tq, tq), tq)

    def load_q(t):
        q = q_ref[0, q_rows(t), :]
        return jnp.concatenate([q[:, g * LANES:(g + 1) * LANES] for g in range(group)], axis=0)

    def stage_a(q4, s_ref, m, start, size):
        s = _dot_nt(q4, k_ref[0, start:start + size, :])
        s_ref[:, start:start + size] = s
        cm = jnp.max(s, axis=1, keepdims=True)
        return cm if m is None else jnp.maximum(m, cm)

    def stage_b(s_ref, m, l, acc, start, size):
        p = jnp.exp2(s_ref[:, start:start + size] - m)
        ps = jnp.sum(p, axis=1, keepdims=True)
        pv = _dot(p.astype(BF16), v_ref[0, start:start + size, :])
        return (ps, pv) if l is None else (l + ps, acc + pv)

    def step(t, cur, nxt):
        q4 = load_q(jnp.minimum(t + 1, nq - 1))
        m_cur = cur[1][...]
        m_next, l, acc = None, None, None
        for start, size in chunks:
            m_next = stage_a(q4, nxt[0], m_next, start, size)
            l, acc = stage_b(cur[0], m_cur, l, acc, start, size)
        nxt[1][...] = m_next
        out = acc / l
        for g in range(group):
            o_ref[0, q_rows(t), g * LANES:(g + 1) * LANES] = out[g * tq:(g + 1) * tq].astype(BF16)

    bufs = ((s0_scr, m0_scr), (s1_scr, m1_scr))
    q4 = load_q(0)
    m = None
    for start, size in chunks:
        m = stage_a(q4, s0_scr, m, start, size)
    m0_scr[...] = m

    def body(u, _):
        step(2 * u, bufs[0], bufs[1])
        step(2 * u + 1, bufs[1], bufs[0])
        return 0

    lax.fori_loop(0, nq // 2, body, 0)


def _key_chunks(total, target):
    chunks = []
    start = 0
    while start < total:
        size = min(target, total - start)
        chunks.append((start, size))
        start += size
    return tuple(chunks)


def _flash(q, k, v, t_ctx):
    batch, t, _ = q.shape
    tk_all = k.shape[1]
    tq = _row_tile(t, FLASH_ROWS)
    assert (t // tq) % 2 == 0
    group = C_HEADS // C_KV_HEADS
    gw = group * C_HEAD_DIM
    chunks = (_key_chunks(t_ctx, FLASH_KEYS)
              + tuple((t_ctx + a, n) for a, n in _key_chunks(tk_all - t_ctx, FLASH_KEYS)))
    return pl.pallas_call(
        functools.partial(_flash_kernel, chunks=chunks, tq=tq),
        grid=(batch, C_KV_HEADS),
        in_specs=[pl.BlockSpec((1, t, gw), lambda b, h: (b, 0, h)),
                  pl.BlockSpec((1, tk_all, C_HEAD_DIM), lambda b, h: (b, 0, h)),
                  pl.BlockSpec((1, tk_all, C_HEAD_DIM), lambda b, h: (b, 0, h))],
        out_specs=pl.BlockSpec((1, t, gw), lambda b, h: (b, 0, h)),
        out_shape=jax.ShapeDtypeStruct((batch, t, C_WIDTH), BF16),
        scratch_shapes=[pltpu.VMEM((group * tq, tk_all), F32), pltpu.VMEM((group * tq, tk_all), F32),
                        pltpu.VMEM((group * tq, 1), F32), pltpu.VMEM((group * tq, 1), F32)],
        compiler_params=_params("arbitrary", "arbitrary"),
        name="flash_c",
    )(q, k, v)


def _out_mlp_kernel(*refs, n_mix, ff_chunk):
    x_ref, mod_ref, g_ref = refs[0:3]
    mix_refs = refs[3:3 + n_mix]
    wo_refs = refs[3 + n_mix:3 + 2 * n_mix]
    w1_ref, w2_ref, o_ref = refs[3 + 2 * n_mix:]
    mod = mod_ref[0]
    y = None
    for a_ref, w_ref in zip(mix_refs, wo_refs):
        part = _dot(a_ref[0], w_ref[...])
        y = part if y is None else y + part
    x1 = x_ref[0] + mod[2:3] * y
    h = _modulated_norm(x1, g_ref[...], mod[3:4], mod[4:5]).astype(BF16)
    d_ff = w1_ref.shape[1]
    acc = None
    for c in range(d_ff // ff_chunk):
        a = _dot(h, w1_ref[:, c * ff_chunk:(c + 1) * ff_chunk])
        a = jnp.square(jnp.maximum(a, 0.0)).astype(BF16)
        part = _dot(a, w2_ref[c * ff_chunk:(c + 1) * ff_chunk, :])
        acc = part if acc is None else acc + part
    o_ref[0] = x1 + mod[5:6] * acc


def _out_mlp(x, mod, g, mixes, w_outs, w1, w2):
    batch, t, d = x.shape
    tm = _row_tile(t, MLP_ROWS)
    mod_map = (lambda b, i: (b, 0, 0)) if mod.shape[0] == batch else (lambda b, i: (0, 0, 0))
    const = lambda b, i: (0, 0)
    resident = lambda a: pl.BlockSpec(a.shape, const, pipeline_mode=pl.Buffered(1))
    row = lambda width: pl.BlockSpec((1, tm, width), lambda b, i: (b, i, 0))
    return pl.pallas_call(
        functools.partial(_out_mlp_kernel, n_mix=len(mixes), ff_chunk=1024),
        grid=(batch, t // tm),
        in_specs=([row(d), pl.BlockSpec((1, MOD_ROWS, d), mod_map), pl.BlockSpec((1, d), const)]
                  + [row(a.shape[2]) for a in mixes] + [resident(w) for w in w_outs]
                  + [resident(w1), resident(w2)]),
        out_specs=row(d),
        out_shape=jax.ShapeDtypeStruct((batch, t, d), F32),
        compiler_params=_params("arbitrary", "arbitrary"),
        name="out_mlp",
    )(x, mod, g, *mixes, *w_outs, w1, w2)


def kernel(x, c, ctx, c_ctx, ada_w, ada_b, norm1_g, norm2_g, ab_w_in, ab_gate_b, mlstm_norm_g, swa_q_norm_g,
           swa_k_norm_g, swa_sink, ab_w_out, c_w_in, c_q_norm_g, c_k_norm_g, c_w_out, mlp_w1, mlp_w2):
    depth = ada_w.shape[0]
    batch, _, d = x.shape
    mods = _mods(c, c_ctx, ada_w, ada_b)
    for layer in range(depth):
        last = layer == depth - 1
        mod_l = mods[layer, :batch]
        mod_c = mods[layer, batch:batch + 1]
        g1 = norm1_g[layer].reshape(1, d)
        g2 = norm2_g[layer].reshape(1, d)
        w1 = mlp_w1[layer].astype(BF16)
        w2 = mlp_w2[layer].astype(BF16)
        j = layer // 2
        if layer % 2 == 0:
            w, wg, gb = _ab_weights(ab_w_in[j], ab_gate_b[j])
            qn = jnp.tile(swa_q_norm_g[j], LANES // B_HEAD_DIM).reshape(1, LANES)
            kn = jnp.tile(swa_k_norm_g[j], LANES // B_HEAD_DIM).reshape(1, LANES)
            qa_l, ka_l, va_l, oa_l, qb_l, kb_l, vb_l, gr_l = _proj_ab(x, mod_l, g1, w, wg, gb, qn, kn, True)
            qa_c, ka_c, va_c, oa_c, qb_c, kb_c, vb_c, gr_c = _proj_ab(ctx, mod_c, g1, w, wg, gb, qn, kn, False)
            ha_l, ha_c = _mlstm(qa_l, ka_l, va_l, oa_l, gr_l, qa_c, ka_c, va_c, oa_c, gr_c, mlstm_norm_g[j])
            ob_l = _swa(swa_sink[j], qb_l, kb_l, vb_l, kb_c, vb_c, True)
            w_out = ab_w_out[j].astype(BF16)
            w_outs = [w_out[:A_WIDTH], w_out[A_WIDTH:]]
            x = _out_mlp(x, mod_l, g2, [ha_l, ob_l], w_outs, w1, w2)
            if not last:
                ob_c = _swa(swa_sink[j], qb_c, kb_c, vb_c, kb_c, vb_c, False)
                ctx = _out_mlp(ctx, mod_c, g2, [ha_c, ob_c], w_outs, w1, w2)
        else:
            w = c_w_in[j].astype(BF16)
            qn = c_q_norm_g[j].reshape(1, LANES)
            kn = c_k_norm_g[j].reshape(1, LANES)
            q_l, k_l, v_l = _proj_c(x, mod_l, g1, w, qn, kn, True)
            q_c, k_c, v_c = _proj_c(ctx, mod_c, g1, w, qn, kn, False)
            k_all = jnp.concatenate([k_c, k_l], axis=1)
            v_all = jnp.concatenate([v_c, v_l], axis=1)
            w_out = c_w_out[j].astype(BF16)
            o_l = _flash(q_l, k_all, v_all, k_c.shape[1])
            x = _out_mlp(x, mod_l, g2, [o_l], [w_out], w1, w2)
            if not last:
                o_c = _flash(q_c, k_c, v_c, 0)
                ctx = _out_mlp(ctx, mod_c, g2, [o_c], [w_out], w1, w2)
    return x
```

```python
import functools

import jax
import jax.numpy as jnp
from jax import lax
from jax.experimental import pallas as pl
from jax.experimental.pallas import tpu as pltpu

F32 = jnp.float32
BF16 = jnp.bfloat16

GRID_W = 64
BLOCK = 128
WINDOW = 128
ROPE_BASE = 10000.0
EPS = 1e-6
LOG2_E = 1.4426950408889634
PROJ_ROWS = 512
MLP_ROWS = 512
SWA_BLOCKS = 4
FLASH_KEYS = 512
FLASH_ROWS = 128
N_MOD = 6
MOD_ROWS = 8
LANES = 128

A_HEADS = 4
A_HEAD_DIM = 128
A_WIDTH = A_HEADS * A_HEAD_DIM
A_GATES = 4 * A_HEADS
GATE_ROWS = 8
B_HEADS = 8
B_KV_HEADS = 2
B_HEAD_DIM = 64
B_WIDTH = B_HEADS * B_HEAD_DIM
B_KV_WIDTH = B_KV_HEADS * B_HEAD_DIM
C_HEADS = 8
C_KV_HEADS = 2
C_HEAD_DIM = 128
C_WIDTH = C_HEADS * C_HEAD_DIM
C_KV_WIDTH = C_KV_HEADS * C_HEAD_DIM

VMEM_LIMIT = 56 * 1024 * 1024

NT_DIMS = (((1,), (1,)), ((), ()))
TN_DIMS = (((0,), (0,)), ((), ()))


def _dot(a, b):
    return jnp.dot(a, b, preferred_element_type=F32)


def _dot_nt(a, b):
    return lax.dot_general(a, b, NT_DIMS, preferred_element_type=F32)


def _dot_tn(a, b):
    return lax.dot_general(a, b, TN_DIMS, preferred_element_type=F32)


def _params(*sem):
    return pltpu.CompilerParams(dimension_semantics=sem, vmem_limit_bytes=VMEM_LIMIT)


def _mods_kernel(c_ref, w_ref, b_ref, o_ref):
    cf = c_ref[...]
    s = (cf * jax.nn.sigmoid(cf)).astype(BF16)
    o_ref[0] = _dot(s, w_ref[0].astype(BF16)) + b_ref[0]


def _mods(c, c_ctx, ada_w, ada_b):
    depth, d, _ = ada_w.shape
    batch = c.shape[0]
    rows = -(-(batch + 1) // 8) * 8
    cc = jnp.zeros((rows, d), F32).at[:batch].set(c).at[batch].set(c_ctx)
    out = pl.pallas_call(
        _mods_kernel,
        grid=(depth, N_MOD),
        in_specs=[
            pl.BlockSpec((rows, d), lambda l, j: (0, 0)),
            pl.BlockSpec((1, d, d), lambda l, j: (l, 0, j)),
            pl.BlockSpec((1, 1, d), lambda l, j: (l, 0, j)),
        ],
        out_specs=pl.BlockSpec((1, rows, d), lambda l, j: (l, 0, j)),
        out_shape=jax.ShapeDtypeStruct((depth, rows, N_MOD * d), F32),
        compiler_params=_params("arbitrary", "arbitrary"),
        name="ada_mods",
    )(cc, ada_w, ada_b.reshape(depth, 1, N_MOD * d))
    out = out.reshape(depth, rows, N_MOD, d)
    return jnp.pad(out, ((0, 0), (0, 0), (0, MOD_ROWS - N_MOD), (0, 0)))


def _modulated_norm(x, g, shift, scale):
    ms = jnp.mean(x * x, axis=-1, keepdims=True)
    return (x * lax.rsqrt(ms + EPS) * g) * (1.0 + scale) + shift


def _split3(a):
    hi = a.astype(BF16)
    r = a - hi.astype(F32)
    mid = r.astype(BF16)
    lo = (r - mid.astype(F32)).astype(BF16)
    return hi, mid, lo


def _head_norm_rope(xh, gain, cos, sin, group, out_scale):
    sq = xh * xh
    half = group // 2
    if group == LANES:
        ssq = jnp.sum(sq, axis=-1, keepdims=True)
    else:
        assert 2 * group == LANES
        lane = lax.broadcasted_iota(jnp.int32, xh.shape, 1)
        left = lane < group
        ssq = jnp.where(left, jnp.sum(jnp.where(left, sq, 0.0), axis=-1, keepdims=True),
                        jnp.sum(jnp.where(left, 0.0, sq), axis=-1, keepdims=True))
    xn = xh * lax.rsqrt(ssq * (1.0 / group) + EPS) * gain
    if group == LANES:
        rot = pltpu.roll(xn, half, 1)
    else:
        first = (lane & (group - 1)) < half
        rot = jnp.where(first, pltpu.roll(xn, LANES - half, 1), pltpu.roll(xn, half, 1))
    y = xn * cos + rot * sin
    if out_scale != 1.0:
        y = y * out_scale
    return y


def _rope_tables(n_tokens, head_dim, use_rope):
    reps = LANES // head_dim
    if not use_rope:
        return jnp.ones((n_tokens, LANES), F32), jnp.zeros((n_tokens, LANES), F32)
    rows = n_tokens // GRID_W
    row = jnp.broadcast_to(jnp.arange(rows)[:, None], (rows, GRID_W)).reshape(n_tokens).astype(F32)
    col = jnp.broadcast_to(jnp.arange(GRID_W)[None, :], (rows, GRID_W)).reshape(n_tokens).astype(F32)
    pairs = head_dim // 4
    inv_freq = ROPE_BASE ** (-jnp.arange(pairs, dtype=F32) / pairs)
    ang = jnp.concatenate([row[:, None] * inv_freq, col[:, None] * inv_freq], axis=-1)
    cos, sin = jnp.cos(ang), jnp.sin(ang)
    cos_full = jnp.concatenate([cos, cos], axis=-1)
    sin_signed = jnp.concatenate([-sin, sin], axis=-1)
    return jnp.tile(cos_full, (1, reps)), jnp.tile(sin_signed, (1, reps))


def _row_tile(t, target):
    tm = min(t, target)
    assert t % tm == 0
    return tm


def _proj_ab_kernel(x_ref, mod_ref, g_ref, w_ref, wt_ref, gb_ref, cos_ref, sin_ref, qn_ref, kn_ref,
                    qa_ref, kat_ref, va_ref, oa_ref, qb_ref, kb_ref, vb_ref, gr_ref):
    mod = mod_ref[0]
    h = _modulated_norm(x_ref[0], g_ref[...], mod[0:1], mod[1:2]).astype(BF16)

    def mm(lo, n):
        return _dot(h, w_ref[:, lo:lo + n])

    qa_ref[0] = mm(0, A_WIDTH).astype(BF16)
    va_ref[0] = mm(A_WIDTH, A_WIDTH).astype(BF16)
    oa_ref[0] = mm(2 * A_WIDTH, A_WIDTH)
    cos, sin = cos_ref[...], sin_ref[...]
    base = 3 * A_WIDTH
    for j in range(B_WIDTH // LANES):
        y = _head_norm_rope(mm(base + LANES * j, LANES), qn_ref[...], cos, sin, B_HEAD_DIM,
                            B_HEAD_DIM ** -0.5 * LOG2_E)
        qb_ref[0, :, LANES * j:LANES * (j + 1)] = y.astype(BF16)
    base += B_WIDTH
    for j in range(B_KV_HEADS):
        y = _head_norm_rope(mm(base + LANES * j, LANES), kn_ref[...], cos, sin, B_HEAD_DIM, 1.0)
        kb_ref[0, :, LANES * j:LANES * (j + 1)] = y.astype(BF16)
    base += B_KV_HEADS * LANES
    vb_ref[0] = mm(base, B_KV_HEADS * LANES).astype(BF16)
    tr = _dot_nt(wt_ref[...], h)
    kat_ref[0] = (tr[0:A_WIDTH] * A_HEAD_DIM ** -0.5).astype(BF16)
    gt = tr[A_WIDTH:] + gb_ref[...]
    typ = lax.broadcasted_iota(jnp.int32, gt.shape, 0) & (GATE_ROWS - 1)
    log_sig = jnp.minimum(gt, 0.0) - jnp.log1p(jnp.exp(-jnp.abs(gt)))
    gr_ref[0] = jnp.where((typ == 1) | (typ == 3), log_sig, gt)


def _proj_ab(x, mod, g, w, wg, gb, qn, kn, use_rope):
    batch, t, d = x.shape
    tm = _row_tile(t, PROJ_ROWS)
    cos, sin = _rope_tables(t, B_HEAD_DIM, use_rope)
    mod_map = (lambda b, i: (b, 0, 0)) if mod.shape[0] == batch else (lambda b, i: (0, 0, 0))
    const = lambda b, i: (0, 0)
    row = lambda width: pl.BlockSpec((1, tm, width), lambda b, i: (b, i, 0))
    kvw = B_KV_HEADS * LANES
    out_shape = [
        jax.ShapeDtypeStruct((batch, t, A_WIDTH), BF16),
        jax.ShapeDtypeStruct((batch, A_WIDTH, t), BF16),
        jax.ShapeDtypeStruct((batch, t, A_WIDTH), BF16),
        jax.ShapeDtypeStruct((batch, t, A_WIDTH), F32),
        jax.ShapeDtypeStruct((batch, t, B_WIDTH), BF16),
        jax.ShapeDtypeStruct((batch, t, kvw), BF16),
        jax.ShapeDtypeStruct((batch, t, kvw), BF16),
        jax.ShapeDtypeStruct((batch, A_HEADS * GATE_ROWS, t), F32),
    ]
    col = lambda height: pl.BlockSpec((1, height, tm), lambda b, i: (b, 0, i))
    out_specs = [row(A_WIDTH), col(A_WIDTH), row(A_WIDTH), row(A_WIDTH), row(B_WIDTH), row(kvw), row(kvw),
                 col(A_HEADS * GATE_ROWS)]
    return pl.pallas_call(
        _proj_ab_kernel,
        grid=(batch, t // tm),
        in_specs=[
            row(d),
            pl.BlockSpec((1, MOD_ROWS, d), mod_map),
            pl.BlockSpec((1, d), const),
            pl.BlockSpec(w.shape, const),
            pl.BlockSpec(wg.shape, const),
            pl.BlockSpec(gb.shape, const),
            pl.BlockSpec((tm, LANES), lambda b, i: (i, 0)),
            pl.BlockSpec((tm, LANES), lambda b, i: (i, 0)),
            pl.BlockSpec((1, LANES), const),
            pl.BlockSpec((1, LANES), const),
        ],
        out_specs=out_specs,
        out_shape=out_shape,
        compiler_params=_params("arbitrary", "arbitrary"),
        name="proj_ab",
    )(x, mod, g, w, wg, gb, cos, sin, qn, kn)


def _ab_weights(w_in, gate_b):
    bounds = [A_WIDTH * 4, A_WIDTH * 4 + A_GATES, A_WIDTH * 4 + A_GATES + B_WIDTH,
              A_WIDTH * 4 + A_GATES + B_WIDTH + B_KV_WIDTH]
    wa, wgate, wqb, wkb, wvb = jnp.split(w_in, bounds, axis=1)
    wqa, wka, wva_oa = wa[:, :A_WIDTH], wa[:, A_WIDTH:2 * A_WIDTH], wa[:, 2 * A_WIDTH:]

    def dup(wk):
        parts = []
        for hh in range(B_KV_HEADS):
            blk = wk[:, hh * B_HEAD_DIM:(hh + 1) * B_HEAD_DIM]
            parts += [blk] * (LANES // B_HEAD_DIM)
        return jnp.concatenate(parts, axis=1)

    w = jnp.concatenate([wqa, wva_oa, wqb, dup(wkb), dup(wvb)], axis=1).astype(BF16)
    d = w_in.shape[0]
    wg = wgate.reshape(d, 4, A_HEADS).transpose(2, 1, 0)
    wg = jnp.pad(wg, ((0, 0), (0, GATE_ROWS - 4), (0, 0))).reshape(A_HEADS * GATE_ROWS, d)
    wt = jnp.concatenate([wka.T, wg], axis=0).astype(BF16)
    gb = gate_b.reshape(4, A_HEADS).T
    gb = jnp.pad(gb, ((0, 0), (0, GATE_ROWS - 4))).reshape(A_HEADS * GATE_ROWS, 1).astype(F32)
    return w, wt, gb


def _proj_c_kernel(x_ref, mod_ref, g_ref, w_ref, cos_ref, sin_ref, qn_ref, kn_ref, q_ref, k_ref, v_ref):
    mod = mod_ref[0]
    h = _modulated_norm(x_ref[0], g_ref[...], mod[0:1], mod[1:2]).astype(BF16)

    def mm(lo, n):
        return _dot(h, w_ref[:, lo:lo + n])

    cos, sin = cos_ref[...], sin_ref[...]
    q_scale = C_HEAD_DIM ** -0.5 * LOG2_E
    for j in range(C_HEADS):
        y = _head_norm_rope(mm(LANES * j, LANES), qn_ref[...], cos, sin, C_HEAD_DIM, q_scale)
        q_ref[0, :, LANES * j:LANES * (j + 1)] = y.astype(BF16)
    for j in range(C_KV_HEADS):
        y = _head_norm_rope(mm(C_WIDTH + LANES * j, LANES), kn_ref[...], cos, sin, C_HEAD_DIM, 1.0)
        k_ref[0, :, LANES * j:LANES * (j + 1)] = y.astype(BF16)
    v_ref[0] = mm(C_WIDTH + C_KV_WIDTH, C_KV_WIDTH).astype(BF16)


def _proj_c(x, mod, g, w, qn, kn, use_rope):
    batch, t, d = x.shape
    tm = _row_tile(t, PROJ_ROWS)
    cos, sin = _rope_tables(t, C_HEAD_DIM, use_rope)
    mod_map = (lambda b, i: (b, 0, 0)) if mod.shape[0] == batch else (lambda b, i: (0, 0, 0))
    const = lambda b, i: (0, 0)
    row = lambda width: pl.BlockSpec((1, tm, width), lambda b, i: (b, i, 0))
    return pl.pallas_call(
        _proj_c_kernel,
        grid=(batch, t // tm),
        in_specs=[
            row(d),
            pl.BlockSpec((1, MOD_ROWS, d), mod_map),
            pl.BlockSpec((1, d), const),
            pl.BlockSpec(w.shape, const),
            pl.BlockSpec((tm, LANES), lambda b, i: (i, 0)),
            pl.BlockSpec((tm, LANES), lambda b, i: (i, 0)),
            pl.BlockSpec((1, LANES), const),
            pl.BlockSpec((1, LANES), const),
        ],
        out_specs=[row(C_WIDTH), row(C_KV_WIDTH), row(C_KV_WIDTH)],
        out_shape=[jax.ShapeDtypeStruct((batch, t, C_WIDTH), BF16),
                   jax.ShapeDtypeStruct((batch, t, C_KV_WIDTH), BF16),
                   jax.ShapeDtypeStruct((batch, t, C_KV_WIDTH), BF16)],
        compiler_params=_params("arbitrary", "arbitrary"),
        name="proj_c",
    )(x, mod, g, w, cos, sin, qn, kn)


def _mlstm_kernel(ql_ref, ktl_ref, vl_ref, oal_ref, gl_ref, qc_ref, ktc_ref, vc_ref, oac_ref, gc_ref, ng_ref,
                  hl_ref, hc_ref,
                  qs, kts, vs, gs, rowq, cols, stats, cst, mst, cstate, *, t_ctx, t_lat):
    L = BLOCK
    ncc = t_ctx // L
    ncl = t_lat // L
    nc = ncc + ncl

    qs[0:t_lat] = ql_ref[0]
    qs[t_lat:] = qc_ref[0]
    kts[:, 0:t_lat] = ktl_ref[0]
    kts[:, t_lat:] = ktc_ref[0]
    vs[0:t_lat] = vl_ref[0]
    vs[t_lat:] = vc_ref[0]
    for r in range(4):
        gs[r, 0:ncl] = gl_ref[0, r]
        gs[r, ncl:nc] = gc_ref[0, r]

    ri = lax.broadcasted_iota(jnp.int32, (L, L), 0)
    ci = lax.broadcasted_iota(jnp.int32, (L, L), 1)
    lower = ci <= ri
    upper = ci >= ri
    ones_blk = jnp.ones((L, L), BF16)

    def chunk(i):
        return pl.ds(pl.multiple_of(i * L, L), L)

    def cum(x, mat):
        hi, mid, lo = _split3(x)
        return _dot(hi, mat) + _dot(mid, mat) + _dot(lo, mat)

    li_f, lf_f, li_b, lf_b = (gs[r, 0:nc] for r in range(4))
    bcum_f = cum(lf_f, jnp.where(upper, 1.0, 0.0).astype(BF16))
    bsuf_b = cum(lf_b, jnp.where(lower, 1.0, 0.0).astype(BF16))
    bl_f = bcum_f[:, L - 1:L]
    bl_b = bsuf_b[:, 0:1]
    wl_f = bl_f - bcum_f + li_f
    wl_b = bl_b - bsuf_b + li_b
    mx_f = jnp.max(wl_f, axis=1, keepdims=True)
    mx_b = jnp.max(wl_b, axis=1, keepdims=True)
    for n, val in enumerate((bcum_f, bsuf_b, li_f - bcum_f, li_b - bsuf_b,
                             jnp.exp(wl_f - mx_f), jnp.exp(wl_b - mx_b))):
        rowq[n, 0:nc] = val
    for n, val in enumerate((bl_f, bl_b, mx_f, mx_b)):
        stats[n, 0:nc] = jnp.broadcast_to(val, (nc, L))

    def prep(i, _):
        r8 = jnp.concatenate([rowq[0, pl.ds(i, 1), :], rowq[1, pl.ds(i, 1), :], jnp.zeros((6, L), F32)], axis=0)
        cols[chunk(i), :] = jnp.concatenate([r8, jnp.zeros((L - 8, L), F32)], axis=0).T
        kt = kts[:, chunk(i)].astype(F32)
        vaug = jnp.concatenate([vs[chunk(i), :], ones_blk], axis=1)
        for d in range(2):
            cst[d, i] = _dot((kt * rowq[4 + d, pl.ds(i, 1), :]).astype(BF16), vaug)
        return 0

    lax.fori_loop(0, nc, prep, 0, unroll=2)

    cstate[...] = jnp.zeros_like(cstate)

    def scan_step(j, carry):
        new = []
        for d in range(2):
            m = carry[d]
            if d == 0:
                i = jnp.where(j < ncc, ncl + j, j - ncc)
            else:
                i = nc - 1 - j
            c_old = cstate[d]
            kv = cst[d, i]
            cst[d, i] = c_old
            mst[d, i] = jnp.broadcast_to(m, (8, L))
            bl = stats[d, pl.ds(i, 1), 0:1]
            mx = stats[2 + d, pl.ds(i, 1), 0:1]
            m_new = jnp.maximum(bl + m, mx)
            cstate[d] = jnp.exp(bl + m - m_new) * c_old + jnp.exp(mx - m_new) * kv
            new.append(m_new)
        return tuple(new)

    m0 = jnp.zeros((1, 1), F32)
    lax.fori_loop(0, nc, scan_step, (m0, m0))

    ng = ng_ref[...]

    def emit(i, oa, out_ref, out_rows):
        q = qs[chunk(i), :]
        vaug = jnp.concatenate([vs[chunk(i), :], ones_blk], axis=1)
        qk = _dot(q, kts[:, chunk(i)])
        colblk = cols[chunk(i), :]
        h = None
        for d in range(2):
            bc = colblk[:, d:d + 1]
            rb = rowq[2 + d, pl.ds(i, 1), :]
            mask = lower if d == 0 else upper
            mintra = bc + jnp.max(jnp.where(mask, rb, -jnp.inf), axis=1, keepdims=True)
            m = mst[d, i][0:1, 0:1]
            inter_log = bc + m
            mt = jnp.maximum(inter_log, mintra)
            dm = jnp.where(mask, jnp.exp((bc - mt) + rb), 0.0)
            s = (qk * dm).astype(BF16)
            iw = jnp.exp(inter_log - mt)
            nd = iw * _dot(q, cst[d, i].astype(BF16)) + _dot(s, vaug)
            hd = nd[:, 0:L] / jnp.maximum(jnp.abs(nd[:, L:2 * L]), jnp.exp(-mt))
            h = hd if h is None else h + hd
        ms = jnp.mean(h * h, axis=-1, keepdims=True)
        y = (h * lax.rsqrt(ms + EPS) * ng) * jax.nn.sigmoid(oa)
        out_ref[0, out_rows, :] = y.astype(BF16)

    def emit_ctx(i, _):
        emit(i + ncl, oac_ref[0, chunk(i), :], hc_ref, chunk(i))
        return 0

    def emit_lat(i, _):
        emit(i, oal_ref[0, chunk(i), :], hl_ref, chunk(i))
        return 0

    lax.fori_loop(0, ncc, emit_ctx, 0, unroll=2)
    lax.fori_loop(0, ncl, emit_lat, 0, unroll=4)


def _mlstm(qa_l, kat_l, va_l, oa_l, gr_l, qa_c, kat_c, va_c, oa_c, gr_c, norm_g):
    batch, t_lat, _ = qa_l.shape
    t_ctx = qa_c.shape[1]
    t_all = t_ctx + t_lat
    nc = t_all // BLOCK
    nc_pad = -(-nc // 8) * 8
    head = lambda t: pl.BlockSpec((1, t, A_HEAD_DIM), lambda b, h: (b, 0, h))
    head_t = lambda t: pl.BlockSpec((1, A_HEAD_DIM, t), lambda b, h: (b, h, 0))
    gates = lambda t: pl.BlockSpec((1, GATE_ROWS, t // BLOCK, BLOCK), lambda b, h: (b, h, 0, 0))
    chunked = lambda g: g.reshape(batch, A_HEADS * GATE_ROWS, g.shape[2] // BLOCK, BLOCK)
    return pl.pallas_call(
        functools.partial(_mlstm_kernel, t_ctx=t_ctx, t_lat=t_lat),
        grid=(batch, A_HEADS),
        in_specs=[head(t_lat), head_t(t_lat), head(t_lat), head(t_lat), gates(t_lat),
                  head(t_ctx), head_t(t_ctx), head(t_ctx), head(t_ctx), gates(t_ctx),
                  pl.BlockSpec((1, A_HEAD_DIM), lambda b, h: (0, h))],
        out_specs=[head(t_lat), head(t_ctx)],
        out_shape=[jax.ShapeDtypeStruct((batch, t_lat, A_WIDTH), BF16),
                   jax.ShapeDtypeStruct((batch, t_ctx, A_WIDTH), BF16)],
        scratch_shapes=[
            pltpu.VMEM((t_all, A_HEAD_DIM), BF16),
            pltpu.VMEM((A_HEAD_DIM, t_all), BF16),
            pltpu.VMEM((t_all, A_HEAD_DIM), BF16),
            pltpu.VMEM((4, nc_pad, BLOCK), F32),
            pltpu.VMEM((6, nc_pad, BLOCK), F32),
            pltpu.VMEM((t_all, LANES), F32),
            pltpu.VMEM((4, nc_pad, LANES), F32),
            pltpu.VMEM((2, nc, A_HEAD_DIM, 2 * A_HEAD_DIM), F32),
            pltpu.VMEM((2, nc, 8, LANES), F32),
            pltpu.VMEM((2, A_HEAD_DIM, 2 * A_HEAD_DIM), F32),
        ],
        compiler_params=_params("arbitrary", "arbitrary"),
        name="mlstm",
    )(qa_l, kat_l, va_l, oa_l, chunked(gr_l), qa_c, kat_c, va_c, oa_c, chunked(gr_c),
      norm_g.reshape(1, A_WIDTH))


def _swa_kernel(sink_ref, q_ref, *refs, t_lat, has_window, blocks):
    if has_window:
        bias_ref, k_ref, v_ref, kx_ref, vx_ref, o_ref, s_scr = refs
    else:
        kx_ref, vx_ref, o_ref, s_scr = refs
    L = BLOCK
    nb = t_lat // L
    group = B_HEADS // B_KV_HEADS
    lane = lax.broadcasted_iota(jnp.int32, (L, LANES), 1)
    lo = lane < B_HEAD_DIM
    zero = jnp.zeros((L, LANES), BF16)
    def window(u):
        qblk = pl.program_id(1) * blocks + u
        start = pl.multiple_of(jnp.clip((qblk - 1) * L, 0, t_lat - 3 * L), L)
        return qblk, pl.ds(start, 3 * L)

    def logits(c, u, kvh):
        sl = slice(kvh * LANES, (kvh + 1) * LANES)
        q = q_ref[0, u * L:(u + 1) * L, :]
        qa = q[:, (2 * kvh) * LANES:(2 * kvh + 1) * LANES]
        qb = q[:, (2 * kvh + 1) * LANES:(2 * kvh + 2) * LANES]
        q4 = jnp.concatenate([jnp.where(lo, qa, zero), jnp.where(lo, zero, qa),
                              jnp.where(lo, qb, zero), jnp.where(lo, zero, qb)], axis=0)
        sink = jnp.concatenate([jnp.full((L, 1), sink_ref[kvh * group + g] * LOG2_E, F32)
                                for g in range(group)], axis=0)
        if has_window:
            qblk, win = window(u)
            bias = bias_ref[jnp.where(qblk == 0, 0, jnp.where(qblk == nb - 1, 2, 1))]
            keys = jnp.concatenate([k_ref[0, win, sl], kx_ref[0, :, sl]], axis=0)
            s = _dot_nt(q4, keys)
            s = (s.reshape(group, L, s.shape[1]) + bias[None]).reshape(s.shape)
        else:
            s = _dot_nt(q4, kx_ref[0, :, sl])
        s_scr[c] = s
        return jnp.maximum(jnp.max(s, axis=1, keepdims=True), sink), sink

    def attend(c, u, kvh, m, sink):
        sl = slice(kvh * LANES, (kvh + 1) * LANES)
        if has_window:
            vals = jnp.concatenate([v_ref[0, window(u)[1], sl], vx_ref[0, :, sl]], axis=0)
        else:
            vals = vx_ref[0, :, sl]
        e = jnp.exp2(s_scr[c] - m)
        denom = jnp.sum(e, axis=1, keepdims=True) + jnp.exp2(sink - m)
        o4 = _dot(e.astype(BF16), vals) / denom
        rows = slice(u * L, (u + 1) * L)
        o_ref[0, rows, (2 * kvh) * LANES:(2 * kvh + 1) * LANES] = (
            jnp.where(lo, o4[0:L], o4[L:2 * L]).astype(BF16))
        o_ref[0, rows, (2 * kvh + 1) * LANES:(2 * kvh + 2) * LANES] = (
            jnp.where(lo, o4[2 * L:3 * L], o4[3 * L:4 * L]).astype(BF16))

    chains = [(u, kvh) for u in range(blocks) for kvh in range(B_KV_HEADS)]
    stats = [logits(0, *chains[0])]
    for c, (u, kvh) in enumerate(chains):
        if c + 1 < len(chains):
            stats.append(logits(c + 1, *chains[c + 1]))
        attend(c, u, kvh, *stats[c])


def _band_bias(t_ctx):
    L = BLOCK
    t = jnp.arange(L)[:, None]
    j = jnp.arange(3 * L)[None, :]
    tables = []
    for rel in (0, -L, -2 * L):
        ok = jnp.abs(j - t + rel) <= WINDOW
        tables.append(jnp.concatenate([jnp.where(ok, 0.0, -jnp.inf), jnp.zeros((L, t_ctx))], axis=1))
    return jnp.stack(tables).astype(F32)


def _swa(sink, q, k, v, k_ctx, v_ctx, has_window):
    batch, t, _ = q.shape
    t_ctx = k_ctx.shape[1]
    nb = t // BLOCK
    blocks = min(SWA_BLOCKS, nb)
    assert nb % blocks == 0 and (nb >= 3 or not has_window)
    kvw = B_KV_HEADS * LANES
    cur = lambda b, i: (b, i, 0)
    whole = lambda b, i: (b, 0, 0)
    ctx_spec = pl.BlockSpec((1, t_ctx, kvw), whole)
    in_specs = [pl.BlockSpec(memory_space=pltpu.SMEM), pl.BlockSpec((1, blocks * BLOCK, B_WIDTH), cur)]
    args = [sink, q]
    if has_window:
        bias = _band_bias(t_ctx)
        in_specs += [pl.BlockSpec(bias.shape, lambda b, i: (0, 0, 0)),
                     pl.BlockSpec((1, t, kvw), whole), pl.BlockSpec((1, t, kvw), whole)]
        args += [bias, k, v]
    in_specs += [ctx_spec, ctx_spec]
    args += [k_ctx, v_ctx]
    return pl.pallas_call(
        functools.partial(_swa_kernel, t_lat=t, has_window=has_window, blocks=blocks),
        grid=(batch, nb // blocks),
        in_specs=in_specs,
        out_specs=pl.BlockSpec((1, blocks * BLOCK, B_WIDTH), cur),
        scratch_shapes=[pltpu.VMEM((blocks * B_KV_HEADS, (B_HEADS // B_KV_HEADS) * BLOCK,
                                    (3 * BLOCK if has_window else 0) + t_ctx), F32)],
        out_shape=jax.ShapeDtypeStruct((batch, t, B_WIDTH), BF16),
        compiler_params=_params("arbitrary", "arbitrary"),
        name="swa" if has_window else "swa_ctx",
    )(*args)


def _flash_kernel(q_ref, k_ref, v_ref, o_ref, s0_scr, s1_scr, m0_scr, m1_scr, *, chunks, tq):
    group = C_HEADS // C_KV_HEADS
    nq = q_ref.shape[1] // tq

    def q_rows(t):
        return pl.ds(pl.multiple_of(t * tq, tq), tq)

    def load_q(t):
        q = q_ref[0, q_rows(t), :]
        return jnp.concatenate([q[:, g * LANES:(g + 1) * LANES] for g in range(group)], axis=0)

    def stage_a(q4, s_ref, m, start, size):
        s = _dot_nt(q4, k_ref[0, start:start + size, :])
        s_ref[:, start:start + size] = s
        cm = jnp.max(s, axis=1, keepdims=True)
        return cm if m is None else jnp.maximum(m, cm)

    def stage_b(s_ref, m, l, acc, start, size):
        p = jnp.exp2(s_ref[:, start:start + size] - m)
        ps = jnp.sum(p, axis=1, keepdims=True)
        pv = _dot(p.astype(BF16), v_ref[0, start:start + size, :])
        return (ps, pv) if l is None else (l + ps, acc + pv)

    def step(t, cur, nxt):
        q4 = load_q(jnp.minimum(t + 1, nq - 1))
        m_cur = cur[1][...]
        m_next, l, acc = None, None, None
        for start, size in chunks:
            m_next = stage_a(q4, nxt[0], m_next, start, size)
            l, acc = stage_b(cur[0], m_cur, l, acc, start, size)
        nxt[1][...] = m_next
        out = acc / l
        for g in range(group):
            o_ref[0, q_rows(t), g * LANES:(g + 1) * LANES] = out[g * tq:(g + 1) * tq].astype(BF16)

    bufs = ((s0_scr, m0_scr), (s1_scr, m1_scr))
    q4 = load_q(0)
    m = None
    for start, size in chunks:
        m = stage_a(q4, s0_scr, m, start, size)
    m0_scr[...] = m

    def body(u, _):
        step(2 * u, bufs[0], bufs[1])
        step(2 * u + 1, bufs[1], bufs[0])
        return 0

    lax.fori_loop(0, nq // 2, body, 0)


def _key_chunks(total, target):
    chunks = []
    start = 0
    while start < total:
        size = min(target, total - start)
        chunks.append((start, size))
        start += size
    return tuple(chunks)


def _flash(q, k, v, t_ctx):
    batch, t, _ = q.shape
    tk_all = k.shape[1]
    tq = _row_tile(t, FLASH_ROWS)
    assert (t // tq) % 2 == 0
    group = C_HEADS // C_KV_HEADS
    gw = group * C_HEAD_DIM
    chunks = (_key_chunks(t_ctx, FLASH_KEYS)
              + tuple((t_ctx + a, n) for a, n in _key_chunks(tk_all - t_ctx, FLASH_KEYS)))
    return pl.pallas_call(
        functools.partial(_flash_kernel, chunks=chunks, tq=tq),
        grid=(batch, C_KV_HEADS),
        in_specs=[pl.BlockSpec((1, t, gw), lambda b, h: (b, 0, h)),
                  pl.BlockSpec((1, tk_all, C_HEAD_DIM), lambda b, h: (b, 0, h)),
                  pl.BlockSpec((1, tk_all, C_HEAD_DIM), lambda b, h: (b, 0, h))],
        out_specs=pl.BlockSpec((1, t, gw), lambda b, h: (b, 0, h)),
        out_shape=jax.ShapeDtypeStruct((batch, t, C_WIDTH), BF16),
        scratch_shapes=[pltpu.VMEM((group * tq, tk_all), F32), pltpu.VMEM((group * tq, tk_all), F32),
                        pltpu.VMEM((group * tq, 1), F32), pltpu.VMEM((group * tq, 1), F32)],
        compiler_params=_params("arbitrary", "arbitrary"),
        name="flash_c",
    )(q, k, v)


def _out_mlp_kernel(*refs, n_mix, ff_chunk):
    x_ref, mod_ref, g_ref = refs[0:3]
    mix_refs = refs[3:3 + n_mix]
    wo_refs = refs[3 + n_mix:3 + 2 * n_mix]
    w1_ref, w2_ref, o_ref = refs[3 + 2 * n_mix:]
    mod = mod_ref[0]
    y = None
    for a_ref, w_ref in zip(mix_refs, wo_refs):
        part = _dot(a_ref[0], w_ref[...])
        y = part if y is None else y + part
    x1 = x_ref[0] + mod[2:3] * y
    h = _modulated_norm(x1, g_ref[...], mod[3:4], mod[4:5]).astype(BF16)
    d_ff = w1_ref.shape[1]
    acc = None
    for c in range(d_ff // ff_chunk):
        a = _dot(h, w1_ref[:, c * ff_chunk:(c + 1) * ff_chunk])
        a = jnp.square(jnp.maximum(a, 0.0)).astype(BF16)
        part = _dot(a, w2_ref[c * ff_chunk:(c + 1) * ff_chunk, :])
        acc = part if acc is None else acc + part
    o_ref[0] = x1 + mod[5:6] * acc


def _out_mlp(x, mod, g, mixes, w_outs, w1, w2):
    batch, t, d = x.shape
    tm = _row_tile(t, MLP_ROWS)
    mod_map = (lambda b, i: (b, 0, 0)) if mod.shape[0] == batch else (lambda b, i: (0, 0, 0))
    const = lambda b, i: (0, 0)
    resident = lambda a: pl.BlockSpec(a.shape, const, pipeline_mode=pl.Buffered(1))
    row = lambda width: pl.BlockSpec((1, tm, width), lambda b, i: (b, i, 0))
    return pl.pallas_call(
        functools.partial(_out_mlp_kernel, n_mix=len(mixes), ff_chunk=1024),
        grid=(batch, t // tm),
        in_specs=([row(d), pl.BlockSpec((1, MOD_ROWS, d), mod_map), pl.BlockSpec((1, d), const)]
                  + [row(a.shape[2]) for a in mixes] + [resident(w) for w in w_outs]
                  + [resident(w1), resident(w2)]),
        out_specs=row(d),
        out_shape=jax.ShapeDtypeStruct((batch, t, d), F32),
        compiler_params=_params("arbitrary", "arbitrary"),
        name="out_mlp",
    )(x, mod, g, *mixes, *w_outs, w1, w2)


def kernel(x, c, ctx, c_ctx, ada_w, ada_b, norm1_g, norm2_g, ab_w_in, ab_gate_b, mlstm_norm_g, swa_q_norm_g,
           swa_k_norm_g, swa_sink, ab_w_out, c_w_in, c_q_norm_g, c_k_norm_g, c_w_out, mlp_w1, mlp_w2):
    depth = ada_w.shape[0]
    batch, _, d = x.shape
    mods = _mods(c, c_ctx, ada_w, ada_b)
    for layer in range(depth):
        last = layer == depth - 1
        mod_l = mods[layer, :batch]
        mod_c = mods[layer, batch:batch + 1]
        g1 = norm1_g[layer].reshape(1, d)
        g2 = norm2_g[layer].reshape(1, d)
        w1 = mlp_w1[layer].astype(BF16)
        w2 = mlp_w2[layer].astype(BF16)
        j = layer // 2
        if layer % 2 == 0:
            w, wt, gb = _ab_weights(ab_w_in[j], ab_gate_b[j])
            qn = jnp.tile(swa_q_norm_g[j], LANES // B_HEAD_DIM).reshape(1, LANES)
            kn = jnp.tile(swa_k_norm_g[j], LANES // B_HEAD_DIM).reshape(1, LANES)
            qa_l, kat_l, va_l, oa_l, qb_l, kb_l, vb_l, gr_l = _proj_ab(x, mod_l, g1, w, wt, gb, qn, kn, True)
            qa_c, kat_c, va_c, oa_c, qb_c, kb_c, vb_c, gr_c = _proj_ab(ctx, mod_c, g1, w, wt, gb, qn, kn, False)
            ha_l, ha_c = _mlstm(qa_l, kat_l, va_l, oa_l, gr_l, qa_c, kat_c, va_c, oa_c, gr_c, mlstm_norm_g[j])
            ob_l = _swa(swa_sink[j], qb_l, kb_l, vb_l, kb_c, vb_c, True)
            w_out = ab_w_out[j].astype(BF16)
            w_outs = [w_out[:A_WIDTH], w_out[A_WIDTH:]]
            x = _out_mlp(x, mod_l, g2, [ha_l, ob_l], w_outs, w1, w2)
            if not last:
                ob_c = _swa(swa_sink[j], qb_c, kb_c, vb_c, kb_c, vb_c, False)
                ctx = _out_mlp(ctx, mod_c, g2, [ha_c, ob_c], w_outs, w1, w2)
        else:
            w = c_w_in[j].astype(BF16)
            qn = c_q_norm_g[j].reshape(1, LANES)
            kn = c_k_norm_g[j].reshape(1, LANES)
            q_l, k_l, v_l = _proj_c(x, mod_l, g1, w, qn, kn, True)
            q_c, k_c, v_c = _proj_c(ctx, mod_c, g1, w, qn, kn, False)
            k_all = jnp.concatenate([k_c, k_l], axis=1)
            v_all = jnp.concatenate([v_c, v_l], axis=1)
            w_out = c_w_out[j].astype(BF16)
            o_l = _flash(q_l, k_all, v_all, k_c.shape[1])
            x = _out_mlp(x, mod_l, g2, [o_l], [w_out], w1, w2)
            if not last:
                o_c = _flash(q_c, k_c, v_c, 0)
                ctx = _out_mlp(ctx, mod_c, g2, [o_c], [w_out], w1, w2)
    return x
```

```python
import functools

import jax
import jax.numpy as jnp
from jax import lax
from jax.experimental import pallas as pl
from jax.experimental.pallas import tpu as pltpu

F32 = jnp.float32
BF16 = jnp.bfloat16

GRID_W = 64
BLOCK = 128
WINDOW = 128
ROPE_BASE = 10000.0
EPS = 1e-6
LOG2_E = 1.4426950408889634
PROJ_ROWS = 512
MLP_ROWS = 512
SWA_BLOCKS = 4
FLASH_KEYS = 512
FLASH_PARTS = 1
FLASH_ROWS = 128
N_MOD = 6
MOD_ROWS = 8
LANES = 128
MXU_WIDTH = 256

A_HEADS = 4
A_HEAD_DIM = 128
A_WIDTH = A_HEADS * A_HEAD_DIM
A_GATES = 4 * A_HEADS
GATE_ROWS = 8
B_HEADS = 8
B_KV_HEADS = 2
B_HEAD_DIM = 64
B_WIDTH = B_HEADS * B_HEAD_DIM
B_KV_WIDTH = B_KV_HEADS * B_HEAD_DIM
C_HEADS = 8
C_KV_HEADS = 2
C_HEAD_DIM = 128
C_WIDTH = C_HEADS * C_HEAD_DIM
C_KV_WIDTH = C_KV_HEADS * C_HEAD_DIM

VMEM_LIMIT = 56 * 1024 * 1024

NT_DIMS = (((1,), (1,)), ((), ()))
TN_DIMS = (((0,), (0,)), ((), ()))


def _dot(a, b):
    return jnp.dot(a, b, preferred_element_type=F32)


def _dot_nt(a, b):
    return lax.dot_general(a, b, NT_DIMS, preferred_element_type=F32)


def _dot_tn(a, b):
    return lax.dot_general(a, b, TN_DIMS, preferred_element_type=F32)


def _params(*sem):
    return pltpu.CompilerParams(dimension_semantics=sem, vmem_limit_bytes=VMEM_LIMIT)


def _mods_kernel(c_ref, w_ref, b_ref, o_ref):
    cf = c_ref[...]
    s = (cf * jax.nn.sigmoid(cf)).astype(BF16)
    o_ref[0] = _dot(s, w_ref[0].astype(BF16)) + b_ref[0]


def _mods(c, c_ctx, ada_w, ada_b):
    depth, d, _ = ada_w.shape
    batch = c.shape[0]
    rows = -(-(batch + 1) // 8) * 8
    cc = jnp.zeros((rows, d), F32).at[:batch].set(c).at[batch].set(c_ctx)
    out = pl.pallas_call(
        _mods_kernel,
        grid=(depth, N_MOD),
        in_specs=[
            pl.BlockSpec((rows, d), lambda l, j: (0, 0)),
            pl.BlockSpec((1, d, d), lambda l, j: (l, 0, j)),
            pl.BlockSpec((1, 1, d), lambda l, j: (l, 0, j)),
        ],
        out_specs=pl.BlockSpec((1, rows, d), lambda l, j: (l, 0, j)),
        out_shape=jax.ShapeDtypeStruct((depth, rows, N_MOD * d), F32),
        compiler_params=_params("arbitrary", "arbitrary"),
        name="ada_mods",
    )(cc, ada_w, ada_b.reshape(depth, 1, N_MOD * d))
    out = out.reshape(depth, rows, N_MOD, d)
    return jnp.pad(out, ((0, 0), (0, 0), (0, MOD_ROWS - N_MOD), (0, 0)))


def _modulated_norm(x, g, shift, scale):
    ms = jnp.mean(x * x, axis=-1, keepdims=True)
    return (x * lax.rsqrt(ms + EPS) * g) * (1.0 + scale) + shift


def _split3(a):
    hi = a.astype(BF16)
    r = a - hi.astype(F32)
    mid = r.astype(BF16)
    lo = (r - mid.astype(F32)).astype(BF16)
    return hi, mid, lo


def _head_norm_rope(xh, gain, cos, sin, group, out_scale):
    sq = xh * xh
    half = group // 2
    if group == LANES:
        ssq = jnp.sum(sq, axis=-1, keepdims=True)
    else:
        assert 2 * group == LANES
        lane = lax.broadcasted_iota(jnp.int32, xh.shape, 1)
        left = lane < group
        ssq = jnp.where(left, jnp.sum(jnp.where(left, sq, 0.0), axis=-1, keepdims=True),
                        jnp.sum(jnp.where(left, 0.0, sq), axis=-1, keepdims=True))
    xn = xh * lax.rsqrt(ssq * (1.0 / group) + EPS) * gain
    if group == LANES:
        rot = pltpu.roll(xn, half, 1)
    else:
        first = (lane & (group - 1)) < half
        rot = jnp.where(first, pltpu.roll(xn, LANES - half, 1), pltpu.roll(xn, half, 1))
    y = xn * cos + rot * sin
    if out_scale != 1.0:
        y = y * out_scale
    return y


def _rope_tables(n_tokens, head_dim, use_rope):
    reps = LANES // head_dim
    if not use_rope:
        return jnp.ones((n_tokens, LANES), F32), jnp.zeros((n_tokens, LANES), F32)
    rows = n_tokens // GRID_W
    row = jnp.broadcast_to(jnp.arange(rows)[:, None], (rows, GRID_W)).reshape(n_tokens).astype(F32)
    col = jnp.broadcast_to(jnp.arange(GRID_W)[None, :], (rows, GRID_W)).reshape(n_tokens).astype(F32)
    pairs = head_dim // 4
    inv_freq = ROPE_BASE ** (-jnp.arange(pairs, dtype=F32) / pairs)
    ang = jnp.concatenate([row[:, None] * inv_freq, col[:, None] * inv_freq], axis=-1)
    cos, sin = jnp.cos(ang), jnp.sin(ang)
    cos_full = jnp.concatenate([cos, cos], axis=-1)
    sin_signed = jnp.concatenate([-sin, sin], axis=-1)
    return jnp.tile(cos_full, (1, reps)), jnp.tile(sin_signed, (1, reps))


def _row_tile(t, target):
    tm = min(t, target)
    assert t % tm == 0
    return tm


def _proj_ab_kernel(x_ref, mod_ref, g_ref, w_ref, wt_ref, gb_ref, cos_ref, sin_ref, qn_ref, kn_ref,
                    qa_ref, kat_ref, va_ref, oa_ref, qb_ref, kb_ref, vb_ref, gr_ref):
    mod = mod_ref[0]
    h = _modulated_norm(x_ref[0], g_ref[...], mod[0:1], mod[1:2]).astype(BF16)

    def mm(lo, n):
        return _dot(h, w_ref[:, lo:lo + n])

    qa_ref[0] = mm(0, A_WIDTH).astype(BF16)
    va_ref[0] = mm(A_WIDTH, A_WIDTH).astype(BF16)
    oa_ref[0] = mm(2 * A_WIDTH, A_WIDTH)
    cos, sin = cos_ref[...], sin_ref[...]
    base = 3 * A_WIDTH
    for j in range(B_WIDTH // LANES):
        y = _head_norm_rope(mm(base + LANES * j, LANES), qn_ref[...], cos, sin, B_HEAD_DIM,
                            B_HEAD_DIM ** -0.5 * LOG2_E)
        qb_ref[0, :, LANES * j:LANES * (j + 1)] = y.astype(BF16)
    base += B_WIDTH
    for j in range(B_KV_HEADS):
        y = _head_norm_rope(mm(base + LANES * j, LANES), kn_ref[...], cos, sin, B_HEAD_DIM, 1.0)
        kb_ref[0, :, LANES * j:LANES * (j + 1)] = y.astype(BF16)
    base += B_KV_HEADS * LANES
    vb_ref[0] = mm(base, B_KV_HEADS * LANES).astype(BF16)
    tr = _dot_nt(wt_ref[...], h)
    kat_ref[0] = (tr[0:A_WIDTH] * A_HEAD_DIM ** -0.5).astype(BF16)
    gt = tr[A_WIDTH:] + gb_ref[...]
    typ = lax.broadcasted_iota(jnp.int32, gt.shape, 0) & (GATE_ROWS - 1)
    log_sig = jnp.minimum(gt, 0.0) - jnp.log1p(jnp.exp(-jnp.abs(gt)))
    gr_ref[0] = jnp.where((typ == 1) | (typ == 3), log_sig, gt)


def _proj_ab(x, mod, g, w, wg, gb, qn, kn, use_rope):
    batch, t, d = x.shape
    tm = _row_tile(t, PROJ_ROWS)
    cos, sin = _rope_tables(t, B_HEAD_DIM, use_rope)
    mod_map = (lambda b, i: (b, 0, 0)) if mod.shape[0] == batch else (lambda b, i: (0, 0, 0))
    const = lambda b, i: (0, 0)
    row = lambda width: pl.BlockSpec((1, tm, width), lambda b, i: (b, i, 0))
    kvw = B_KV_HEADS * LANES
    out_shape = [
        jax.ShapeDtypeStruct((batch, t, A_WIDTH), BF16),
        jax.ShapeDtypeStruct((batch, A_WIDTH, t), BF16),
        jax.ShapeDtypeStruct((batch, t, A_WIDTH), BF16),
        jax.ShapeDtypeStruct((batch, t, A_WIDTH), F32),
        jax.ShapeDtypeStruct((batch, t, B_WIDTH), BF16),
        jax.ShapeDtypeStruct((batch, t, kvw), BF16),
        jax.ShapeDtypeStruct((batch, t, kvw), BF16),
        jax.ShapeDtypeStruct((batch, A_HEADS * GATE_ROWS, t), F32),
    ]
    col = lambda height: pl.BlockSpec((1, height, tm), lambda b, i: (b, 0, i))
    out_specs = [row(A_WIDTH), col(A_WIDTH), row(A_WIDTH), row(A_WIDTH), row(B_WIDTH), row(kvw), row(kvw),
                 col(A_HEADS * GATE_ROWS)]
    return pl.pallas_call(
        _proj_ab_kernel,
        grid=(batch, t // tm),
        in_specs=[
            row(d),
            pl.BlockSpec((1, MOD_ROWS, d), mod_map),
            pl.BlockSpec((1, d), const),
            pl.BlockSpec(w.shape, const),
            pl.BlockSpec(wg.shape, const),
            pl.BlockSpec(gb.shape, const),
            pl.BlockSpec((tm, LANES), lambda b, i: (i, 0)),
            pl.BlockSpec((tm, LANES), lambda b, i: (i, 0)),
            pl.BlockSpec((1, LANES), const),
            pl.BlockSpec((1, LANES), const),
        ],
        out_specs=out_specs,
        out_shape=out_shape,
        compiler_params=_params("arbitrary", "arbitrary"),
        name="proj_ab",
    )(x, mod, g, w, wg, gb, cos, sin, qn, kn)


def _ab_weights(w_in, gate_b):
    bounds = [A_WIDTH * 4, A_WIDTH * 4 + A_GATES, A_WIDTH * 4 + A_GATES + B_WIDTH,
              A_WIDTH * 4 + A_GATES + B_WIDTH + B_KV_WIDTH]
    wa, wgate, wqb, wkb, wvb = jnp.split(w_in, bounds, axis=1)
    wqa, wka, wva_oa = wa[:, :A_WIDTH], wa[:, A_WIDTH:2 * A_WIDTH], wa[:, 2 * A_WIDTH:]

    def dup(wk):
        parts = []
        for hh in range(B_KV_HEADS):
            blk = wk[:, hh * B_HEAD_DIM:(hh + 1) * B_HEAD_DIM]
            parts += [blk] * (LANES // B_HEAD_DIM)
        return jnp.concatenate(parts, axis=1)

    w = jnp.concatenate([wqa, wva_oa, wqb, dup(wkb), dup(wvb)], axis=1).astype(BF16)
    d = w_in.shape[0]
    wg = wgate.reshape(d, 4, A_HEADS).transpose(2, 1, 0)
    wg = jnp.pad(wg, ((0, 0), (0, GATE_ROWS - 4), (0, 0))).reshape(A_HEADS * GATE_ROWS, d)
    wt = jnp.concatenate([wka.T, wg], axis=0).astype(BF16)
    gb = gate_b.reshape(4, A_HEADS).T
    gb = jnp.pad(gb, ((0, 0), (0, GATE_ROWS - 4))).reshape(A_HEADS * GATE_ROWS, 1).astype(F32)
    return w, wt, gb


def _proj_c_kernel(x_ref, mod_ref, g_ref, w_ref, cos_ref, sin_ref, qn_ref, kn_ref, q_ref, k_ref, v_ref):
    mod = mod_ref[0]
    h = _modulated_norm(x_ref[0], g_ref[...], mod[0:1], mod[1:2]).astype(BF16)

    def mm(lo, n):
        return _dot(h, w_ref[:, lo:lo + n])

    cos, sin = cos_ref[...], sin_ref[...]
    q_scale = C_HEAD_DIM ** -0.5 * LOG2_E
    for j in range(C_HEADS):
        y = _head_norm_rope(mm(LANES * j, LANES), qn_ref[...], cos, sin, C_HEAD_DIM, q_scale)
        q_ref[0, :, LANES * j:LANES * (j + 1)] = y.astype(BF16)
    for j in range(C_KV_HEADS):
        y = _head_norm_rope(mm(C_WIDTH + LANES * j, LANES), kn_ref[...], cos, sin, C_HEAD_DIM, 1.0)
        k_ref[0, :, LANES * j:LANES * (j + 1)] = y.astype(BF16)
    v_ref[0] = mm(C_WIDTH + C_KV_WIDTH, C_KV_WIDTH).astype(BF16)


def _proj_c(x, mod, g, w, qn, kn, use_rope):
    batch, t, d = x.shape
    tm = _row_tile(t, PROJ_ROWS)
    cos, sin = _rope_tables(t, C_HEAD_DIM, use_rope)
    mod_map = (lambda b, i: (b, 0, 0)) if mod.shape[0] == batch else (lambda b, i: (0, 0, 0))
    const = lambda b, i: (0, 0)
    row = lambda width: pl.BlockSpec((1, tm, width), lambda b, i: (b, i, 0))
    return pl.pallas_call(
        _proj_c_kernel,
        grid=(batch, t // tm),
        in_specs=[
            row(d),
            pl.BlockSpec((1, MOD_ROWS, d), mod_map),
            pl.BlockSpec((1, d), const),
            pl.BlockSpec(w.shape, const),
            pl.BlockSpec((tm, LANES), lambda b, i: (i, 0)),
            pl.BlockSpec((tm, LANES), lambda b, i: (i, 0)),
            pl.BlockSpec((1, LANES), const),
            pl.BlockSpec((1, LANES), const),
        ],
        out_specs=[row(C_WIDTH), row(C_KV_WIDTH), row(C_KV_WIDTH)],
        out_shape=[jax.ShapeDtypeStruct((batch, t, C_WIDTH), BF16),
                   jax.ShapeDtypeStruct((batch, t, C_KV_WIDTH), BF16),
                   jax.ShapeDtypeStruct((batch, t, C_KV_WIDTH), BF16)],
        compiler_params=_params("arbitrary", "arbitrary"),
        name="proj_c",
    )(x, mod, g, w, cos, sin, qn, kn)


def _mlstm_kernel(ql_ref, ktl_ref, vl_ref, oal_ref, gl_ref, qc_ref, ktc_ref, vc_ref, oac_ref, gc_ref, ng_ref,
                  hl_ref, hc_ref,
                  qs, kts, vs, gs, rowq, rowg, cols, stats, cst, mst, cstate, *, t_ctx, t_lat):
    L = BLOCK
    ncc = t_ctx // L
    ncl = t_lat // L
    nc = ncc + ncl

    qs[0:t_lat] = ql_ref[0]
    qs[t_lat:] = qc_ref[0]
    kts[:, 0:t_lat] = ktl_ref[0]
    kts[:, t_lat:] = ktc_ref[0]
    vs[0:t_lat] = vl_ref[0]
    vs[t_lat:] = vc_ref[0]
    for r in range(4):
        gs[r, 0:ncl] = gl_ref[0, r]
        gs[r, ncl:nc] = gc_ref[0, r]

    ri = lax.broadcasted_iota(jnp.int32, (L, L), 0)
    ci = lax.broadcasted_iota(jnp.int32, (L, L), 1)
    lower = ci <= ri
    upper = ci >= ri
    ones_blk = jnp.ones((L, L), BF16)

    def chunk(i):
        return pl.ds(pl.multiple_of(i * L, L), L)

    def cum(x, mat):
        hi, mid, lo = _split3(x)
        return _dot(hi, mat) + _dot(mid, mat) + _dot(lo, mat)

    li_f, lf_f, li_b, lf_b = (gs[r, 0:nc] for r in range(4))
    bcum_f = cum(lf_f, jnp.where(upper, 1.0, 0.0).astype(BF16))
    bsuf_b = cum(lf_b, jnp.where(lower, 1.0, 0.0).astype(BF16))
    bl_f = bcum_f[:, L - 1:L]
    bl_b = bsuf_b[:, 0:1]
    wl_f = bl_f - bcum_f + li_f
    wl_b = bl_b - bsuf_b + li_b
    mx_f = jnp.max(wl_f, axis=1, keepdims=True)
    mx_b = jnp.max(wl_b, axis=1, keepdims=True)
    rb_f = li_f - bcum_f
    rb_b = li_b - bsuf_b
    lane = lax.broadcasted_iota(jnp.int32, (nc, L), 1)
    pm_f, pm_b = rb_f, rb_b
    step = 1
    while step < L:
        pm_f = jnp.where(lane >= step, jnp.maximum(pm_f, pltpu.roll(pm_f, step, 1)), pm_f)
        pm_b = jnp.where(lane < L - step, jnp.maximum(pm_b, pltpu.roll(pm_b, L - step, 1)), pm_b)
        step *= 2
    for n, val in enumerate((bcum_f, bsuf_b, rb_f, rb_b, jnp.exp(wl_f - mx_f), jnp.exp(wl_b - mx_b), pm_f, pm_b)):
        rowq[n, 0:nc] = val
    for n, val in enumerate((bl_f, bl_b, mx_f, mx_b)):
        stats[n, 0:nc] = jnp.broadcast_to(val, (nc, L))

    def prep(i, _):
        kt = kts[:, chunk(i)].astype(F32)
        vaug = jnp.concatenate([vs[chunk(i), :], ones_blk], axis=1)
        for d in range(2):
            cst[d, i] = _dot((kt * rowq[4 + d, pl.ds(i, 1), :]).astype(BF16), vaug)
        return 0

    lax.fori_loop(0, nc, prep, 0, unroll=2)

    cstate[...] = jnp.zeros_like(cstate)

    def scan_step(j, carry):
        new = []
        for d in range(2):
            m = carry[d]
            if d == 0:
                i = jnp.where(j < ncc, ncl + j, j - ncc)
            else:
                i = nc - 1 - j
            c_old = cstate[d]
            kv = cst[d, i]
            cst[d, i] = c_old
            mst[d, pl.ds(i, 1), :] = jnp.broadcast_to(m, (1, L))
            bl = stats[d, pl.ds(i, 1), 0:1]
            mx = stats[2 + d, pl.ds(i, 1), 0:1]
            m_new = jnp.maximum(bl + m, mx)
            cstate[d] = jnp.exp(bl + m - m_new) * c_old + jnp.exp(mx - m_new) * kv
            new.append(m_new)
        return tuple(new)

    m0 = jnp.zeros((1, 1), F32)
    lax.fori_loop(0, nc, scan_step, (m0, m0))

    for d in range(2):
        g = jnp.maximum(mst[d, 0:nc], rowq[6 + d, 0:nc])
        rowg[d, 0:nc] = g
        rowg[2 + d, 0:nc] = jnp.exp(-rowq[d, 0:nc] - g)

    def to_cols(i, _):
        r8 = jnp.concatenate([rowg[n, pl.ds(i, 1), :] for n in range(4)] + [jnp.zeros((4, L), F32)], axis=0)
        cols[chunk(i), :] = jnp.concatenate([r8, jnp.zeros((L - 8, L), F32)], axis=0).T
        return 0

    lax.fori_loop(0, nc, to_cols, 0, unroll=2)

    ng = ng_ref[...]

    def emit(i, oa, out_ref, out_rows):
        q = qs[chunk(i), :]
        vaug = jnp.concatenate([vs[chunk(i), :], ones_blk], axis=1)
        qk = _dot(q, kts[:, chunk(i)])
        colblk = cols[chunk(i), :]
        h = None
        for d in range(2):
            g = jnp.broadcast_to(colblk[:, d:d + 1], (L, L))
            floor = colblk[:, 2 + d:3 + d]
            rb = rowq[2 + d, pl.ds(i, 1), :]
            mask = lower if d == 0 else upper
            dm = jnp.where(mask, jnp.exp(rb - g), 0.0)
            s = (qk * dm).astype(BF16)
            iw = jnp.exp(mst[d, pl.ds(i, 1), 0:1] - g)
            nd = jnp.concatenate([iw, iw], axis=1) * _dot(q, cst[d, i].astype(BF16)) + _dot(s, vaug)
            hd = nd[:, 0:L] / jnp.maximum(jnp.abs(nd[:, L:2 * L]), floor)
            h = hd if h is None else h + hd
        ms = jnp.mean(h * h, axis=-1, keepdims=True)
        y = (h * lax.rsqrt(ms + EPS) * ng) * jax.nn.sigmoid(oa)
        out_ref[0, out_rows, :] = y.astype(BF16)

    def emit_ctx(i, _):
        emit(i + ncl, oac_ref[0, chunk(i), :], hc_ref, chunk(i))
        return 0

    def emit_lat(i, _):
        emit(i, oal_ref[0, chunk(i), :], hl_ref, chunk(i))
        return 0

    lax.fori_loop(0, ncc, emit_ctx, 0, unroll=2)
    lax.fori_loop(0, ncl, emit_lat, 0, unroll=4)


def _mlstm(qa_l, kat_l, va_l, oa_l, gr_l, qa_c, kat_c, va_c, oa_c, gr_c, norm_g):
    batch, t_lat, _ = qa_l.shape
    t_ctx = qa_c.shape[1]
    t_all = t_ctx + t_lat
    nc = t_all // BLOCK
    nc_pad = -(-nc // 8) * 8
    head = lambda t: pl.BlockSpec((1, t, A_HEAD_DIM), lambda b, h: (b, 0, h))
    head_t = lambda t: pl.BlockSpec((1, A_HEAD_DIM, t), lambda b, h: (b, h, 0))
    gates = lambda t: pl.BlockSpec((1, GATE_ROWS, t // BLOCK, BLOCK), lambda b, h: (b, h, 0, 0))
    chunked = lambda g: g.reshape(batch, A_HEADS * GATE_ROWS, g.shape[2] // BLOCK, BLOCK)
    return pl.pallas_call(
        functools.partial(_mlstm_kernel, t_ctx=t_ctx, t_lat=t_lat),
        grid=(batch, A_HEADS),
        in_specs=[head(t_lat), head_t(t_lat), head(t_lat), head(t_lat), gates(t_lat),
                  head(t_ctx), head_t(t_ctx), head(t_ctx), head(t_ctx), gates(t_ctx),
                  pl.BlockSpec((1, A_HEAD_DIM), lambda b, h: (0, h))],
        out_specs=[head(t_lat), head(t_ctx)],
        out_shape=[jax.ShapeDtypeStruct((batch, t_lat, A_WIDTH), BF16),
                   jax.ShapeDtypeStruct((batch, t_ctx, A_WIDTH), BF16)],
        scratch_shapes=[
            pltpu.VMEM((t_all, A_HEAD_DIM), BF16),
            pltpu.VMEM((A_HEAD_DIM, t_all), BF16),
            pltpu.VMEM((t_all, A_HEAD_DIM), BF16),
            pltpu.VMEM((4, nc_pad, BLOCK), F32),
            pltpu.VMEM((8, nc_pad, BLOCK), F32),
            pltpu.VMEM((4, nc_pad, BLOCK), F32),
            pltpu.VMEM((t_all, LANES), F32),
            pltpu.VMEM((4, nc_pad, LANES), F32),
            pltpu.VMEM((2, nc, A_HEAD_DIM, 2 * A_HEAD_DIM), F32),
            pltpu.VMEM((2, nc_pad, LANES), F32),
            pltpu.VMEM((2, A_HEAD_DIM, 2 * A_HEAD_DIM), F32),
        ],
        compiler_params=_params("arbitrary", "arbitrary"),
        name="mlstm",
    )(qa_l, kat_l, va_l, oa_l, chunked(gr_l), qa_c, kat_c, va_c, oa_c, chunked(gr_c),
      norm_g.reshape(1, A_WIDTH))


def _swa_kernel(sink_ref, q_ref, *refs, t_lat, has_window, blocks):
    if has_window:
        bias_ref, k_ref, v_ref, kx_ref, vx_ref, o_ref, s_scr = refs
    else:
        kx_ref, vx_ref, o_ref, s_scr = refs
    L = BLOCK
    nb = t_lat // L
    group = B_HEADS // B_KV_HEADS
    lane = lax.broadcasted_iota(jnp.int32, (L, LANES), 1)
    lo = lane < B_HEAD_DIM
    zero = jnp.zeros((L, LANES), BF16)
    def window(u):
        qblk = pl.program_id(1) * blocks + u
        start = pl.multiple_of(jnp.clip((qblk - 1) * L, 0, t_lat - 3 * L), L)
        return qblk, pl.ds(start, 3 * L)

    def logits(c, u, kvh):
        sl = slice(kvh * LANES, (kvh + 1) * LANES)
        q = q_ref[0, u * L:(u + 1) * L, :]
        qa = q[:, (2 * kvh) * LANES:(2 * kvh + 1) * LANES]
        qb = q[:, (2 * kvh + 1) * LANES:(2 * kvh + 2) * LANES]
        q4 = jnp.concatenate([jnp.where(lo, qa, zero), jnp.where(lo, zero, qa),
                              jnp.where(lo, qb, zero), jnp.where(lo, zero, qb)], axis=0)
        sink = jnp.concatenate([jnp.full((L, 1), sink_ref[kvh * group + g] * LOG2_E, F32)
                                for g in range(group)], axis=0)
        if has_window:
            qblk, win = window(u)
            bias = bias_ref[jnp.where(qblk == 0, 0, jnp.where(qblk == nb - 1, 2, 1))]
            keys = jnp.concatenate([k_ref[0, win, sl], kx_ref[0, :, sl]], axis=0)
            s = _dot_nt(q4, keys)
            s = (s.reshape(group, L, s.shape[1]) + bias[None]).reshape(s.shape)
        else:
            s = _dot_nt(q4, kx_ref[0, :, sl])
        s_scr[c] = s
        return jnp.maximum(jnp.max(s, axis=1, keepdims=True), sink), sink

    def attend(c, u, kvh, m, sink):
        sl = slice(kvh * LANES, (kvh + 1) * LANES)
        if has_window:
            vals = jnp.concatenate([v_ref[0, window(u)[1], sl], vx_ref[0, :, sl]], axis=0)
        else:
            vals = vx_ref[0, :, sl]
        e = jnp.exp2(s_scr[c] - m)
        denom = jnp.sum(e, axis=1, keepdims=True) + jnp.exp2(sink - m)
        o4 = _dot(e.astype(BF16), vals) / denom
        rows = slice(u * L, (u + 1) * L)
        o_ref[0, rows, (2 * kvh) * LANES:(2 * kvh + 1) * LANES] = (
            jnp.where(lo, o4[0:L], o4[L:2 * L]).astype(BF16))
        o_ref[0, rows, (2 * kvh + 1) * LANES:(2 * kvh + 2) * LANES] = (
            jnp.where(lo, o4[2 * L:3 * L], o4[3 * L:4 * L]).astype(BF16))

    chains = [(u, kvh) for u in range(blocks) for kvh in range(B_KV_HEADS)]
    stats = [logits(0, *chains[0])]
    for c, (u, kvh) in enumerate(chains):
        if c + 1 < len(chains):
            stats.append(logits(c + 1, *chains[c + 1]))
        attend(c, u, kvh, *stats[c])


def _band_bias(t_ctx):
    L = BLOCK
    t = jnp.arange(L)[:, None]
    j = jnp.arange(3 * L)[None, :]
    tables = []
    for rel in (0, -L, -2 * L):
        ok = jnp.abs(j - t + rel) <= WINDOW
        tables.append(jnp.concatenate([jnp.where(ok, 0.0, -jnp.inf), jnp.zeros((L, t_ctx))], axis=1))
    return jnp.stack(tables).astype(F32)


def _swa(sink, q, k, v, k_ctx, v_ctx, has_window):
    batch, t, _ = q.shape
    t_ctx = k_ctx.shape[1]
    nb = t // BLOCK
    blocks = min(SWA_BLOCKS, nb)
    assert nb % blocks == 0 and (nb >= 3 or not has_window)
    kvw = B_KV_HEADS * LANES
    cur = lambda b, i: (b, i, 0)
    whole = lambda b, i: (b, 0, 0)
    ctx_spec = pl.BlockSpec((1, t_ctx, kvw), whole)
    in_specs = [pl.BlockSpec(memory_space=pltpu.SMEM), pl.BlockSpec((1, blocks * BLOCK, B_WIDTH), cur)]
    args = [sink, q]
    if has_window:
        bias = _band_bias(t_ctx)
        in_specs += [pl.BlockSpec(bias.shape, lambda b, i: (0, 0, 0)),
                     pl.BlockSpec((1, t, kvw), whole), pl.BlockSpec((1, t, kvw), whole)]
        args += [bias, k, v]
    in_specs += [ctx_spec, ctx_spec]
    args += [k_ctx, v_ctx]
    return pl.pallas_call(
        functools.partial(_swa_kernel, t_lat=t, has_window=has_window, blocks=blocks),
        grid=(batch, nb // blocks),
        in_specs=in_specs,
        out_specs=pl.BlockSpec((1, blocks * BLOCK, B_WIDTH), cur),
        scratch_shapes=[pltpu.VMEM((blocks * B_KV_HEADS, (B_HEADS // B_KV_HEADS) * BLOCK,
                                    (3 * BLOCK if has_window else 0) + t_ctx), F32)],
        out_shape=jax.ShapeDtypeStruct((batch, t, B_WIDTH), BF16),
        compiler_params=_params("arbitrary", "arbitrary"),
        name="swa" if has_window else "swa_ctx",
    )(*args)


def _flash_kernel(q_ref, k_ref, v_ref, o_ref, s0_scr, s1_scr, m0_scr, m1_scr, *, chunks, tq):
    group = C_HEADS // C_KV_HEADS
    nq = q_ref.shape[1] // tq

    def q_rows(t):
        return pl.ds(pl.multiple_of(t * tq, tq), tq)

    def load_q(t):
        q = q_ref[0, q_rows(t), :]
        return jnp.concatenate([q[:, g * LANES:(g + 1) * LANES] for g in range(group)], axis=0)

    def stage_a(q4, s_ref, m, start, size):
        s = _dot_nt(q4, k_ref[0, start:start + size, :])
        s_ref[:, start:start + size] = s
        cm = jnp.max(s, axis=1, keepdims=True)
        return cm if m is None else jnp.maximum(m, cm)

    def stage_b(s_ref, m, l, acc, start, size):
        p = jnp.exp2(s_ref[:, start:start + size] - m)
        ps = jnp.sum(p, axis=1, keepdims=True)
        pv = _dot(p.astype(BF16), v_ref[0, start:start + size, :])
        return (ps, pv) if l is None else (l + ps, acc + pv)

    def step(t, cur, nxt):
        q4 = load_q(jnp.minimum(t + 1, nq - 1))
        m_cur = cur[1][...]
        m_next, l, acc = None, None, None
        for start, size in chunks:
            m_next = stage_a(q4, nxt[0], m_next, start, size)
            l, acc = stage_b(cur[0], m_cur, l, acc, start, size)
        nxt[1][...] = m_next
        out = acc / l
        for g in range(group):
            o_ref[0, q_rows(t), g * LANES:(g + 1) * LANES] = out[g * tq:(g + 1) * tq].astype(BF16)

    bufs = ((s0_scr, m0_scr), (s1_scr, m1_scr))
    q4 = load_q(0)
    m = None
    for start, size in chunks:
        m = stage_a(q4, s0_scr, m, start, size)
    m0_scr[...] = m

    def body(u, _):
        step(2 * u, bufs[0], bufs[1])
        step(2 * u + 1, bufs[1], bufs[0])
        return 0

    lax.fori_loop(0, nq // 2, body, 0)


def _key_chunks(total, target):
    chunks = []
    start = 0
    while start < total:
        size = min(target, total - start)
        chunks.append((start, size))
        start += size
    return tuple(chunks)


def _flash(q, k, v, t_ctx):
    batch, t, _ = q.shape
    tk_all = k.shape[1]
    tq = _row_tile(t, FLASH_ROWS)
    parts = FLASH_PARTS if t % (FLASH_PARTS * 2 * tq) == 0 else 1
    tp = t // parts
    assert (tp // tq) % 2 == 0
    group = C_HEADS // C_KV_HEADS
    gw = group * C_HEAD_DIM
    chunks = (_key_chunks(t_ctx, FLASH_KEYS)
              + tuple((t_ctx + a, n) for a, n in _key_chunks(tk_all - t_ctx, FLASH_KEYS)))
    return pl.pallas_call(
        functools.partial(_flash_kernel, chunks=chunks, tq=tq),
        grid=(batch, C_KV_HEADS, parts),
        in_specs=[pl.BlockSpec((1, tp, gw), lambda b, h, p: (b, p, h)),
                  pl.BlockSpec((1, tk_all, C_HEAD_DIM), lambda b, h, p: (b, 0, h)),
                  pl.BlockSpec((1, tk_all, C_HEAD_DIM), lambda b, h, p: (b, 0, h))],
        out_specs=pl.BlockSpec((1, tp, gw), lambda b, h, p: (b, p, h)),
        out_shape=jax.ShapeDtypeStruct((batch, t, C_WIDTH), BF16),
        scratch_shapes=[pltpu.VMEM((group * tq, tk_all), F32), pltpu.VMEM((group * tq, tk_all), F32),
                        pltpu.VMEM((group * tq, 1), F32), pltpu.VMEM((group * tq, 1), F32)],
        compiler_params=_params("arbitrary", "arbitrary", "arbitrary"),
        name="flash_c",
    )(q, k, v)


def _out_mlp_kernel(*refs, n_mix, ff_chunk):
    x_ref, mod_ref, g_ref = refs[0:3]
    mix_refs = refs[3:3 + n_mix]
    wo_refs = refs[3 + n_mix:3 + 2 * n_mix]
    w1_ref, w2_ref, o_ref = refs[3 + 2 * n_mix:]
    mod = mod_ref[0]
    y = None
    for a_ref, w_ref in zip(mix_refs, wo_refs):
        part = _dot(a_ref[0], w_ref[...])
        y = part if y is None else y + part
    x1 = x_ref[0] + mod[2:3] * y
    h = _modulated_norm(x1, g_ref[...], mod[3:4], mod[4:5]).astype(BF16)
    d_ff = w1_ref.shape[1]
    acc = None
    for c in range(d_ff // ff_chunk):
        a = _dot(h, w1_ref[:, c * ff_chunk:(c + 1) * ff_chunk])
        a = jnp.square(jnp.maximum(a, 0.0)).astype(BF16)
        part = _dot(a, w2_ref[c * ff_chunk:(c + 1) * ff_chunk, :])
        acc = part if acc is None else acc + part
    o_ref[0] = x1 + mod[5:6] * acc


def _out_mlp(x, mod, g, mixes, w_outs, w1, w2):
    batch, t, d = x.shape
    tm = _row_tile(t, MLP_ROWS)
    mod_map = (lambda b, i: (b, 0, 0)) if mod.shape[0] == batch else (lambda b, i: (0, 0, 0))
    const = lambda b, i: (0, 0)
    resident = lambda a: pl.BlockSpec(a.shape, const, pipeline_mode=pl.Buffered(1))
    row = lambda width: pl.BlockSpec((1, tm, width), lambda b, i: (b, i, 0))
    return pl.pallas_call(
        functools.partial(_out_mlp_kernel, n_mix=len(mixes), ff_chunk=1024),
        grid=(batch, t // tm),
        in_specs=([row(d), pl.BlockSpec((1, MOD_ROWS, d), mod_map), pl.BlockSpec((1, d), const)]
                  + [row(a.shape[2]) for a in mixes] + [resident(w) for w in w_outs]
                  + [resident(w1), resident(w2)]),
        out_specs=row(d),
        out_shape=jax.ShapeDtypeStruct((batch, t, d), F32),
        compiler_params=_params("arbitrary", "arbitrary"),
        name="out_mlp",
    )(x, mod, g, *mixes, *w_outs, w1, w2)


def kernel(x, c, ctx, c_ctx, ada_w, ada_b, norm1_g, norm2_g, ab_w_in, ab_gate_b, mlstm_norm_g, swa_q_norm_g,
           swa_k_norm_g, swa_sink, ab_w_out, c_w_in, c_q_norm_g, c_k_norm_g, c_w_out, mlp_w1, mlp_w2):
    depth = ada_w.shape[0]
    batch, _, d = x.shape
    mods = _mods(c, c_ctx, ada_w, ada_b)
    for layer in range(depth):
        last = layer == depth - 1
        mod_l = mods[layer, :batch]
        mod_c = mods[layer, batch:batch + 1]
        g1 = norm1_g[layer].reshape(1, d)
        g2 = norm2_g[layer].reshape(1, d)
        w1 = mlp_w1[layer].astype(BF16)
        w2 = mlp_w2[layer].astype(BF16)
        j = layer // 2
        if layer % 2 == 0:
            w, wt, gb = _ab_weights(ab_w_in[j], ab_gate_b[j])
            qn = jnp.tile(swa_q_norm_g[j], LANES // B_HEAD_DIM).reshape(1, LANES)
            kn = jnp.tile(swa_k_norm_g[j], LANES // B_HEAD_DIM).reshape(1, LANES)
            qa_l, kat_l, va_l, oa_l, qb_l, kb_l, vb_l, gr_l = _proj_ab(x, mod_l, g1, w, wt, gb, qn, kn, True)
            qa_c, kat_c, va_c, oa_c, qb_c, kb_c, vb_c, gr_c = _proj_ab(ctx, mod_c, g1, w, wt, gb, qn, kn, False)
            ha_l, ha_c = _mlstm(qa_l, kat_l, va_l, oa_l, gr_l, qa_c, kat_c, va_c, oa_c, gr_c, mlstm_norm_g[j])
            ob_l = _swa(swa_sink[j], qb_l, kb_l, vb_l, kb_c, vb_c, True)
            w_out = ab_w_out[j].astype(BF16)
            w_outs = [w_out[:A_WIDTH], w_out[A_WIDTH:]]
            x = _out_mlp(x, mod_l, g2, [ha_l, ob_l], w_outs, w1, w2)
            if not last:
                ob_c = _swa(swa_sink[j], qb_c, kb_c, vb_c, kb_c, vb_c, False)
                ctx = _out_mlp(ctx, mod_c, g2, [ha_c, ob_c], w_outs, w1, w2)
        else:
            w = c_w_in[j].astype(BF16)
            qn = c_q_norm_g[j].reshape(1, LANES)
            kn = c_k_norm_g[j].reshape(1, LANES)
            q_l, k_l, v_l = _proj_c(x, mod_l, g1, w, qn, kn, True)
            q_c, k_c, v_c = _proj_c(ctx, mod_c, g1, w, qn, kn, False)
            k_all = jnp.concatenate([k_c, k_l], axis=1)
            v_all = jnp.concatenate([v_c, v_l], axis=1)
            w_out = c_w_out[j].astype(BF16)
            o_l = _flash(q_l, k_all, v_all, k_c.shape[1])
            x = _out_mlp(x, mod_l, g2, [o_l], [w_out], w1, w2)
            if not last:
                o_c = _flash(q_c, k_c, v_c, 0)
                ctx = _out_mlp(ctx, mod_c, g2, [o_c], [w_out], w1, w2)
    return x
```

```python
import functools

import jax
import jax.numpy as jnp
from jax import lax
from jax.experimental import pallas as pl
from jax.experimental.pallas import tpu as pltpu

F32 = jnp.float32
BF16 = jnp.bfloat16

GRID_W = 64
BLOCK = 128
WINDOW = 128
ROPE_BASE = 10000.0
EPS = 1e-6
LOG2_E = 1.4426950408889634
PROJ_ROWS = 512
MLP_ROWS = 512
SWA_BLOCKS = 4
FLASH_KEYS = 512
FLASH_UNROLL = 4
FLASH_PARTS = 1
FLASH_ROWS = 128
N_MOD = 6
MOD_ROWS = 8
LANES = 128
MXU_WIDTH = 256

A_HEADS = 4
A_HEAD_DIM = 128
A_WIDTH = A_HEADS * A_HEAD_DIM
A_GATES = 4 * A_HEADS
GATE_ROWS = 8
B_HEADS = 8
B_KV_HEADS = 2
B_HEAD_DIM = 64
B_WIDTH = B_HEADS * B_HEAD_DIM
B_KV_WIDTH = B_KV_HEADS * B_HEAD_DIM
C_HEADS = 8
C_KV_HEADS = 2
C_HEAD_DIM = 128
C_WIDTH = C_HEADS * C_HEAD_DIM
C_KV_WIDTH = C_KV_HEADS * C_HEAD_DIM

VMEM_LIMIT = 56 * 1024 * 1024

NT_DIMS = (((1,), (1,)), ((), ()))
TN_DIMS = (((0,), (0,)), ((), ()))


def _dot(a, b):
    return jnp.dot(a, b, preferred_element_type=F32)


def _dot_nt(a, b):
    return lax.dot_general(a, b, NT_DIMS, preferred_element_type=F32)


def _dot_tn(a, b):
    return lax.dot_general(a, b, TN_DIMS, preferred_element_type=F32)


def _params(*sem):
    return pltpu.CompilerParams(dimension_semantics=sem, vmem_limit_bytes=VMEM_LIMIT)


def _mods_kernel(c_ref, w_ref, b_ref, o_ref):
    cf = c_ref[...]
    s = (cf * jax.nn.sigmoid(cf)).astype(BF16)
    o_ref[0] = _dot(s, w_ref[0].astype(BF16)) + b_ref[0]


def _mods(c, c_ctx, ada_w, ada_b):
    depth, d, _ = ada_w.shape
    batch = c.shape[0]
    rows = -(-(batch + 1) // 8) * 8
    cc = jnp.zeros((rows, d), F32).at[:batch].set(c).at[batch].set(c_ctx)
    out = pl.pallas_call(
        _mods_kernel,
        grid=(depth, N_MOD),
        in_specs=[
            pl.BlockSpec((rows, d), lambda l, j: (0, 0)),
            pl.BlockSpec((1, d, d), lambda l, j: (l, 0, j)),
            pl.BlockSpec((1, 1, d), lambda l, j: (l, 0, j)),
        ],
        out_specs=pl.BlockSpec((1, rows, d), lambda l, j: (l, 0, j)),
        out_shape=jax.ShapeDtypeStruct((depth, rows, N_MOD * d), F32),
        compiler_params=_params("arbitrary", "arbitrary"),
        name="ada_mods",
    )(cc, ada_w, ada_b.reshape(depth, 1, N_MOD * d))
    out = out.reshape(depth, rows, N_MOD, d)
    return jnp.pad(out, ((0, 0), (0, 0), (0, MOD_ROWS - N_MOD), (0, 0)))


def _modulated_norm(x, g, shift, scale):
    ms = jnp.mean(x * x, axis=-1, keepdims=True)
    return (x * lax.rsqrt(ms + EPS) * g) * (1.0 + scale) + shift


def _split3(a):
    hi = a.astype(BF16)
    r = a - hi.astype(F32)
    mid = r.astype(BF16)
    lo = (r - mid.astype(F32)).astype(BF16)
    return hi, mid, lo


def _head_norm_rope(xh, gain, cos, sin, group, out_scale):
    sq = xh * xh
    half = group // 2
    if group == LANES:
        ssq = jnp.sum(sq, axis=-1, keepdims=True)
    else:
        assert 2 * group == LANES
        lane = lax.broadcasted_iota(jnp.int32, xh.shape, 1)
        left = lane < group
        ssq = jnp.where(left, jnp.sum(jnp.where(left, sq, 0.0), axis=-1, keepdims=True),
                        jnp.sum(jnp.where(left, 0.0, sq), axis=-1, keepdims=True))
    xn = xh * lax.rsqrt(ssq * (1.0 / group) + EPS) * gain
    if group == LANES:
        rot = pltpu.roll(xn, half, 1)
    else:
        first = (lane & (group - 1)) < half
        rot = jnp.where(first, pltpu.roll(xn, LANES - half, 1), pltpu.roll(xn, half, 1))
    y = xn * cos + rot * sin
    if out_scale != 1.0:
        y = y * out_scale
    return y


def _rope_tables(n_tokens, head_dim, use_rope):
    reps = LANES // head_dim
    if not use_rope:
        return jnp.ones((n_tokens, LANES), F32), jnp.zeros((n_tokens, LANES), F32)
    rows = n_tokens // GRID_W
    row = jnp.broadcast_to(jnp.arange(rows)[:, None], (rows, GRID_W)).reshape(n_tokens).astype(F32)
    col = jnp.broadcast_to(jnp.arange(GRID_W)[None, :], (rows, GRID_W)).reshape(n_tokens).astype(F32)
    pairs = head_dim // 4
    inv_freq = ROPE_BASE ** (-jnp.arange(pairs, dtype=F32) / pairs)
    ang = jnp.concatenate([row[:, None] * inv_freq, col[:, None] * inv_freq], axis=-1)
    cos, sin = jnp.cos(ang), jnp.sin(ang)
    cos_full = jnp.concatenate([cos, cos], axis=-1)
    sin_signed = jnp.concatenate([-sin, sin], axis=-1)
    return jnp.tile(cos_full, (1, reps)), jnp.tile(sin_signed, (1, reps))


def _row_tile(t, target):
    tm = min(t, target)
    assert t % tm == 0
    return tm


def _proj_ab_kernel(x_ref, mod_ref, g_ref, w_ref, wt_ref, gb_ref, cos_ref, sin_ref, qn_ref, kn_ref,
                    qa_ref, kat_ref, va_ref, oa_ref, qb_ref, kb_ref, vb_ref, gr_ref):
    mod = mod_ref[0]
    h = _modulated_norm(x_ref[0], g_ref[...], mod[0:1], mod[1:2]).astype(BF16)

    def mm(lo, n):
        return _dot(h, w_ref[:, lo:lo + n])

    qa_ref[0] = mm(0, A_WIDTH).astype(BF16)
    va_ref[0] = mm(A_WIDTH, A_WIDTH).astype(BF16)
    oa_ref[0] = mm(2 * A_WIDTH, A_WIDTH)
    cos, sin = cos_ref[...], sin_ref[...]
    base = 3 * A_WIDTH
    for j in range(B_WIDTH // LANES):
        y = _head_norm_rope(mm(base + LANES * j, LANES), qn_ref[...], cos, sin, B_HEAD_DIM,
                            B_HEAD_DIM ** -0.5 * LOG2_E)
        qb_ref[0, :, LANES * j:LANES * (j + 1)] = y.astype(BF16)
    base += B_WIDTH
    for j in range(B_KV_HEADS):
        y = _head_norm_rope(mm(base + LANES * j, LANES), kn_ref[...], cos, sin, B_HEAD_DIM, 1.0)
        kb_ref[0, :, LANES * j:LANES * (j + 1)] = y.astype(BF16)
    base += B_KV_HEADS * LANES
    vb_ref[0] = mm(base, B_KV_HEADS * LANES).astype(BF16)
    tr = _dot_nt(wt_ref[...], h)
    kat_ref[0] = (tr[0:A_WIDTH] * A_HEAD_DIM ** -0.5).astype(BF16)
    gt = tr[A_WIDTH:] + gb_ref[...]
    typ = lax.broadcasted_iota(jnp.int32, gt.shape, 0) & (GATE_ROWS - 1)
    log_sig = jnp.minimum(gt, 0.0) - jnp.log1p(jnp.exp(-jnp.abs(gt)))
    gr_ref[0] = jnp.where((typ == 1) | (typ == 3), log_sig, gt)


def _proj_ab(x, mod, g, w, wg, gb, qn, kn, use_rope):
    batch, t, d = x.shape
    tm = _row_tile(t, PROJ_ROWS)
    cos, sin = _rope_tables(t, B_HEAD_DIM, use_rope)
    mod_map = (lambda b, i: (b, 0, 0)) if mod.shape[0] == batch else (lambda b, i: (0, 0, 0))
    const = lambda b, i: (0, 0)
    row = lambda width: pl.BlockSpec((1, tm, width), lambda b, i: (b, i, 0))
    kvw = B_KV_HEADS * LANES
    out_shape = [
        jax.ShapeDtypeStruct((batch, t, A_WIDTH), BF16),
        jax.ShapeDtypeStruct((batch, A_WIDTH, t), BF16),
        jax.ShapeDtypeStruct((batch, t, A_WIDTH), BF16),
        jax.ShapeDtypeStruct((batch, t, A_WIDTH), F32),
        jax.ShapeDtypeStruct((batch, t, B_WIDTH), BF16),
        jax.ShapeDtypeStruct((batch, t, kvw), BF16),
        jax.ShapeDtypeStruct((batch, t, kvw), BF16),
        jax.ShapeDtypeStruct((batch, A_HEADS * GATE_ROWS, t), F32),
    ]
    col = lambda height: pl.BlockSpec((1, height, tm), lambda b, i: (b, 0, i))
    out_specs = [row(A_WIDTH), col(A_WIDTH), row(A_WIDTH), row(A_WIDTH), row(B_WIDTH), row(kvw), row(kvw),
                 col(A_HEADS * GATE_ROWS)]
    return pl.pallas_call(
        _proj_ab_kernel,
        grid=(batch, t // tm),
        in_specs=[
            row(d),
            pl.BlockSpec((1, MOD_ROWS, d), mod_map),
            pl.BlockSpec((1, d), const),
            pl.BlockSpec(w.shape, const),
            pl.BlockSpec(wg.shape, const),
            pl.BlockSpec(gb.shape, const),
            pl.BlockSpec((tm, LANES), lambda b, i: (i, 0)),
            pl.BlockSpec((tm, LANES), lambda b, i: (i, 0)),
            pl.BlockSpec((1, LANES), const),
            pl.BlockSpec((1, LANES), const),
        ],
        out_specs=out_specs,
        out_shape=out_shape,
        compiler_params=_params("arbitrary", "arbitrary"),
        name="proj_ab",
    )(x, mod, g, w, wg, gb, cos, sin, qn, kn)


def _ab_weights(w_in, gate_b):
    bounds = [A_WIDTH * 4, A_WIDTH * 4 + A_GATES, A_WIDTH * 4 + A_GATES + B_WIDTH,
              A_WIDTH * 4 + A_GATES + B_WIDTH + B_KV_WIDTH]
    wa, wgate, wqb, wkb, wvb = jnp.split(w_in, bounds, axis=1)
    wqa, wka, wva_oa = wa[:, :A_WIDTH], wa[:, A_WIDTH:2 * A_WIDTH], wa[:, 2 * A_WIDTH:]

    def dup(wk):
        parts = []
        for hh in range(B_KV_HEADS):
            blk = wk[:, hh * B_HEAD_DIM:(hh + 1) * B_HEAD_DIM]
            parts += [blk] * (LANES // B_HEAD_DIM)
        return jnp.concatenate(parts, axis=1)

    w = jnp.concatenate([wqa, wva_oa, wqb, dup(wkb), dup(wvb)], axis=1).astype(BF16)
    d = w_in.shape[0]
    wg = wgate.reshape(d, 4, A_HEADS).transpose(2, 1, 0)
    wg = jnp.pad(wg, ((0, 0), (0, GATE_ROWS - 4), (0, 0))).reshape(A_HEADS * GATE_ROWS, d)
    wt = jnp.concatenate([wka.T, wg], axis=0).astype(BF16)
    gb = gate_b.reshape(4, A_HEADS).T
    gb = jnp.pad(gb, ((0, 0), (0, GATE_ROWS - 4))).reshape(A_HEADS * GATE_ROWS, 1).astype(F32)
    return w, wt, gb


def _proj_c_kernel(x_ref, mod_ref, g_ref, w_ref, cos_ref, sin_ref, qn_ref, kn_ref, q_ref, k_ref, v_ref):
    mod = mod_ref[0]
    h = _modulated_norm(x_ref[0], g_ref[...], mod[0:1], mod[1:2]).astype(BF16)

    def mm(lo, n):
        return _dot(h, w_ref[:, lo:lo + n])

    cos, sin = cos_ref[...], sin_ref[...]
    q_scale = C_HEAD_DIM ** -0.5 * LOG2_E
    for j in range(C_HEADS):
        y = _head_norm_rope(mm(LANES * j, LANES), qn_ref[...], cos, sin, C_HEAD_DIM, q_scale)
        q_ref[0, :, LANES * j:LANES * (j + 1)] = y.astype(BF16)
    for j in range(C_KV_HEADS):
        y = _head_norm_rope(mm(C_WIDTH + LANES * j, LANES), kn_ref[...], cos, sin, C_HEAD_DIM, 1.0)
        k_ref[0, :, LANES * j:LANES * (j + 1)] = y.astype(BF16)
    v_ref[0] = mm(C_WIDTH + C_KV_WIDTH, C_KV_WIDTH).astype(BF16)


def _proj_c(x, mod, g, w, qn, kn, use_rope):
    batch, t, d = x.shape
    tm = _row_tile(t, PROJ_ROWS)
    cos, sin = _rope_tables(t, C_HEAD_DIM, use_rope)
    mod_map = (lambda b, i: (b, 0, 0)) if mod.shape[0] == batch else (lambda b, i: (0, 0, 0))
    const = lambda b, i: (0, 0)
    row = lambda width: pl.BlockSpec((1, tm, width), lambda b, i: (b, i, 0))
    return pl.pallas_call(
        _proj_c_kernel,
        grid=(batch, t // tm),
        in_specs=[
            row(d),
            pl.BlockSpec((1, MOD_ROWS, d), mod_map),
            pl.BlockSpec((1, d), const),
            pl.BlockSpec(w.shape, const),
            pl.BlockSpec((tm, LANES), lambda b, i: (i, 0)),
            pl.BlockSpec((tm, LANES), lambda b, i: (i, 0)),
            pl.BlockSpec((1, LANES), const),
            pl.BlockSpec((1, LANES), const),
        ],
        out_specs=[row(C_WIDTH), row(C_KV_WIDTH), row(C_KV_WIDTH)],
        out_shape=[jax.ShapeDtypeStruct((batch, t, C_WIDTH), BF16),
                   jax.ShapeDtypeStruct((batch, t, C_KV_WIDTH), BF16),
                   jax.ShapeDtypeStruct((batch, t, C_KV_WIDTH), BF16)],
        compiler_params=_params("arbitrary", "arbitrary"),
        name="proj_c",
    )(x, mod, g, w, cos, sin, qn, kn)


def _mlstm_kernel(ql_ref, ktl_ref, vl_ref, oal_ref, gl_ref, qc_ref, ktc_ref, vc_ref, oac_ref, gc_ref, ng_ref,
                  hl_ref, hc_ref,
                  qs, kts, vs, gs, rowq, rowg, stats, cst, mst, cstate, *, t_ctx, t_lat):
    L = BLOCK
    ncc = t_ctx // L
    ncl = t_lat // L
    nc = ncc + ncl

    qs[0:t_lat] = ql_ref[0]
    qs[t_lat:] = qc_ref[0]
    kts[:, 0:t_lat] = ktl_ref[0]
    kts[:, t_lat:] = ktc_ref[0]
    vs[0:t_lat] = vl_ref[0]
    vs[t_lat:] = vc_ref[0]
    for r in range(4):
        gs[r, 0:ncl] = gl_ref[0, r]
        gs[r, ncl:nc] = gc_ref[0, r]

    ri = lax.broadcasted_iota(jnp.int32, (L, L), 0)
    ci = lax.broadcasted_iota(jnp.int32, (L, L), 1)
    lower = ci <= ri
    upper = ci >= ri
    ones_blk = jnp.ones((L, L), BF16)

    def chunk(i):
        return pl.ds(pl.multiple_of(i * L, L), L)

    def cum(x, mat):
        hi, mid, lo = _split3(x)
        return _dot(hi, mat) + _dot(mid, mat) + _dot(lo, mat)

    li_f, lf_f, li_b, lf_b = (gs[r, 0:nc] for r in range(4))
    bcum_f = cum(lf_f, jnp.where(upper, 1.0, 0.0).astype(BF16))
    bsuf_b = cum(lf_b, jnp.where(lower, 1.0, 0.0).astype(BF16))
    bl_f = bcum_f[:, L - 1:L]
    bl_b = bsuf_b[:, 0:1]
    wl_f = bl_f - bcum_f + li_f
    wl_b = bl_b - bsuf_b + li_b
    mx_f = jnp.max(wl_f, axis=1, keepdims=True)
    mx_b = jnp.max(wl_b, axis=1, keepdims=True)
    rb_f = li_f - bcum_f
    rb_b = li_b - bsuf_b
    lane = lax.broadcasted_iota(jnp.int32, (nc, L), 1)
    pm_f, pm_b = rb_f, rb_b
    step = 1
    while step < L:
        pm_f = jnp.where(lane >= step, jnp.maximum(pm_f, pltpu.roll(pm_f, step, 1)), pm_f)
        pm_b = jnp.where(lane < L - step, jnp.maximum(pm_b, pltpu.roll(pm_b, L - step, 1)), pm_b)
        step *= 2
    for n, val in enumerate((bcum_f, bsuf_b, rb_f, rb_b, jnp.exp(wl_f - mx_f), jnp.exp(wl_b - mx_b), pm_f, pm_b)):
        rowq[n, 0:nc] = val
    for n, val in enumerate((bl_f, bl_b, mx_f, mx_b)):
        stats[n, 0:nc] = jnp.broadcast_to(val, (nc, L))

    def prep(i, _):
        kt = kts[:, chunk(i)].astype(F32)
        vaug = jnp.concatenate([vs[chunk(i), :], ones_blk], axis=1)
        for d in range(2):
            cst[d, i] = _dot((kt * rowq[4 + d, pl.ds(i, 1), :]).astype(BF16), vaug)
        return 0

    lax.fori_loop(0, nc, prep, 0, unroll=2)

    cstate[...] = jnp.zeros_like(cstate)

    def scan_step(j, carry):
        new = []
        for d in range(2):
            m = carry[d]
            if d == 0:
                i = jnp.where(j < ncc, ncl + j, j - ncc)
            else:
                i = nc - 1 - j
            c_old = cstate[d]
            kv = cst[d, i]
            cst[d, i] = c_old
            mst[d, pl.ds(i, 1), :] = jnp.broadcast_to(m, (1, L))
            bl = stats[d, pl.ds(i, 1), 0:1]
            mx = stats[2 + d, pl.ds(i, 1), 0:1]
            m_new = jnp.maximum(bl + m, mx)
            cstate[d] = jnp.exp(bl + m - m_new) * c_old + jnp.exp(mx - m_new) * kv
            new.append(m_new)
        return tuple(new)

    m0 = jnp.zeros((1, 1), F32)
    lax.fori_loop(0, nc, scan_step, (m0, m0))

    for d in range(2):
        g = jnp.maximum(mst[d, 0:nc], rowq[6 + d, 0:nc])
        rowg[d, 0:nc] = g
        rowg[2 + d, 0:nc] = jnp.exp(-rowq[d, 0:nc] - g)

    ng = ng_ref[...]

    def emit(i, oa, out_ref, out_rows):
        q = qs[chunk(i), :]
        vaug = jnp.concatenate([vs[chunk(i), :], ones_blk], axis=1)
        qk = _dot(q, kts[:, chunk(i)])
        r8 = jnp.concatenate([rowg[n, pl.ds(i, 1), :] for n in range(4)] + [jnp.zeros((4, L), F32)], axis=0)
        colblk = jnp.concatenate([r8, jnp.zeros((L - 8, L), F32)], axis=0).T
        h = None
        for d in range(2):
            g = jnp.broadcast_to(colblk[:, d:d + 1], (L, L))
            floor = colblk[:, 2 + d:3 + d]
            rb = rowq[2 + d, pl.ds(i, 1), :]
            mask = lower if d == 0 else upper
            dm = jnp.where(mask, jnp.exp(rb - g), 0.0)
            s = (qk * dm).astype(BF16)
            iw = jnp.exp(mst[d, pl.ds(i, 1), 0:1] - g)
            nd = jnp.concatenate([iw, iw], axis=1) * _dot(q, cst[d, i].astype(BF16)) + _dot(s, vaug)
            hd = nd[:, 0:L] / jnp.maximum(jnp.abs(nd[:, L:2 * L]), floor)
            h = hd if h is None else h + hd
        ms = jnp.mean(h * h, axis=-1, keepdims=True)
        y = (h * lax.rsqrt(ms + EPS) * ng) * jax.nn.sigmoid(oa)
        out_ref[0, out_rows, :] = y.astype(BF16)

    def emit_ctx(i, _):
        emit(i + ncl, oac_ref[0, chunk(i), :], hc_ref, chunk(i))
        return 0

    def emit_lat(i, _):
        emit(i, oal_ref[0, chunk(i), :], hl_ref, chunk(i))
        return 0

    lax.fori_loop(0, ncc, emit_ctx, 0, unroll=2)
    lax.fori_loop(0, ncl, emit_lat, 0, unroll=4)


def _mlstm(qa_l, kat_l, va_l, oa_l, gr_l, qa_c, kat_c, va_c, oa_c, gr_c, norm_g):
    batch, t_lat, _ = qa_l.shape
    t_ctx = qa_c.shape[1]
    t_all = t_ctx + t_lat
    nc = t_all // BLOCK
    nc_pad = -(-nc // 8) * 8
    head = lambda t: pl.BlockSpec((1, t, A_HEAD_DIM), lambda b, h: (b, 0, h))
    head_t = lambda t: pl.BlockSpec((1, A_HEAD_DIM, t), lambda b, h: (b, h, 0))
    gates = lambda t: pl.BlockSpec((1, GATE_ROWS, t // BLOCK, BLOCK), lambda b, h: (b, h, 0, 0))
    chunked = lambda g: g.reshape(batch, A_HEADS * GATE_ROWS, g.shape[2] // BLOCK, BLOCK)
    return pl.pallas_call(
        functools.partial(_mlstm_kernel, t_ctx=t_ctx, t_lat=t_lat),
        grid=(batch, A_HEADS),
        in_specs=[head(t_lat), head_t(t_lat), head(t_lat), head(t_lat), gates(t_lat),
                  head(t_ctx), head_t(t_ctx), head(t_ctx), head(t_ctx), gates(t_ctx),
                  pl.BlockSpec((1, A_HEAD_DIM), lambda b, h: (0, h))],
        out_specs=[head(t_lat), head(t_ctx)],
        out_shape=[jax.ShapeDtypeStruct((batch, t_lat, A_WIDTH), BF16),
                   jax.ShapeDtypeStruct((batch, t_ctx, A_WIDTH), BF16)],
        scratch_shapes=[
            pltpu.VMEM((t_all, A_HEAD_DIM), BF16),
            pltpu.VMEM((A_HEAD_DIM, t_all), BF16),
            pltpu.VMEM((t_all, A_HEAD_DIM), BF16),
            pltpu.VMEM((4, nc_pad, BLOCK), F32),
            pltpu.VMEM((8, nc_pad, BLOCK), F32),
            pltpu.VMEM((4, nc_pad, BLOCK), F32),
            pltpu.VMEM((4, nc_pad, LANES), F32),
            pltpu.VMEM((2, nc, A_HEAD_DIM, 2 * A_HEAD_DIM), F32),
            pltpu.VMEM((2, nc_pad, LANES), F32),
            pltpu.VMEM((2, A_HEAD_DIM, 2 * A_HEAD_DIM), F32),
        ],
        compiler_params=_params("arbitrary", "arbitrary"),
        name="mlstm",
    )(qa_l, kat_l, va_l, oa_l, chunked(gr_l), qa_c, kat_c, va_c, oa_c, chunked(gr_c),
      norm_g.reshape(1, A_WIDTH))


def _swa_kernel(sink_ref, q_ref, *refs, t_lat, has_window, blocks):
    if has_window:
        bias_ref, k_ref, v_ref, kx_ref, vx_ref, o_ref, s_scr = refs
    else:
        kx_ref, vx_ref, o_ref, s_scr = refs
    L = BLOCK
    nb = t_lat // L
    group = B_HEADS // B_KV_HEADS
    lane = lax.broadcasted_iota(jnp.int32, (L, LANES), 1)
    lo = lane < B_HEAD_DIM
    zero = jnp.zeros((L, LANES), BF16)
    def window(u):
        qblk = pl.program_id(1) * blocks + u
        start = pl.multiple_of(jnp.clip((qblk - 1) * L, 0, t_lat - 3 * L), L)
        return qblk, pl.ds(start, 3 * L)

    def logits(c, u, kvh):
        sl = slice(kvh * LANES, (kvh + 1) * LANES)
        q = q_ref[0, u * L:(u + 1) * L, :]
        qa = q[:, (2 * kvh) * LANES:(2 * kvh + 1) * LANES]
        qb = q[:, (2 * kvh + 1) * LANES:(2 * kvh + 2) * LANES]
        q4 = jnp.concatenate([jnp.where(lo, qa, zero), jnp.where(lo, zero, qa),
                              jnp.where(lo, qb, zero), jnp.where(lo, zero, qb)], axis=0)
        sink = jnp.concatenate([jnp.full((L, 1), sink_ref[kvh * group + g] * LOG2_E, F32)
                                for g in range(group)], axis=0)
        if has_window:
            qblk, win = window(u)
            bias = bias_ref[jnp.where(qblk == 0, 0, jnp.where(qblk == nb - 1, 2, 1))]
            keys = jnp.concatenate([k_ref[0, win, sl], kx_ref[0, :, sl]], axis=0)
            s = _dot_nt(q4, keys)
            s = (s.reshape(group, L, s.shape[1]) + bias[None]).reshape(s.shape)
        else:
            s = _dot_nt(q4, kx_ref[0, :, sl])
        s_scr[c] = s
        return jnp.maximum(jnp.max(s, axis=1, keepdims=True), sink), sink

    def attend(c, u, kvh, m, sink):
        sl = slice(kvh * LANES, (kvh + 1) * LANES)
        if has_window:
            vals = jnp.concatenate([v_ref[0, window(u)[1], sl], vx_ref[0, :, sl]], axis=0)
        else:
            vals = vx_ref[0, :, sl]
        e = jnp.exp2(s_scr[c] - m)
        denom = jnp.sum(e, axis=1, keepdims=True) + jnp.exp2(sink - m)
        o4 = _dot(e.astype(BF16), vals) / denom
        rows = slice(u * L, (u + 1) * L)
        o_ref[0, rows, (2 * kvh) * LANES:(2 * kvh + 1) * LANES] = (
            jnp.where(lo, o4[0:L], o4[L:2 * L]).astype(BF16))
        o_ref[0, rows, (2 * kvh + 1) * LANES:(2 * kvh + 2) * LANES] = (
            jnp.where(lo, o4[2 * L:3 * L], o4[3 * L:4 * L]).astype(BF16))

    chains = [(u, kvh) for u in range(blocks) for kvh in range(B_KV_HEADS)]
    stats = [logits(0, *chains[0])]
    for c, (u, kvh) in enumerate(chains):
        if c + 1 < len(chains):
            stats.append(logits(c + 1, *chains[c + 1]))
        attend(c, u, kvh, *stats[c])


def _band_bias(t_ctx):
    L = BLOCK
    t = jnp.arange(L)[:, None]
    j = jnp.arange(3 * L)[None, :]
    tables = []
    for rel in (0, -L, -2 * L):
        ok = jnp.abs(j - t + rel) <= WINDOW
        tables.append(jnp.concatenate([jnp.where(ok, 0.0, -jnp.inf), jnp.zeros((L, t_ctx))], axis=1))
    return jnp.stack(tables).astype(F32)


def _swa(sink, q, k, v, k_ctx, v_ctx, has_window):
    batch, t, _ = q.shape
    t_ctx = k_ctx.shape[1]
    nb = t // BLOCK
    blocks = min(SWA_BLOCKS, nb)
    assert nb % blocks == 0 and (nb >= 3 or not has_window)
    kvw = B_KV_HEADS * LANES
    cur = lambda b, i: (b, i, 0)
    whole = lambda b, i: (b, 0, 0)
    ctx_spec = pl.BlockSpec((1, t_ctx, kvw), whole)
    in_specs = [pl.BlockSpec(memory_space=pltpu.SMEM), pl.BlockSpec((1, blocks * BLOCK, B_WIDTH), cur)]
    args = [sink, q]
    if has_window:
        bias = _band_bias(t_ctx)
        in_specs += [pl.BlockSpec(bias.shape, lambda b, i: (0, 0, 0)),
                     pl.BlockSpec((1, t, kvw), whole), pl.BlockSpec((1, t, kvw), whole)]
        args += [bias, k, v]
    in_specs += [ctx_spec, ctx_spec]
    args += [k_ctx, v_ctx]
    return pl.pallas_call(
        functools.partial(_swa_kernel, t_lat=t, has_window=has_window, blocks=blocks),
        grid=(batch, nb // blocks),
        in_specs=in_specs,
        out_specs=pl.BlockSpec((1, blocks * BLOCK, B_WIDTH), cur),
        scratch_shapes=[pltpu.VMEM((blocks * B_KV_HEADS, (B_HEADS // B_KV_HEADS) * BLOCK,
                                    (3 * BLOCK if has_window else 0) + t_ctx), F32)],
        out_shape=jax.ShapeDtypeStruct((batch, t, B_WIDTH), BF16),
        compiler_params=_params("arbitrary", "arbitrary"),
        name="swa" if has_window else "swa_ctx",
    )(*args)


def _flash_kernel(q_ref, k_ref, v_ref, o_ref, s0_scr, s1_scr, m0_scr, m1_scr, *, chunks, tq, unroll):
    group = C_HEADS // C_KV_HEADS
    nq = q_ref.shape[1] // tq

    def q_rows(t):
        return pl.ds(pl.multiple_of(t * tq, tq), tq)

    def load_q(t):
        q = q_ref[0, q_rows(t), :]
        return jnp.concatenate([q[:, g * LANES:(g + 1) * LANES] for g in range(group)], axis=0)

    def stage_a(q4, s_ref, m, start, size):
        s = _dot_nt(q4, k_ref[0, start:start + size, :])
        s_ref[:, start:start + size] = s
        cm = jnp.max(s, axis=1, keepdims=True)
        return cm if m is None else jnp.maximum(m, cm)

    def stage_b(s_ref, m, l, acc, start, size):
        p = jnp.exp2(s_ref[:, start:start + size] - m)
        ps = jnp.sum(p, axis=1, keepdims=True)
        pv = _dot(p.astype(BF16), v_ref[0, start:start + size, :])
        return (ps, pv) if l is None else (l + ps, acc + pv)

    def step(t, cur, nxt):
        q4 = load_q(jnp.minimum(t + 1, nq - 1))
        m_cur = cur[1][...]
        m_next, l, acc = None, None, None
        for start, size in chunks:
            m_next = stage_a(q4, nxt[0], m_next, start, size)
            l, acc = stage_b(cur[0], m_cur, l, acc, start, size)
        nxt[1][...] = m_next
        out = acc / l
        for g in range(group):
            o_ref[0, q_rows(t), g * LANES:(g + 1) * LANES] = out[g * tq:(g + 1) * tq].astype(BF16)

    bufs = ((s0_scr, m0_scr), (s1_scr, m1_scr))
    q4 = load_q(0)
    m = None
    for start, size in chunks:
        m = stage_a(q4, s0_scr, m, start, size)
    m0_scr[...] = m

    def body(u, _):
        for k in range(unroll):
            step(unroll * u + k, bufs[k % 2], bufs[(k + 1) % 2])
        return 0

    lax.fori_loop(0, nq // unroll, body, 0)


def _key_chunks(total, target):
    chunks = []
    start = 0
    while start < total:
        size = min(target, total - start)
        chunks.append((start, size))
        start += size
    return tuple(chunks)


def _flash(q, k, v, t_ctx):
    batch, t, _ = q.shape
    tk_all = k.shape[1]
    tq = _row_tile(t, FLASH_ROWS)
    parts = FLASH_PARTS if t % (FLASH_PARTS * 2 * tq) == 0 else 1
    tp = t // parts
    unroll = FLASH_UNROLL if (tp // tq) % FLASH_UNROLL == 0 else 2
    assert unroll % 2 == 0 and (tp // tq) % unroll == 0
    group = C_HEADS // C_KV_HEADS
    gw = group * C_HEAD_DIM
    chunks = (_key_chunks(t_ctx, FLASH_KEYS)
              + tuple((t_ctx + a, n) for a, n in _key_chunks(tk_all - t_ctx, FLASH_KEYS)))
    return pl.pallas_call(
        functools.partial(_flash_kernel, chunks=chunks, tq=tq, unroll=unroll),
        grid=(batch, C_KV_HEADS, parts),
        in_specs=[pl.BlockSpec((1, tp, gw), lambda b, h, p: (b, p, h)),
                  pl.BlockSpec((1, tk_all, C_HEAD_DIM), lambda b, h, p: (b, 0, h)),
                  pl.BlockSpec((1, tk_all, C_HEAD_DIM), lambda b, h, p: (b, 0, h))],
        out_specs=pl.BlockSpec((1, tp, gw), lambda b, h, p: (b, p, h)),
        out_shape=jax.ShapeDtypeStruct((batch, t, C_WIDTH), BF16),
        scratch_shapes=[pltpu.VMEM((group * tq, tk_all), F32), pltpu.VMEM((group * tq, tk_all), F32),
                        pltpu.VMEM((group * tq, 1), F32), pltpu.VMEM((group * tq, 1), F32)],
        compiler_params=_params("arbitrary", "arbitrary", "arbitrary"),
        name="flash_c",
    )(q, k, v)


def _out_mlp_kernel(*refs, n_mix, ff_chunk):
    x_ref, mod_ref, g_ref = refs[0:3]
    mix_refs = refs[3:3 + n_mix]
    wo_refs = refs[3 + n_mix:3 + 2 * n_mix]
    w1_ref, w2_ref, o_ref = refs[3 + 2 * n_mix:]
    mod = mod_ref[0]
    y = None
    for a_ref, w_ref in zip(mix_refs, wo_refs):
        part = _dot(a_ref[0], w_ref[...])
        y = part if y is None else y + part
    x1 = x_ref[0] + mod[2:3] * y
    h = _modulated_norm(x1, g_ref[...], mod[3:4], mod[4:5]).astype(BF16)
    d_ff = w1_ref.shape[1]
    acc = None
    for c in range(d_ff // ff_chunk):
        a = _dot(h, w1_ref[:, c * ff_chunk:(c + 1) * ff_chunk])
        a = jnp.square(jnp.maximum(a, 0.0)).astype(BF16)
        part = _dot(a, w2_ref[c * ff_chunk:(c + 1) * ff_chunk, :])
        acc = part if acc is None else acc + part
    o_ref[0] = x1 + mod[5:6] * acc


def _out_mlp(x, mod, g, mixes, w_outs, w1, w2):
    batch, t, d = x.shape
    tm = _row_tile(t, MLP_ROWS)
    mod_map = (lambda b, i: (b, 0, 0)) if mod.shape[0] == batch else (lambda b, i: (0, 0, 0))
    const = lambda b, i: (0, 0)
    resident = lambda a: pl.BlockSpec(a.shape, const, pipeline_mode=pl.Buffered(1))
    row = lambda width: pl.BlockSpec((1, tm, width), lambda b, i: (b, i, 0))
    return pl.pallas_call(
        functools.partial(_out_mlp_kernel, n_mix=len(mixes), ff_chunk=1024),
        grid=(batch, t // tm),
        in_specs=([row(d), pl.BlockSpec((1, MOD_ROWS, d), mod_map), pl.BlockSpec((1, d), const)]
                  + [row(a.shape[2]) for a in mixes] + [resident(w) for w in w_outs]
                  + [resident(w1), resident(w2)]),
        out_specs=row(d),
        out_shape=jax.ShapeDtypeStruct((batch, t, d), F32),
        compiler_params=_params("arbitrary", "arbitrary"),
        name="out_mlp",
    )(x, mod, g, *mixes, *w_outs, w1, w2)


def kernel(x, c, ctx, c_ctx, ada_w, ada_b, norm1_g, norm2_g, ab_w_in, ab_gate_b, mlstm_norm_g, swa_q_norm_g,
           swa_k_norm_g, swa_sink, ab_w_out, c_w_in, c_q_norm_g, c_k_norm_g, c_w_out, mlp_w1, mlp_w2):
    depth = ada_w.shape[0]
    batch, _, d = x.shape
    mods = _mods(c, c_ctx, ada_w, ada_b)
    for layer in range(depth):
        last = layer == depth - 1
        mod_l = mods[layer, :batch]
        mod_c = mods[layer, batch:batch + 1]
        g1 = norm1_g[layer].reshape(1, d)
        g2 = norm2_g[layer].reshape(1, d)
        w1 = mlp_w1[layer].astype(BF16)
        w2 = mlp_w2[layer].astype(BF16)
        j = layer // 2
        if layer % 2 == 0:
            w, wt, gb = _ab_weights(ab_w_in[j], ab_gate_b[j])
            qn = jnp.tile(swa_q_norm_g[j], LANES // B_HEAD_DIM).reshape(1, LANES)
            kn = jnp.tile(swa_k_norm_g[j], LANES // B_HEAD_DIM).reshape(1, LANES)
            qa_l, kat_l, va_l, oa_l, qb_l, kb_l, vb_l, gr_l = _proj_ab(x, mod_l, g1, w, wt, gb, qn, kn, True)
            qa_c, kat_c, va_c, oa_c, qb_c, kb_c, vb_c, gr_c = _proj_ab(ctx, mod_c, g1, w, wt, gb, qn, kn, False)
            ha_l, ha_c = _mlstm(qa_l, kat_l, va_l, oa_l, gr_l, qa_c, kat_c, va_c, oa_c, gr_c, mlstm_norm_g[j])
            ob_l = _swa(swa_sink[j], qb_l, kb_l, vb_l, kb_c, vb_c, True)
            w_out = ab_w_out[j].astype(BF16)
            w_outs = [w_out[:A_WIDTH], w_out[A_WIDTH:]]
            x = _out_mlp(x, mod_l, g2, [ha_l, ob_l], w_outs, w1, w2)
            if not last:
                ob_c = _swa(swa_sink[j], qb_c, kb_c, vb_c, kb_c, vb_c, False)
                ctx = _out_mlp(ctx, mod_c, g2, [ha_c, ob_c], w_outs, w1, w2)
        else:
            w = c_w_in[j].astype(BF16)
            qn = c_q_norm_g[j].reshape(1, LANES)
            kn = c_k_norm_g[j].reshape(1, LANES)
            q_l, k_l, v_l = _proj_c(x, mod_l, g1, w, qn, kn, True)
            q_c, k_c, v_c = _proj_c(ctx, mod_c, g1, w, qn, kn, False)
            k_all = jnp.concatenate([k_c, k_l], axis=1)
            v_all = jnp.concatenate([v_c, v_l], axis=1)
            w_out = c_w_out[j].astype(BF16)
            o_l = _flash(q_l, k_all, v_all, k_c.shape[1])
            x = _out_mlp(x, mod_l, g2, [o_l], [w_out], w1, w2)
            if not last:
                o_c = _flash(q_c, k_c, v_c, 0)
                ctx = _out_mlp(ctx, mod_c, g2, [o_c], [w_out], w1, w2)
    return x
```

```python
import functools

import jax
import jax.numpy as jnp
from jax import lax
from jax.experimental import pallas as pl
from jax.experimental.pallas import tpu as pltpu

F32 = jnp.float32
BF16 = jnp.bfloat16

GRID_W = 64
BLOCK = 128
WINDOW = 128
ROPE_BASE = 10000.0
EPS = 1e-6
LOG2_E = 1.4426950408889634
PROJ_ROWS = 512
MLP_ROWS = 512
SWA_BLOCKS = 4
FLASH_KEYS = 512
FLASH_UNROLL = 4
FLASH_PARTS = 1
FLASH_ROWS = 128
N_MOD = 6
MOD_ROWS = 8
LANES = 128
MXU_WIDTH = 256

A_HEADS = 4
A_HEAD_DIM = 128
A_WIDTH = A_HEADS * A_HEAD_DIM
A_GATES = 4 * A_HEADS
GATE_ROWS = 8
B_HEADS = 8
B_KV_HEADS = 2
B_HEAD_DIM = 64
B_WIDTH = B_HEADS * B_HEAD_DIM
B_KV_WIDTH = B_KV_HEADS * B_HEAD_DIM
C_HEADS = 8
C_KV_HEADS = 2
C_HEAD_DIM = 128
C_WIDTH = C_HEADS * C_HEAD_DIM
C_KV_WIDTH = C_KV_HEADS * C_HEAD_DIM

VMEM_LIMIT = 56 * 1024 * 1024

NT_DIMS = (((1,), (1,)), ((), ()))
TN_DIMS = (((0,), (0,)), ((), ()))


def _dot(a, b):
    return jnp.dot(a, b, preferred_element_type=F32)


def _dot_nt(a, b):
    return lax.dot_general(a, b, NT_DIMS, preferred_element_type=F32)


def _dot_tn(a, b):
    return lax.dot_general(a, b, TN_DIMS, preferred_element_type=F32)


def _params(*sem):
    return pltpu.CompilerParams(dimension_semantics=sem, vmem_limit_bytes=VMEM_LIMIT)


def _mods_kernel(c_ref, w_ref, b_ref, o_ref):
    cf = c_ref[...]
    s = (cf * jax.nn.sigmoid(cf)).astype(BF16)
    o_ref[0] = _dot(s, w_ref[0].astype(BF16)) + b_ref[0]


def _mods(c, c_ctx, ada_w, ada_b):
    depth, d, _ = ada_w.shape
    batch = c.shape[0]
    rows = -(-(batch + 1) // 8) * 8
    cc = jnp.zeros((rows, d), F32).at[:batch].set(c).at[batch].set(c_ctx)
    out = pl.pallas_call(
        _mods_kernel,
        grid=(depth, N_MOD),
        in_specs=[
            pl.BlockSpec((rows, d), lambda l, j: (0, 0)),
            pl.BlockSpec((1, d, d), lambda l, j: (l, 0, j)),
            pl.BlockSpec((1, 1, d), lambda l, j: (l, 0, j)),
        ],
        out_specs=pl.BlockSpec((1, rows, d), lambda l, j: (l, 0, j)),
        out_shape=jax.ShapeDtypeStruct((depth, rows, N_MOD * d), F32),
        compiler_params=_params("arbitrary", "arbitrary"),
        name="ada_mods",
    )(cc, ada_w, ada_b.reshape(depth, 1, N_MOD * d))
    out = out.reshape(depth, rows, N_MOD, d)
    return jnp.pad(out, ((0, 0), (0, 0), (0, MOD_ROWS - N_MOD), (0, 0)))


def _modulated_norm(x, g, shift, scale):
    ms = jnp.mean(x * x, axis=-1, keepdims=True)
    return (x * lax.rsqrt(ms + EPS) * g) * (1.0 + scale) + shift


def _split3(a):
    hi = a.astype(BF16)
    r = a - hi.astype(F32)
    mid = r.astype(BF16)
    lo = (r - mid.astype(F32)).astype(BF16)
    return hi, mid, lo


def _head_norm_rope(xh, gain, cos, sin, group, out_scale):
    sq = xh * xh
    half = group // 2
    if group == LANES:
        ssq = jnp.sum(sq, axis=-1, keepdims=True)
    else:
        assert 2 * group == LANES
        lane = lax.broadcasted_iota(jnp.int32, xh.shape, 1)
        left = lane < group
        ssq = jnp.where(left, jnp.sum(jnp.where(left, sq, 0.0), axis=-1, keepdims=True),
                        jnp.sum(jnp.where(left, 0.0, sq), axis=-1, keepdims=True))
    xn = xh * lax.rsqrt(ssq * (1.0 / group) + EPS) * gain
    if group == LANES:
        rot = pltpu.roll(xn, half, 1)
    else:
        first = (lane & (group - 1)) < half
        rot = jnp.where(first, pltpu.roll(xn, LANES - half, 1), pltpu.roll(xn, half, 1))
    y = xn * cos + rot * sin
    if out_scale != 1.0:
        y = y * out_scale
    return y


def _rope_tables(n_tokens, head_dim, use_rope):
    reps = LANES // head_dim
    if not use_rope:
        return jnp.ones((n_tokens, LANES), F32), jnp.zeros((n_tokens, LANES), F32)
    rows = n_tokens // GRID_W
    row = jnp.broadcast_to(jnp.arange(rows)[:, None], (rows, GRID_W)).reshape(n_tokens).astype(F32)
    col = jnp.broadcast_to(jnp.arange(GRID_W)[None, :], (rows, GRID_W)).reshape(n_tokens).astype(F32)
    pairs = head_dim // 4
    inv_freq = ROPE_BASE ** (-jnp.arange(pairs, dtype=F32) / pairs)
    ang = jnp.concatenate([row[:, None] * inv_freq, col[:, None] * inv_freq], axis=-1)
    cos, sin = jnp.cos(ang), jnp.sin(ang)
    cos_full = jnp.concatenate([cos, cos], axis=-1)
    sin_signed = jnp.concatenate([-sin, sin], axis=-1)
    return jnp.tile(cos_full, (1, reps)), jnp.tile(sin_signed, (1, reps))


def _row_tile(t, target):
    tm = min(t, target)
    assert t % tm == 0
    return tm


def _proj_ab_kernel(x_ref, mod_ref, g_ref, w_ref, wt_ref, gb_ref, cos_ref, sin_ref, qn_ref, kn_ref,
                    qa_ref, kat_ref, va_ref, oa_ref, qb_ref, kb_ref, vb_ref, gr_ref):
    mod = mod_ref[0]
    h = _modulated_norm(x_ref[0], g_ref[...], mod[0:1], mod[1:2]).astype(BF16)

    def mm(lo, n):
        return _dot(h, w_ref[:, lo:lo + n])

    qa_ref[0] = mm(0, A_WIDTH).astype(BF16)
    va_ref[0] = mm(A_WIDTH, A_WIDTH).astype(BF16)
    oa_ref[0] = mm(2 * A_WIDTH, A_WIDTH)
    cos, sin = cos_ref[...], sin_ref[...]
    base = 3 * A_WIDTH
    for j in range(B_WIDTH // LANES):
        y = _head_norm_rope(mm(base + LANES * j, LANES), qn_ref[...], cos, sin, B_HEAD_DIM,
                            B_HEAD_DIM ** -0.5 * LOG2_E)
        qb_ref[0, :, LANES * j:LANES * (j + 1)] = y.astype(BF16)
    base += B_WIDTH
    for j in range(B_KV_HEADS):
        y = _head_norm_rope(mm(base + LANES * j, LANES), kn_ref[...], cos, sin, B_HEAD_DIM, 1.0)
        kb_ref[0, :, LANES * j:LANES * (j + 1)] = y.astype(BF16)
    base += B_KV_HEADS * LANES
    vb_ref[0] = mm(base, B_KV_HEADS * LANES).astype(BF16)
    tr = _dot_nt(wt_ref[...], h)
    kat_ref[0] = (tr[0:A_WIDTH] * A_HEAD_DIM ** -0.5).astype(BF16)
    gt = tr[A_WIDTH:] + gb_ref[...]
    typ = lax.broadcasted_iota(jnp.int32, gt.shape, 0) & (GATE_ROWS - 1)
    log_sig = jnp.minimum(gt, 0.0) - jnp.log1p(jnp.exp(-jnp.abs(gt)))
    gr_ref[0] = jnp.where((typ == 1) | (typ == 3), log_sig, gt)


def _proj_ab(x, mod, g, w, wg, gb, qn, kn, use_rope):
    batch, t, d = x.shape
    tm = _row_tile(t, PROJ_ROWS)
    cos, sin = _rope_tables(t, B_HEAD_DIM, use_rope)
    mod_map = (lambda b, i: (b, 0, 0)) if mod.shape[0] == batch else (lambda b, i: (0, 0, 0))
    const = lambda b, i: (0, 0)
    row = lambda width: pl.BlockSpec((1, tm, width), lambda b, i: (b, i, 0))
    kvw = B_KV_HEADS * LANES
    out_shape = [
        jax.ShapeDtypeStruct((batch, t, A_WIDTH), BF16),
        jax.ShapeDtypeStruct((batch, A_WIDTH, t), BF16),
        jax.ShapeDtypeStruct((batch, t, A_WIDTH), BF16),
        jax.ShapeDtypeStruct((batch, t, A_WIDTH), F32),
        jax.ShapeDtypeStruct((batch, t, B_WIDTH), BF16),
        jax.ShapeDtypeStruct((batch, t, kvw), BF16),
        jax.ShapeDtypeStruct((batch, t, kvw), BF16),
        jax.ShapeDtypeStruct((batch, A_HEADS * GATE_ROWS, t), F32),
    ]
    col = lambda height: pl.BlockSpec((1, height, tm), lambda b, i: (b, 0, i))
    out_specs = [row(A_WIDTH), col(A_WIDTH), row(A_WIDTH), row(A_WIDTH), row(B_WIDTH), row(kvw), row(kvw),
                 col(A_HEADS * GATE_ROWS)]
    return pl.pallas_call(
        _proj_ab_kernel,
        grid=(batch, t // tm),
        in_specs=[
            row(d),
            pl.BlockSpec((1, MOD_ROWS, d), mod_map),
            pl.BlockSpec((1, d), const),
            pl.BlockSpec(w.shape, const),
            pl.BlockSpec(wg.shape, const),
            pl.BlockSpec(gb.shape, const),
            pl.BlockSpec((tm, LANES), lambda b, i: (i, 0)),
            pl.BlockSpec((tm, LANES), lambda b, i: (i, 0)),
            pl.BlockSpec((1, LANES), const),
            pl.BlockSpec((1, LANES), const),
        ],
        out_specs=out_specs,
        out_shape=out_shape,
        compiler_params=_params("arbitrary", "arbitrary"),
        name="proj_ab",
    )(x, mod, g, w, wg, gb, cos, sin, qn, kn)


def _ab_weights(w_in, gate_b):
    bounds = [A_WIDTH * 4, A_WIDTH * 4 + A_GATES, A_WIDTH * 4 + A_GATES + B_WIDTH,
              A_WIDTH * 4 + A_GATES + B_WIDTH + B_KV_WIDTH]
    wa, wgate, wqb, wkb, wvb = jnp.split(w_in, bounds, axis=1)
    wqa, wka, wva_oa = wa[:, :A_WIDTH], wa[:, A_WIDTH:2 * A_WIDTH], wa[:, 2 * A_WIDTH:]

    def dup(wk):
        parts = []
        for hh in range(B_KV_HEADS):
            blk = wk[:, hh * B_HEAD_DIM:(hh + 1) * B_HEAD_DIM]
            parts += [blk] * (LANES // B_HEAD_DIM)
        return jnp.concatenate(parts, axis=1)

    w = jnp.concatenate([wqa, wva_oa, wqb, dup(wkb), dup(wvb)], axis=1).astype(BF16)
    d = w_in.shape[0]
    wg = wgate.reshape(d, 4, A_HEADS).transpose(2, 1, 0)
    wg = jnp.pad(wg, ((0, 0), (0, GATE_ROWS - 4), (0, 0))).reshape(A_HEADS * GATE_ROWS, d)
    wt = jnp.concatenate([wka.T, wg], axis=0).astype(BF16)
    gb = gate_b.reshape(4, A_HEADS).T
    gb = jnp.pad(gb, ((0, 0), (0, GATE_ROWS - 4))).reshape(A_HEADS * GATE_ROWS, 1).astype(F32)
    return w, wt, gb


def _proj_c_kernel(x_ref, mod_ref, g_ref, w_ref, cos_ref, sin_ref, qn_ref, kn_ref, q_ref, k_ref, v_ref):
    mod = mod_ref[0]
    h = _modulated_norm(x_ref[0], g_ref[...], mod[0:1], mod[1:2]).astype(BF16)

    def mm(lo, n):
        return _dot(h, w_ref[:, lo:lo + n])

    cos, sin = cos_ref[...], sin_ref[...]
    q_scale = C_HEAD_DIM ** -0.5 * LOG2_E
    for j in range(C_HEADS):
        y = _head_norm_rope(mm(LANES * j, LANES), qn_ref[...], cos, sin, C_HEAD_DIM, q_scale)
        q_ref[0, :, LANES * j:LANES * (j + 1)] = y.astype(BF16)
    for j in range(C_KV_HEADS):
        y = _head_norm_rope(mm(C_WIDTH + LANES * j, LANES), kn_ref[...], cos, sin, C_HEAD_DIM, 1.0)
        k_ref[0, :, LANES * j:LANES * (j + 1)] = y.astype(BF16)
    v_ref[0] = mm(C_WIDTH + C_KV_WIDTH, C_KV_WIDTH).astype(BF16)


def _proj_c(x, mod, g, w, qn, kn, use_rope):
    batch, t, d = x.shape
    tm = _row_tile(t, PROJ_ROWS)
    cos, sin = _rope_tables(t, C_HEAD_DIM, use_rope)
    mod_map = (lambda b, i: (b, 0, 0)) if mod.shape[0] == batch else (lambda b, i: (0, 0, 0))
    const = lambda b, i: (0, 0)
    row = lambda width: pl.BlockSpec((1, tm, width), lambda b, i: (b, i, 0))
    return pl.pallas_call(
        _proj_c_kernel,
        grid=(batch, t // tm),
        in_specs=[
            row(d),
            pl.BlockSpec((1, MOD_ROWS, d), mod_map),
            pl.BlockSpec((1, d), const),
            pl.BlockSpec(w.shape, const),
            pl.BlockSpec((tm, LANES), lambda b, i: (i, 0)),
            pl.BlockSpec((tm, LANES), lambda b, i: (i, 0)),
            pl.BlockSpec((1, LANES), const),
            pl.BlockSpec((1, LANES), const),
        ],
        out_specs=[row(C_WIDTH), row(C_KV_WIDTH), row(C_KV_WIDTH)],
        out_shape=[jax.ShapeDtypeStruct((batch, t, C_WIDTH), BF16),
                   jax.ShapeDtypeStruct((batch, t, C_KV_WIDTH), BF16),
                   jax.ShapeDtypeStruct((batch, t, C_KV_WIDTH), BF16)],
        compiler_params=_params("arbitrary", "arbitrary"),
        name="proj_c",
    )(x, mod, g, w, cos, sin, qn, kn)


def _mlstm_kernel(ql_ref, ktl_ref, vl_ref, oal_ref, gl_ref, qc_ref, ktc_ref, vc_ref, oac_ref, gc_ref, ng_ref,
                  hl_ref, hc_ref,
                  qs, kts, vs, gs, rowq, rowg, cols, stats, cst, mst, cstate, *, t_ctx, t_lat):
    L = BLOCK
    ncc = t_ctx // L
    ncl = t_lat // L
    nc = ncc + ncl

    qs[0:t_lat] = ql_ref[0]
    qs[t_lat:] = qc_ref[0]
    kts[:, 0:t_lat] = ktl_ref[0]
    kts[:, t_lat:] = ktc_ref[0]
    vs[0:t_lat] = vl_ref[0]
    vs[t_lat:] = vc_ref[0]
    for r in range(4):
        gs[r, 0:ncl] = gl_ref[0, r]
        gs[r, ncl:nc] = gc_ref[0, r]

    ri = lax.broadcasted_iota(jnp.int32, (L, L), 0)
    ci = lax.broadcasted_iota(jnp.int32, (L, L), 1)
    lower = ci <= ri
    upper = ci >= ri
    ones_blk = jnp.ones((L, L), BF16)

    def chunk(i):
        return pl.ds(pl.multiple_of(i * L, L), L)

    def cum(x, mat):
        hi, mid, lo = _split3(x)
        return _dot(hi, mat) + _dot(mid, mat) + _dot(lo, mat)

    li_f, lf_f, li_b, lf_b = (gs[r, 0:nc] for r in range(4))
    bcum_f = cum(lf_f, jnp.where(upper, 1.0, 0.0).astype(BF16))
    bsuf_b = cum(lf_b, jnp.where(lower, 1.0, 0.0).astype(BF16))
    bl_f = bcum_f[:, L - 1:L]
    bl_b = bsuf_b[:, 0:1]
    wl_f = bl_f - bcum_f + li_f
    wl_b = bl_b - bsuf_b + li_b
    mx_f = jnp.max(wl_f, axis=1, keepdims=True)
    mx_b = jnp.max(wl_b, axis=1, keepdims=True)
    rb_f = li_f - bcum_f
    rb_b = li_b - bsuf_b
    lane = lax.broadcasted_iota(jnp.int32, (nc, L), 1)
    pm_f, pm_b = rb_f, rb_b
    step = 1
    while step < L:
        pm_f = jnp.where(lane >= step, jnp.maximum(pm_f, pltpu.roll(pm_f, step, 1)), pm_f)
        pm_b = jnp.where(lane < L - step, jnp.maximum(pm_b, pltpu.roll(pm_b, L - step, 1)), pm_b)
        step *= 2
    for n, val in enumerate((bcum_f, bsuf_b, rb_f, rb_b, jnp.exp(wl_f - mx_f), jnp.exp(wl_b - mx_b), pm_f, pm_b)):
        rowq[n, 0:nc] = val
    for n, val in enumerate((bl_f, bl_b, mx_f, mx_b)):
        stats[n, 0:nc] = jnp.broadcast_to(val, (nc, L))

    def prep(i, _):
        kt = kts[:, chunk(i)].astype(F32)
        vaug = jnp.concatenate([vs[chunk(i), :], ones_blk], axis=1)
        for d in range(2):
            cst[d, i] = _dot((kt * rowq[4 + d, pl.ds(i, 1), :]).astype(BF16), vaug)
        return 0

    lax.fori_loop(0, nc, prep, 0, unroll=2)

    cstate[...] = jnp.zeros_like(cstate)

    def scan_step(j, carry):
        new = []
        for d in range(2):
            m = carry[d]
            if d == 0:
                i = jnp.where(j < ncc, ncl + j, j - ncc)
            else:
                i = nc - 1 - j
            c_old = cstate[d]
            kv = cst[d, i]
            cst[d, i] = c_old
            mst[d, pl.ds(i, 1), :] = jnp.broadcast_to(m, (1, L))
            bl = stats[d, pl.ds(i, 1), 0:1]
            mx = stats[2 + d, pl.ds(i, 1), 0:1]
            m_new = jnp.maximum(bl + m, mx)
            cstate[d] = jnp.exp(bl + m - m_new) * c_old + jnp.exp(mx - m_new) * kv
            new.append(m_new)
        return tuple(new)

    m0 = jnp.zeros((1, 1), F32)
    lax.fori_loop(0, nc, scan_step, (m0, m0))

    for d in range(2):
        g = jnp.maximum(mst[d, 0:nc], rowq[6 + d, 0:nc])
        rowg[d, 0:nc] = g
        rowg[2 + d, 0:nc] = jnp.exp(-rowq[d, 0:nc] - g)

    def to_cols(i, _):
        r8 = jnp.concatenate([rowg[n, pl.ds(i, 1), :] for n in range(4)] + [jnp.zeros((4, L), F32)], axis=0)
        cols[chunk(i), :] = jnp.concatenate([r8, jnp.zeros((L - 8, L), F32)], axis=0).T
        return 0

    lax.fori_loop(0, nc, to_cols, 0, unroll=2)

    ng = ng_ref[...]

    def emit(i, oa, out_ref, out_rows):
        q = qs[chunk(i), :]
        vaug = jnp.concatenate([vs[chunk(i), :], ones_blk], axis=1)
        qk = _dot(q, kts[:, chunk(i)])
        colblk = cols[chunk(i), :]
        h = None
        for d in range(2):
            g = jnp.broadcast_to(colblk[:, d:d + 1], (L, L))
            floor = colblk[:, 2 + d:3 + d]
            rb = rowq[2 + d, pl.ds(i, 1), :]
            mask = lower if d == 0 else upper
            dm = jnp.where(mask, jnp.exp(rb - g), 0.0)
            s = (qk * dm).astype(BF16)
            iw = jnp.exp(mst[d, pl.ds(i, 1), 0:1] - g)
            nd = jnp.concatenate([iw, iw], axis=1) * _dot(q, cst[d, i].astype(BF16)) + _dot(s, vaug)
            hd = nd[:, 0:L] / jnp.maximum(jnp.abs(nd[:, L:2 * L]), floor)
            h = hd if h is None else h + hd
        ms = jnp.mean(h * h, axis=-1, keepdims=True)
        y = (h * lax.rsqrt(ms + EPS) * ng) * jax.nn.sigmoid(oa)
        out_ref[0, out_rows, :] = y.astype(BF16)

    def emit_ctx(i, _):
        emit(i + ncl, oac_ref[0, chunk(i), :], hc_ref, chunk(i))
        return 0

    def emit_lat(i, _):
        emit(i, oal_ref[0, chunk(i), :], hl_ref, chunk(i))
        return 0

    lax.fori_loop(0, ncc, emit_ctx, 0, unroll=2)
    lax.fori_loop(0, ncl, emit_lat, 0, unroll=4)


def _mlstm(qa_l, kat_l, va_l, oa_l, gr_l, qa_c, kat_c, va_c, oa_c, gr_c, norm_g):
    batch, t_lat, _ = qa_l.shape
    t_ctx = qa_c.shape[1]
    t_all = t_ctx + t_lat
    nc = t_all // BLOCK
    nc_pad = -(-nc // 8) * 8
    head = lambda t: pl.BlockSpec((1, t, A_HEAD_DIM), lambda b, h: (b, 0, h))
    head_t = lambda t: pl.BlockSpec((1, A_HEAD_DIM, t), lambda b, h: (b, h, 0))
    gates = lambda t: pl.BlockSpec((1, GATE_ROWS, t // BLOCK, BLOCK), lambda b, h: (b, h, 0, 0))
    chunked = lambda g: g.reshape(batch, A_HEADS * GATE_ROWS, g.shape[2] // BLOCK, BLOCK)
    return pl.pallas_call(
        functools.partial(_mlstm_kernel, t_ctx=t_ctx, t_lat=t_lat),
        grid=(batch, A_HEADS),
        in_specs=[head(t_lat), head_t(t_lat), head(t_lat), head(t_lat), gates(t_lat),
                  head(t_ctx), head_t(t_ctx), head(t_ctx), head(t_ctx), gates(t_ctx),
                  pl.BlockSpec((1, A_HEAD_DIM), lambda b, h: (0, h))],
        out_specs=[head(t_lat), head(t_ctx)],
        out_shape=[jax.ShapeDtypeStruct((batch, t_lat, A_WIDTH), BF16),
                   jax.ShapeDtypeStruct((batch, t_ctx, A_WIDTH), BF16)],
        scratch_shapes=[
            pltpu.VMEM((t_all, A_HEAD_DIM), BF16),
            pltpu.VMEM((A_HEAD_DIM, t_all), BF16),
            pltpu.VMEM((t_all, A_HEAD_DIM), BF16),
            pltpu.VMEM((4, nc_pad, BLOCK), F32),
            pltpu.VMEM((8, nc_pad, BLOCK), F32),
            pltpu.VMEM((4, nc_pad, BLOCK), F32),
            pltpu.VMEM((t_all, LANES), F32),
            pltpu.VMEM((4, nc_pad, LANES), F32),
            pltpu.VMEM((2, nc, A_HEAD_DIM, 2 * A_HEAD_DIM), F32),
            pltpu.VMEM((2, nc_pad, LANES), F32),
            pltpu.VMEM((2, A_HEAD_DIM, 2 * A_HEAD_DIM), F32),
        ],
        compiler_params=_params("arbitrary", "arbitrary"),
        name="mlstm",
    )(qa_l, kat_l, va_l, oa_l, chunked(gr_l), qa_c, kat_c, va_c, oa_c, chunked(gr_c),
      norm_g.reshape(1, A_WIDTH))


def _swa_kernel(sink_ref, q_ref, *refs, t_lat, has_window, blocks):
    if has_window:
        bias_ref, k_ref, v_ref, kx_ref, vx_ref, o_ref, s_scr = refs
    else:
        kx_ref, vx_ref, o_ref, s_scr = refs
    L = BLOCK
    nb = t_lat // L
    group = B_HEADS // B_KV_HEADS
    lane = lax.broadcasted_iota(jnp.int32, (L, LANES), 1)
    lo = lane < B_HEAD_DIM
    zero = jnp.zeros((L, LANES), BF16)
    def window(u):
        qblk = pl.program_id(1) * blocks + u
        start = pl.multiple_of(jnp.clip((qblk - 1) * L, 0, t_lat - 3 * L), L)
        return qblk, pl.ds(start, 3 * L)

    def logits(c, u, kvh):
        sl = slice(kvh * LANES, (kvh + 1) * LANES)
        q = q_ref[0, u * L:(u + 1) * L, :]
        qa = q[:, (2 * kvh) * LANES:(2 * kvh + 1) * LANES]
        qb = q[:, (2 * kvh + 1) * LANES:(2 * kvh + 2) * LANES]
        q4 = jnp.concatenate([jnp.where(lo, qa, zero), jnp.where(lo, zero, qa),
                              jnp.where(lo, qb, zero), jnp.where(lo, zero, qb)], axis=0)
        sink = jnp.concatenate([jnp.full((L, 1), sink_ref[kvh * group + g] * LOG2_E, F32)
                                for g in range(group)], axis=0)
        if has_window:
            qblk, win = window(u)
            bias = bias_ref[jnp.where(qblk == 0, 0, jnp.where(qblk == nb - 1, 2, 1))]
            keys = jnp.concatenate([k_ref[0, win, sl], kx_ref[0, :, sl]], axis=0)
            s = _dot_nt(q4, keys)
            s = (s.reshape(group, L, s.shape[1]) + bias[None]).reshape(s.shape)
        else:
            s = _dot_nt(q4, kx_ref[0, :, sl])
        s_scr[c] = s
        return jnp.maximum(jnp.max(s, axis=1, keepdims=True), sink), sink

    def attend(c, u, kvh, m, sink):
        sl = slice(kvh * LANES, (kvh + 1) * LANES)
        if has_window:
            vals = jnp.concatenate([v_ref[0, window(u)[1], sl], vx_ref[0, :, sl]], axis=0)
        else:
            vals = vx_ref[0, :, sl]
        e = jnp.exp2(s_scr[c] - m)
        denom = jnp.sum(e, axis=1, keepdims=True) + jnp.exp2(sink - m)
        o4 = _dot(e.astype(BF16), vals) / denom
        rows = slice(u * L, (u + 1) * L)
        o_ref[0, rows, (2 * kvh) * LANES:(2 * kvh + 1) * LANES] = (
            jnp.where(lo, o4[0:L], o4[L:2 * L]).astype(BF16))
        o_ref[0, rows, (2 * kvh + 1) * LANES:(2 * kvh + 2) * LANES] = (
            jnp.where(lo, o4[2 * L:3 * L], o4[3 * L:4 * L]).astype(BF16))

    chains = [(u, kvh) for u in range(blocks) for kvh in range(B_KV_HEADS)]
    stats = [logits(0, *chains[0])]
    for c, (u, kvh) in enumerate(chains):
        if c + 1 < len(chains):
            stats.append(logits(c + 1, *chains[c + 1]))
        attend(c, u, kvh, *stats[c])


def _band_bias(t_ctx):
    L = BLOCK
    t = jnp.arange(L)[:, None]
    j = jnp.arange(3 * L)[None, :]
    tables = []
    for rel in (0, -L, -2 * L):
        ok = jnp.abs(j - t + rel) <= WINDOW
        tables.append(jnp.concatenate([jnp.where(ok, 0.0, -jnp.inf), jnp.zeros((L, t_ctx))], axis=1))
    return jnp.stack(tables).astype(F32)


def _swa(sink, q, k, v, k_ctx, v_ctx, has_window):
    batch, t, _ = q.shape
    t_ctx = k_ctx.shape[1]
    nb = t // BLOCK
    blocks = min(SWA_BLOCKS, nb)
    assert nb % blocks == 0 and (nb >= 3 or not has_window)
    kvw = B_KV_HEADS * LANES
    cur = lambda b, i: (b, i, 0)
    whole = lambda b, i: (b, 0, 0)
    ctx_spec = pl.BlockSpec((1, t_ctx, kvw), whole)
    in_specs = [pl.BlockSpec(memory_space=pltpu.SMEM), pl.BlockSpec((1, blocks * BLOCK, B_WIDTH), cur)]
    args = [sink, q]
    if has_window:
        bias = _band_bias(t_ctx)
        in_specs += [pl.BlockSpec(bias.shape, lambda b, i: (0, 0, 0)),
                     pl.BlockSpec((1, t, kvw), whole), pl.BlockSpec((1, t, kvw), whole)]
        args += [bias, k, v]
    in_specs += [ctx_spec, ctx_spec]
    args += [k_ctx, v_ctx]
    return pl.pallas_call(
        functools.partial(_swa_kernel, t_lat=t, has_window=has_window, blocks=blocks),
        grid=(batch, nb // blocks),
        in_specs=in_specs,
        out_specs=pl.BlockSpec((1, blocks * BLOCK, B_WIDTH), cur),
        scratch_shapes=[pltpu.VMEM((blocks * B_KV_HEADS, (B_HEADS // B_KV_HEADS) * BLOCK,
                                    (3 * BLOCK if has_window else 0) + t_ctx), F32)],
        out_shape=jax.ShapeDtypeStruct((batch, t, B_WIDTH), BF16),
        compiler_params=_params("arbitrary", "arbitrary"),
        name="swa" if has_window else "swa_ctx",
    )(*args)


def _flash_kernel(q_ref, k_ref, v_ref, o_ref, s0_scr, s1_scr, m0_scr, m1_scr, *, chunks, tq, unroll):
    group = C_HEADS // C_KV_HEADS
    nq = q_ref.shape[1] // tq

    def q_rows(t):
        return pl.ds(pl.multiple_of(t * tq, tq), tq)

    def load_q(t):
        q = q_ref[0, q_rows(t), :]
        return jnp.concatenate([q[:, g * LANES:(g + 1) * LANES] for g in range(group)], axis=0)

    def stage_a(q4, s_ref, m, start, size):
        s = _dot_nt(q4, k_ref[0, start:start + size, :])
        s_ref[:, start:start + size] = s
        cm = jnp.max(s, axis=1, keepdims=True)
        return cm if m is None else jnp.maximum(m, cm)

    def stage_b(s_ref, m, acc, start, size):
        p = jnp.exp2(s_ref[:, start:start + size] - m)
        pv = _dot(p.astype(BF16), v_ref[0, start:start + size, :])
        return pv if acc is None else acc + pv

    def step(t, cur, nxt):
        q4 = load_q(jnp.minimum(t + 1, nq - 1))
        m_cur = cur[1][...]
        m_next, acc = None, None
        for start, size in chunks:
            m_next = stage_a(q4, nxt[0], m_next, start, size)
            acc = stage_b(cur[0], m_cur, acc, start, size)
        nxt[1][...] = m_next
        out = acc[:, 0:LANES] / acc[:, LANES:2 * LANES]
        for g in range(group):
            o_ref[0, q_rows(t), g * LANES:(g + 1) * LANES] = out[g * tq:(g + 1) * tq].astype(BF16)

    bufs = ((s0_scr, m0_scr), (s1_scr, m1_scr))
    q4 = load_q(0)
    m = None
    for start, size in chunks:
        m = stage_a(q4, s0_scr, m, start, size)
    m0_scr[...] = m

    def body(u, _):
        for k in range(unroll):
            step(unroll * u + k, bufs[k % 2], bufs[(k + 1) % 2])
        return 0

    lax.fori_loop(0, nq // unroll, body, 0)


def _key_chunks(total, target):
    chunks = []
    start = 0
    while start < total:
        size = min(target, total - start)
        chunks.append((start, size))
        start += size
    return tuple(chunks)


def _flash(q, k, v, t_ctx):
    batch, t, _ = q.shape
    tk_all = k.shape[1]
    tq = _row_tile(t, FLASH_ROWS)
    parts = FLASH_PARTS if t % (FLASH_PARTS * 2 * tq) == 0 else 1
    tp = t // parts
    unroll = FLASH_UNROLL if (tp // tq) % FLASH_UNROLL == 0 else 2
    assert unroll % 2 == 0 and (tp // tq) % unroll == 0
    group = C_HEADS // C_KV_HEADS
    gw = group * C_HEAD_DIM
    v_heads = v.reshape(batch, tk_all, C_KV_HEADS, C_HEAD_DIM)
    v_aug = jnp.concatenate([v_heads, jnp.ones_like(v_heads)], axis=-1).reshape(batch, tk_all, -1)
    chunks = (_key_chunks(t_ctx, FLASH_KEYS)
              + tuple((t_ctx + a, n) for a, n in _key_chunks(tk_all - t_ctx, FLASH_KEYS)))
    return pl.pallas_call(
        functools.partial(_flash_kernel, chunks=chunks, tq=tq, unroll=unroll),
        grid=(batch, C_KV_HEADS, parts),
        in_specs=[pl.BlockSpec((1, tp, gw), lambda b, h, p: (b, p, h)),
                  pl.BlockSpec((1, tk_all, C_HEAD_DIM), lambda b, h, p: (b, 0, h)),
                  pl.BlockSpec((1, tk_all, 2 * C_HEAD_DIM), lambda b, h, p: (b, 0, h))],
        out_specs=pl.BlockSpec((1, tp, gw), lambda b, h, p: (b, p, h)),
        out_shape=jax.ShapeDtypeStruct((batch, t, C_WIDTH), BF16),
        scratch_shapes=[pltpu.VMEM((group * tq, tk_all), F32), pltpu.VMEM((group * tq, tk_all), F32),
                        pltpu.VMEM((group * tq, 1), F32), pltpu.VMEM((group * tq, 1), F32)],
        compiler_params=_params("arbitrary", "arbitrary", "arbitrary"),
        name="flash_c",
    )(q, k, v_aug)


def _out_mlp_kernel(*refs, n_mix, ff_chunk):
    x_ref, mod_ref, g_ref = refs[0:3]
    mix_refs = refs[3:3 + n_mix]
    wo_refs = refs[3 + n_mix:3 + 2 * n_mix]
    w1_ref, w2_ref, o_ref = refs[3 + 2 * n_mix:]
    mod = mod_ref[0]
    y = None
    for a_ref, w_ref in zip(mix_refs, wo_refs):
        part = _dot(a_ref[0], w_ref[...])
        y = part if y is None else y + part
    x1 = x_ref[0] + mod[2:3] * y
    h = _modulated_norm(x1, g_ref[...], mod[3:4], mod[4:5]).astype(BF16)
    d_ff = w1_ref.shape[1]
    acc = None
    for c in range(d_ff // ff_chunk):
        a = _dot(h, w1_ref[:, c * ff_chunk:(c + 1) * ff_chunk])
        a = jnp.square(jnp.maximum(a, 0.0)).astype(BF16)
        part = _dot(a, w2_ref[c * ff_chunk:(c + 1) * ff_chunk, :])
        acc = part if acc is None else acc + part
    o_ref[0] = x1 + mod[5:6] * acc


def _out_mlp(x, mod, g, mixes, w_outs, w1, w2):
    batch, t, d = x.shape
    tm = _row_tile(t, MLP_ROWS)
    mod_map = (lambda b, i: (b, 0, 0)) if mod.shape[0] == batch else (lambda b, i: (0, 0, 0))
    const = lambda b, i: (0, 0)
    resident = lambda a: pl.BlockSpec(a.shape, const, pipeline_mode=pl.Buffered(1))
    row = lambda width: pl.BlockSpec((1, tm, width), lambda b, i: (b, i, 0))
    return pl.pallas_call(
        functools.partial(_out_mlp_kernel, n_mix=len(mixes), ff_chunk=1024),
        grid=(batch, t // tm),
        in_specs=([row(d), pl.BlockSpec((1, MOD_ROWS, d), mod_map), pl.BlockSpec((1, d), const)]
                  + [row(a.shape[2]) for a in mixes] + [resident(w) for w in w_outs]
                  + [resident(w1), resident(w2)]),
        out_specs=row(d),
        out_shape=jax.ShapeDtypeStruct((batch, t, d), F32),
        compiler_params=_params("arbitrary", "arbitrary"),
        name="out_mlp",
    )(x, mod, g, *mixes, *w_outs, w1, w2)


def kernel(x, c, ctx, c_ctx, ada_w, ada_b, norm1_g, norm2_g, ab_w_in, ab_gate_b, mlstm_norm_g, swa_q_norm_g,
           swa_k_norm_g, swa_sink, ab_w_out, c_w_in, c_q_norm_g, c_k_norm_g, c_w_out, mlp_w1, mlp_w2):
    depth = ada_w.shape[0]
    batch, _, d = x.shape
    mods = _mods(c, c_ctx, ada_w, ada_b)
    for layer in range(depth):
        last = layer == depth - 1
        mod_l = mods[layer, :batch]
        mod_c = mods[layer, batch:batch + 1]
        g1 = norm1_g[layer].reshape(1, d)
        g2 = norm2_g[layer].reshape(1, d)
        w1 = mlp_w1[layer].astype(BF16)
        w2 = mlp_w2[layer].astype(BF16)
        j = layer // 2
        if layer % 2 == 0:
            w, wt, gb = _ab_weights(ab_w_in[j], ab_gate_b[j])
            qn = jnp.tile(swa_q_norm_g[j], LANES // B_HEAD_DIM).reshape(1, LANES)
            kn = jnp.tile(swa_k_norm_g[j], LANES // B_HEAD_DIM).reshape(1, LANES)
            qa_l, kat_l, va_l, oa_l, qb_l, kb_l, vb_l, gr_l = _proj_ab(x, mod_l, g1, w, wt, gb, qn, kn, True)
            qa_c, kat_c, va_c, oa_c, qb_c, kb_c, vb_c, gr_c = _proj_ab(ctx, mod_c, g1, w, wt, gb, qn, kn, False)
            ha_l, ha_c = _mlstm(qa_l, kat_l, va_l, oa_l, gr_l, qa_c, kat_c, va_c, oa_c, gr_c, mlstm_norm_g[j])
            ob_l = _swa(swa_sink[j], qb_l, kb_l, vb_l, kb_c, vb_c, True)
            w_out = ab_w_out[j].astype(BF16)
            w_outs = [w_out[:A_WIDTH], w_out[A_WIDTH:]]
            x = _out_mlp(x, mod_l, g2, [ha_l, ob_l], w_outs, w1, w2)
            if not last:
                ob_c = _swa(swa_sink[j], qb_c, kb_c, vb_c, kb_c, vb_c, False)
                ctx = _out_mlp(ctx, mod_c, g2, [ha_c, ob_c], w_outs, w1, w2)
        else:
            w = c_w_in[j].astype(BF16)
            qn = c_q_norm_g[j].reshape(1, LANES)
            kn = c_k_norm_g[j].reshape(1, LANES)
            q_l, k_l, v_l = _proj_c(x, mod_l, g1, w, qn, kn, True)
            q_c, k_c, v_c = _proj_c(ctx, mod_c, g1, w, qn, kn, False)
            k_all = jnp.concatenate([k_c, k_l], axis=1)
            v_all = jnp.concatenate([v_c, v_l], axis=1)
            w_out = c_w_out[j].astype(BF16)
            o_l = _flash(q_l, k_all, v_all, k_c.shape[1])
            x = _out_mlp(x, mod_l, g2, [o_l], [w_out], w1, w2)
            if not last:
                o_c = _flash(q_c, k_c, v_c, 0)
                ctx = _out_mlp(ctx, mod_c, g2, [o_c], [w_out], w1, w2)
    return x
```

```python
import functools

import jax
import jax.numpy as jnp
from jax import lax
from jax.experimental import pallas as pl
from jax.experimental.pallas import tpu as pltpu

F32 = jnp.float32
BF16 = jnp.bfloat16

GRID_W = 64
BLOCK = 128
WINDOW = 128
ROPE_BASE = 10000.0
EPS = 1e-6
LOG2_E = 1.4426950408889634
PROJ_ROWS = 512
MLP_ROWS = 512
SWA_BLOCKS = 4
FLASH_KEYS = 512
FLASH_UNROLL = 4
FLASH_PARTS = 1
FLASH_ROWS = 128
N_MOD = 6
MOD_ROWS = 8
LANES = 128
MXU_WIDTH = 256

A_HEADS = 4
A_HEAD_DIM = 128
A_WIDTH = A_HEADS * A_HEAD_DIM
A_GATES = 4 * A_HEADS
GATE_ROWS = 8
B_HEADS = 8
B_KV_HEADS = 2
B_HEAD_DIM = 64
B_WIDTH = B_HEADS * B_HEAD_DIM
B_KV_WIDTH = B_KV_HEADS * B_HEAD_DIM
C_HEADS = 8
C_KV_HEADS = 2
C_HEAD_DIM = 128
C_WIDTH = C_HEADS * C_HEAD_DIM
C_KV_WIDTH = C_KV_HEADS * C_HEAD_DIM

VMEM_LIMIT = 56 * 1024 * 1024

NT_DIMS = (((1,), (1,)), ((), ()))
TN_DIMS = (((0,), (0,)), ((), ()))


def _dot(a, b):
    return jnp.dot(a, b, preferred_element_type=F32)


def _dot_nt(a, b):
    return lax.dot_general(a, b, NT_DIMS, preferred_element_type=F32)


def _dot_tn(a, b):
    return lax.dot_general(a, b, TN_DIMS, preferred_element_type=F32)


def _params(*sem):
    return pltpu.CompilerParams(dimension_semantics=sem, vmem_limit_bytes=VMEM_LIMIT)


def _mods_kernel(c_ref, w_ref, b_ref, o_ref):
    cf = c_ref[...]
    s = (cf * jax.nn.sigmoid(cf)).astype(BF16)
    o_ref[0] = _dot(s, w_ref[0].astype(BF16)) + b_ref[0]


def _mods(c, c_ctx, ada_w, ada_b):
    depth, d, _ = ada_w.shape
    batch = c.shape[0]
    rows = -(-(batch + 1) // 8) * 8
    cc = jnp.zeros((rows, d), F32).at[:batch].set(c).at[batch].set(c_ctx)
    out = pl.pallas_call(
        _mods_kernel,
        grid=(depth, N_MOD),
        in_specs=[
            pl.BlockSpec((rows, d), lambda l, j: (0, 0)),
            pl.BlockSpec((1, d, d), lambda l, j: (l, 0, j)),
            pl.BlockSpec((1, 1, d), lambda l, j: (l, 0, j)),
        ],
        out_specs=pl.BlockSpec((1, rows, d), lambda l, j: (l, 0, j)),
        out_shape=jax.ShapeDtypeStruct((depth, rows, N_MOD * d), F32),
        compiler_params=_params("arbitrary", "arbitrary"),
        name="ada_mods",
    )(cc, ada_w, ada_b.reshape(depth, 1, N_MOD * d))
    out = out.reshape(depth, rows, N_MOD, d)
    return jnp.pad(out, ((0, 0), (0, 0), (0, MOD_ROWS - N_MOD), (0, 0)))


def _modulated_norm(x, g, shift, scale):
    ms = jnp.mean(x * x, axis=-1, keepdims=True)
    return (x * lax.rsqrt(ms + EPS) * g) * (1.0 + scale) + shift


def _split3(a):
    hi = a.astype(BF16)
    r = a - hi.astype(F32)
    mid = r.astype(BF16)
    lo = (r - mid.astype(F32)).astype(BF16)
    return hi, mid, lo


def _head_norm_rope(xh, gain, cos, sin, group, out_scale):
    sq = xh * xh
    half = group // 2
    if group == LANES:
        ssq = jnp.sum(sq, axis=-1, keepdims=True)
    else:
        assert 2 * group == LANES
        lane = lax.broadcasted_iota(jnp.int32, xh.shape, 1)
        left = lane < group
        ssq = jnp.where(left, jnp.sum(jnp.where(left, sq, 0.0), axis=-1, keepdims=True),
                        jnp.sum(jnp.where(left, 0.0, sq), axis=-1, keepdims=True))
    xn = xh * lax.rsqrt(ssq * (1.0 / group) + EPS) * gain
    if group == LANES:
        rot = pltpu.roll(xn, half, 1)
    else:
        first = (lane & (group - 1)) < half
        rot = jnp.where(first, pltpu.roll(xn, LANES - half, 1), pltpu.roll(xn, half, 1))
    y = xn * cos + rot * sin
    if out_scale != 1.0:
        y = y * out_scale
    return y


def _rope_tables(n_tokens, head_dim, use_rope):
    reps = LANES // head_dim
    if not use_rope:
        return jnp.ones((n_tokens, LANES), F32), jnp.zeros((n_tokens, LANES), F32)
    rows = n_tokens // GRID_W
    row = jnp.broadcast_to(jnp.arange(rows)[:, None], (rows, GRID_W)).reshape(n_tokens).astype(F32)
    col = jnp.broadcast_to(jnp.arange(GRID_W)[None, :], (rows, GRID_W)).reshape(n_tokens).astype(F32)
    pairs = head_dim // 4
    inv_freq = ROPE_BASE ** (-jnp.arange(pairs, dtype=F32) / pairs)
    ang = jnp.concatenate([row[:, None] * inv_freq, col[:, None] * inv_freq], axis=-1)
    cos, sin = jnp.cos(ang), jnp.sin(ang)
    cos_full = jnp.concatenate([cos, cos], axis=-1)
    sin_signed = jnp.concatenate([-sin, sin], axis=-1)
    return jnp.tile(cos_full, (1, reps)), jnp.tile(sin_signed, (1, reps))


def _row_tile(t, target):
    tm = min(t, target)
    assert t % tm == 0
    return tm


def _proj_ab_kernel(x_ref, mod_ref, g_ref, w_ref, wt_ref, gb_ref, cos_ref, sin_ref, qn_ref, kn_ref,
                    qa_ref, kat_ref, va_ref, oa_ref, qb_ref, kb_ref, vb_ref, gr_ref):
    mod = mod_ref[0]
    h = _modulated_norm(x_ref[0], g_ref[...], mod[0:1], mod[1:2]).astype(BF16)

    def mm(lo, n):
        return _dot(h, w_ref[:, lo:lo + n])

    qa_ref[0] = mm(0, A_WIDTH).astype(BF16)
    va_ref[0] = mm(A_WIDTH, A_WIDTH).astype(BF16)
    oa_ref[0] = mm(2 * A_WIDTH, A_WIDTH)
    cos, sin = cos_ref[...], sin_ref[...]
    base = 3 * A_WIDTH
    for j in range(B_WIDTH // LANES):
        y = _head_norm_rope(mm(base + LANES * j, LANES), qn_ref[...], cos, sin, B_HEAD_DIM,
                            B_HEAD_DIM ** -0.5 * LOG2_E)
        qb_ref[0, :, LANES * j:LANES * (j + 1)] = y.astype(BF16)
    base += B_WIDTH
    for j in range(B_KV_HEADS):
        y = _head_norm_rope(mm(base + LANES * j, LANES), kn_ref[...], cos, sin, B_HEAD_DIM, 1.0)
        kb_ref[0, :, LANES * j:LANES * (j + 1)] = y.astype(BF16)
    base += B_KV_HEADS * LANES
    vb_ref[0] = mm(base, B_KV_HEADS * LANES).astype(BF16)
    tr = _dot_nt(wt_ref[...], h)
    kat_ref[0] = (tr[0:A_WIDTH] * A_HEAD_DIM ** -0.5).astype(BF16)
    gt = tr[A_WIDTH:] + gb_ref[...]
    typ = lax.broadcasted_iota(jnp.int32, gt.shape, 0) & (GATE_ROWS - 1)
    log_sig = jnp.minimum(gt, 0.0) - jnp.log1p(jnp.exp(-jnp.abs(gt)))
    gr_ref[0] = jnp.where((typ == 1) | (typ == 3), log_sig, gt)


def _proj_ab(x, mod, g, w, wg, gb, qn, kn, use_rope):
    batch, t, d = x.shape
    tm = _row_tile(t, PROJ_ROWS)
    cos, sin = _rope_tables(t, B_HEAD_DIM, use_rope)
    mod_map = (lambda b, i: (b, 0, 0)) if mod.shape[0] == batch else (lambda b, i: (0, 0, 0))
    const = lambda b, i: (0, 0)
    row = lambda width: pl.BlockSpec((1, tm, width), lambda b, i: (b, i, 0))
    kvw = B_KV_HEADS * LANES
    out_shape = [
        jax.ShapeDtypeStruct((batch, t, A_WIDTH), BF16),
        jax.ShapeDtypeStruct((batch, A_WIDTH, t), BF16),
        jax.ShapeDtypeStruct((batch, t, A_WIDTH), BF16),
        jax.ShapeDtypeStruct((batch, t, A_WIDTH), F32),
        jax.ShapeDtypeStruct((batch, t, B_WIDTH), BF16),
        jax.ShapeDtypeStruct((batch, t, kvw), BF16),
        jax.ShapeDtypeStruct((batch, t, kvw), BF16),
        jax.ShapeDtypeStruct((batch, A_HEADS * GATE_ROWS, t), F32),
    ]
    col = lambda height: pl.BlockSpec((1, height, tm), lambda b, i: (b, 0, i))
    out_specs = [row(A_WIDTH), col(A_WIDTH), row(A_WIDTH), row(A_WIDTH), row(B_WIDTH), row(kvw), row(kvw),
                 col(A_HEADS * GATE_ROWS)]
    return pl.pallas_call(
        _proj_ab_kernel,
        grid=(batch, t // tm),
        in_specs=[
            row(d),
            pl.BlockSpec((1, MOD_ROWS, d), mod_map),
            pl.BlockSpec((1, d), const),
            pl.BlockSpec(w.shape, const),
            pl.BlockSpec(wg.shape, const),
            pl.BlockSpec(gb.shape, const),
            pl.BlockSpec((tm, LANES), lambda b, i: (i, 0)),
            pl.BlockSpec((tm, LANES), lambda b, i: (i, 0)),
            pl.BlockSpec((1, LANES), const),
            pl.BlockSpec((1, LANES), const),
        ],
        out_specs=out_specs,
        out_shape=out_shape,
        compiler_params=_params("arbitrary", "arbitrary"),
        name="proj_ab",
    )(x, mod, g, w, wg, gb, cos, sin, qn, kn)


def _ab_weights(w_in, gate_b):
    bounds = [A_WIDTH * 4, A_WIDTH * 4 + A_GATES, A_WIDTH * 4 + A_GATES + B_WIDTH,
              A_WIDTH * 4 + A_GATES + B_WIDTH + B_KV_WIDTH]
    wa, wgate, wqb, wkb, wvb = jnp.split(w_in, bounds, axis=1)
    wqa, wka, wva_oa = wa[:, :A_WIDTH], wa[:, A_WIDTH:2 * A_WIDTH], wa[:, 2 * A_WIDTH:]

    def dup(wk):
        parts = []
        for hh in range(B_KV_HEADS):
            blk = wk[:, hh * B_HEAD_DIM:(hh + 1) * B_HEAD_DIM]
            parts += [blk] * (LANES // B_HEAD_DIM)
        return jnp.concatenate(parts, axis=1)

    w = jnp.concatenate([wqa, wva_oa, wqb, dup(wkb), dup(wvb)], axis=1).astype(BF16)
    d = w_in.shape[0]
    wg = wgate.reshape(d, 4, A_HEADS).transpose(2, 1, 0)
    wg = jnp.pad(wg, ((0, 0), (0, GATE_ROWS - 4), (0, 0))).reshape(A_HEADS * GATE_ROWS, d)
    wt = jnp.concatenate([wka.T, wg], axis=0).astype(BF16)
    gb = gate_b.reshape(4, A_HEADS).T
    gb = jnp.pad(gb, ((0, 0), (0, GATE_ROWS - 4))).reshape(A_HEADS * GATE_ROWS, 1).astype(F32)
    return w, wt, gb


def _proj_c_kernel(x_ref, mod_ref, g_ref, w_ref, cos_ref, sin_ref, qn_ref, kn_ref, q_ref, k_ref, v_ref):
    mod = mod_ref[0]
    h = _modulated_norm(x_ref[0], g_ref[...], mod[0:1], mod[1:2]).astype(BF16)

    def mm(lo, n):
        return _dot(h, w_ref[:, lo:lo + n])

    cos, sin = cos_ref[...], sin_ref[...]
    q_scale = C_HEAD_DIM ** -0.5 * LOG2_E
    for j in range(C_HEADS):
        y = _head_norm_rope(mm(LANES * j, LANES), qn_ref[...], cos, sin, C_HEAD_DIM, q_scale)
        q_ref[0, :, LANES * j:LANES * (j + 1)] = y.astype(BF16)
    for j in range(C_KV_HEADS):
        y = _head_norm_rope(mm(C_WIDTH + LANES * j, LANES), kn_ref[...], cos, sin, C_HEAD_DIM, 1.0)
        k_ref[0, :, LANES * j:LANES * (j + 1)] = y.astype(BF16)
    v_ref[0] = mm(C_WIDTH + C_KV_WIDTH, C_KV_WIDTH).astype(BF16)


def _proj_c(x, mod, g, w, qn, kn, use_rope):
    batch, t, d = x.shape
    tm = _row_tile(t, PROJ_ROWS)
    cos, sin = _rope_tables(t, C_HEAD_DIM, use_rope)
    mod_map = (lambda b, i: (b, 0, 0)) if mod.shape[0] == batch else (lambda b, i: (0, 0, 0))
    const = lambda b, i: (0, 0)
    row = lambda width: pl.BlockSpec((1, tm, width), lambda b, i: (b, i, 0))
    return pl.pallas_call(
        _proj_c_kernel,
        grid=(batch, t // tm),
        in_specs=[
            row(d),
            pl.BlockSpec((1, MOD_ROWS, d), mod_map),
            pl.BlockSpec((1, d), const),
            pl.BlockSpec(w.shape, const),
            pl.BlockSpec((tm, LANES), lambda b, i: (i, 0)),
            pl.BlockSpec((tm, LANES), lambda b, i: (i, 0)),
            pl.BlockSpec((1, LANES), const),
            pl.BlockSpec((1, LANES), const),
        ],
        out_specs=[row(C_WIDTH), row(C_KV_WIDTH), row(C_KV_WIDTH)],
        out_shape=[jax.ShapeDtypeStruct((batch, t, C_WIDTH), BF16),
                   jax.ShapeDtypeStruct((batch, t, C_KV_WIDTH), BF16),
                   jax.ShapeDtypeStruct((batch, t, C_KV_WIDTH), BF16)],
        compiler_params=_params("arbitrary", "arbitrary"),
        name="proj_c",
    )(x, mod, g, w, cos, sin, qn, kn)


def _mlstm_kernel(ql_ref, ktl_ref, vl_ref, oal_ref, gl_ref, qc_ref, ktc_ref, vc_ref, oac_ref, gc_ref, ng_ref,
                  hl_ref, hc_ref,
                  qs, kts, vs, gs, rowq, rowg, cols, stats, kvb, cst, mst, *, t_ctx, t_lat):
    L = BLOCK
    ncc = t_ctx // L
    ncl = t_lat // L
    nc = ncc + ncl

    qs[0:t_lat] = ql_ref[0]
    qs[t_lat:] = qc_ref[0]
    kts[:, 0:t_lat] = ktl_ref[0]
    kts[:, t_lat:] = ktc_ref[0]
    vs[0:t_lat] = vl_ref[0]
    vs[t_lat:] = vc_ref[0]
    for r in range(4):
        gs[r, 0:ncl] = gl_ref[0, r]
        gs[r, ncl:nc] = gc_ref[0, r]

    ri = lax.broadcasted_iota(jnp.int32, (L, L), 0)
    ci = lax.broadcasted_iota(jnp.int32, (L, L), 1)
    lower = ci <= ri
    upper = ci >= ri
    ones_blk = jnp.ones((L, L), BF16)

    def chunk(i):
        return pl.ds(pl.multiple_of(i * L, L), L)

    def cum(x, mat):
        hi, mid, lo = _split3(x)
        return _dot(hi, mat) + _dot(mid, mat) + _dot(lo, mat)

    li_f, lf_f, li_b, lf_b = (gs[r, 0:nc] for r in range(4))
    bcum_f = cum(lf_f, jnp.where(upper, 1.0, 0.0).astype(BF16))
    bsuf_b = cum(lf_b, jnp.where(lower, 1.0, 0.0).astype(BF16))
    bl_f = bcum_f[:, L - 1:L]
    bl_b = bsuf_b[:, 0:1]
    wl_f = bl_f - bcum_f + li_f
    wl_b = bl_b - bsuf_b + li_b
    mx_f = jnp.max(wl_f, axis=1, keepdims=True)
    mx_b = jnp.max(wl_b, axis=1, keepdims=True)
    rb_f = li_f - bcum_f
    rb_b = li_b - bsuf_b
    lane = lax.broadcasted_iota(jnp.int32, (nc, L), 1)
    pm_f, pm_b = rb_f, rb_b
    step = 1
    while step < L:
        pm_f = jnp.where(lane >= step, jnp.maximum(pm_f, pltpu.roll(pm_f, step, 1)), pm_f)
        pm_b = jnp.where(lane < L - step, jnp.maximum(pm_b, pltpu.roll(pm_b, L - step, 1)), pm_b)
        step *= 2
    for n, val in enumerate((bcum_f, bsuf_b, rb_f, rb_b, jnp.exp(wl_f - mx_f), jnp.exp(wl_b - mx_b), pm_f, pm_b)):
        rowq[n, 0:nc] = val
    for n, val in enumerate((bl_f, bl_b, mx_f, mx_b)):
        stats[n, 0:nc] = jnp.broadcast_to(val, (nc, L))

    def prep(i, _):
        kt = kts[:, chunk(i)].astype(F32)
        vaug = jnp.concatenate([vs[chunk(i), :], ones_blk], axis=1)
        for d in range(2):
            kvb[d, i] = _dot((kt * rowq[4 + d, pl.ds(i, 1), :]).astype(BF16), vaug)
        return 0

    lax.fori_loop(0, nc, prep, 0, unroll=2)

    def visit(d, j):
        if d == 0:
            return jnp.where(j < ncc, ncl + j, j - ncc)
        return nc - 1 - j

    for d in range(2):
        cst[d, visit(d, 0)] = jnp.zeros((L, 2 * L), F32)

    def scan_step(j, carry):
        new = []
        for d in range(2):
            m = carry[d]
            i = visit(d, j)
            i_next = jnp.where(j + 1 < nc, visit(d, j + 1), nc)
            mst[d, pl.ds(i, 1), :] = jnp.broadcast_to(m, (1, L))
            bl = stats[d, pl.ds(i, 1), 0:1]
            mx = stats[2 + d, pl.ds(i, 1), 0:1]
            m_new = jnp.maximum(bl + m, mx)
            cst[d, i_next] = jnp.exp(bl + m - m_new) * cst[d, i] + jnp.exp(mx - m_new) * kvb[d, i]
            new.append(m_new)
        return tuple(new)

    m0 = jnp.zeros((1, 1), F32)
    lax.fori_loop(0, nc, scan_step, (m0, m0))

    for d in range(2):
        g = jnp.maximum(mst[d, 0:nc], rowq[6 + d, 0:nc])
        rowg[d, 0:nc] = g
        rowg[2 + d, 0:nc] = jnp.exp(-rowq[d, 0:nc] - g)

    def to_cols(i, _):
        r8 = jnp.concatenate([rowg[n, pl.ds(i, 1), :] for n in range(4)] + [jnp.zeros((4, L), F32)], axis=0)
        cols[chunk(i), :] = jnp.concatenate([r8, jnp.zeros((L - 8, L), F32)], axis=0).T
        return 0

    lax.fori_loop(0, nc, to_cols, 0, unroll=2)

    ng = ng_ref[...]

    def emit(i, oa, out_ref, out_rows):
        q = qs[chunk(i), :]
        vaug = jnp.concatenate([vs[chunk(i), :], ones_blk], axis=1)
        qk = _dot(q, kts[:, chunk(i)])
        colblk = cols[chunk(i), :]
        h = None
        for d in range(2):
            g = jnp.broadcast_to(colblk[:, d:d + 1], (L, L))
            floor = colblk[:, 2 + d:3 + d]
            rb = rowq[2 + d, pl.ds(i, 1), :]
            mask = lower if d == 0 else upper
            dm = jnp.where(mask, jnp.exp(rb - g), 0.0)
            s = (qk * dm).astype(BF16)
            iw = jnp.exp(mst[d, pl.ds(i, 1), 0:1] - g)
            nd = jnp.concatenate([iw, iw], axis=1) * _dot(q, cst[d, i].astype(BF16)) + _dot(s, vaug)
            hd = nd[:, 0:L] / jnp.maximum(jnp.abs(nd[:, L:2 * L]), floor)
            h = hd if h is None else h + hd
        ms = jnp.mean(h * h, axis=-1, keepdims=True)
        y = (h * lax.rsqrt(ms + EPS) * ng) * jax.nn.sigmoid(oa)
        out_ref[0, out_rows, :] = y.astype(BF16)

    def emit_ctx(i, _):
        emit(i + ncl, oac_ref[0, chunk(i), :], hc_ref, chunk(i))
        return 0

    def emit_lat(i, _):
        emit(i, oal_ref[0, chunk(i), :], hl_ref, chunk(i))
        return 0

    lax.fori_loop(0, ncc, emit_ctx, 0, unroll=2)
    lax.fori_loop(0, ncl, emit_lat, 0, unroll=4)


def _mlstm(qa_l, kat_l, va_l, oa_l, gr_l, qa_c, kat_c, va_c, oa_c, gr_c, norm_g):
    batch, t_lat, _ = qa_l.shape
    t_ctx = qa_c.shape[1]
    t_all = t_ctx + t_lat
    nc = t_all // BLOCK
    nc_pad = -(-nc // 8) * 8
    head = lambda t: pl.BlockSpec((1, t, A_HEAD_DIM), lambda b, h: (b, 0, h))
    head_t = lambda t: pl.BlockSpec((1, A_HEAD_DIM, t), lambda b, h: (b, h, 0))
    gates = lambda t: pl.BlockSpec((1, GATE_ROWS, t // BLOCK, BLOCK), lambda b, h: (b, h, 0, 0))
    chunked = lambda g: g.reshape(batch, A_HEADS * GATE_ROWS, g.shape[2] // BLOCK, BLOCK)
    return pl.pallas_call(
        functools.partial(_mlstm_kernel, t_ctx=t_ctx, t_lat=t_lat),
        grid=(batch, A_HEADS),
        in_specs=[head(t_lat), head_t(t_lat), head(t_lat), head(t_lat), gates(t_lat),
                  head(t_ctx), head_t(t_ctx), head(t_ctx), head(t_ctx), gates(t_ctx),
                  pl.BlockSpec((1, A_HEAD_DIM), lambda b, h: (0, h))],
        out_specs=[head(t_lat), head(t_ctx)],
        out_shape=[jax.ShapeDtypeStruct((batch, t_lat, A_WIDTH), BF16),
                   jax.ShapeDtypeStruct((batch, t_ctx, A_WIDTH), BF16)],
        scratch_shapes=[
            pltpu.VMEM((t_all, A_HEAD_DIM), BF16),
            pltpu.VMEM((A_HEAD_DIM, t_all), BF16),
            pltpu.VMEM((t_all, A_HEAD_DIM), BF16),
            pltpu.VMEM((4, nc_pad, BLOCK), F32),
            pltpu.VMEM((8, nc_pad, BLOCK), F32),
            pltpu.VMEM((4, nc_pad, BLOCK), F32),
            pltpu.VMEM((t_all, LANES), F32),
            pltpu.VMEM((4, nc_pad, LANES), F32),
            pltpu.VMEM((2, nc, A_HEAD_DIM, 2 * A_HEAD_DIM), F32),
            pltpu.VMEM((2, nc + 1, A_HEAD_DIM, 2 * A_HEAD_DIM), F32),
            pltpu.VMEM((2, nc_pad, LANES), F32),
        ],
        compiler_params=_params("arbitrary", "arbitrary"),
        name="mlstm",
    )(qa_l, kat_l, va_l, oa_l, chunked(gr_l), qa_c, kat_c, va_c, oa_c, chunked(gr_c),
      norm_g.reshape(1, A_WIDTH))


def _swa_kernel(sink_ref, q_ref, *refs, t_lat, has_window, blocks):
    if has_window:
        bias_ref, k_ref, v_ref, kx_ref, vx_ref, o_ref, s_scr = refs
    else:
        kx_ref, vx_ref, o_ref, s_scr = refs
    L = BLOCK
    nb = t_lat // L
    group = B_HEADS // B_KV_HEADS
    lane = lax.broadcasted_iota(jnp.int32, (L, LANES), 1)
    lo = lane < B_HEAD_DIM
    zero = jnp.zeros((L, LANES), BF16)
    def window(u):
        qblk = pl.program_id(1) * blocks + u
        start = pl.multiple_of(jnp.clip((qblk - 1) * L, 0, t_lat - 3 * L), L)
        return qblk, pl.ds(start, 3 * L)

    def logits(c, u, kvh):
        sl = slice(kvh * LANES, (kvh + 1) * LANES)
        q = q_ref[0, u * L:(u + 1) * L, :]
        qa = q[:, (2 * kvh) * LANES:(2 * kvh + 1) * LANES]
        qb = q[:, (2 * kvh + 1) * LANES:(2 * kvh + 2) * LANES]
        q4 = jnp.concatenate([jnp.where(lo, qa, zero), jnp.where(lo, zero, qa),
                              jnp.where(lo, qb, zero), jnp.where(lo, zero, qb)], axis=0)
        sink = jnp.concatenate([jnp.full((L, 1), sink_ref[kvh * group + g] * LOG2_E, F32)
                                for g in range(group)], axis=0)
        if has_window:
            qblk, win = window(u)
            bias = bias_ref[jnp.where(qblk == 0, 0, jnp.where(qblk == nb - 1, 2, 1))]
            keys = jnp.concatenate([k_ref[0, win, sl], kx_ref[0, :, sl]], axis=0)
            s = _dot_nt(q4, keys)
            s = (s.reshape(group, L, s.shape[1]) + bias[None]).reshape(s.shape)
        else:
            s = _dot_nt(q4, kx_ref[0, :, sl])
        s_scr[c] = s
        return jnp.maximum(jnp.max(s, axis=1, keepdims=True), sink), sink

    def attend(c, u, kvh, m, sink):
        sl = slice(kvh * LANES, (kvh + 1) * LANES)
        if has_window:
            vals = jnp.concatenate([v_ref[0, window(u)[1], sl], vx_ref[0, :, sl]], axis=0)
        else:
            vals = vx_ref[0, :, sl]
        e = jnp.exp2(s_scr[c] - m)
        denom = jnp.sum(e, axis=1, keepdims=True) + jnp.exp2(sink - m)
        o4 = _dot(e.astype(BF16), vals) / denom
        rows = slice(u * L, (u + 1) * L)
        o_ref[0, rows, (2 * kvh) * LANES:(2 * kvh + 1) * LANES] = (
            jnp.where(lo, o4[0:L], o4[L:2 * L]).astype(BF16))
        o_ref[0, rows, (2 * kvh + 1) * LANES:(2 * kvh + 2) * LANES] = (
            jnp.where(lo, o4[2 * L:3 * L], o4[3 * L:4 * L]).astype(BF16))

    chains = [(u, kvh) for u in range(blocks) for kvh in range(B_KV_HEADS)]
    stats = [logits(0, *chains[0])]
    for c, (u, kvh) in enumerate(chains):
        if c + 1 < len(chains):
            stats.append(logits(c + 1, *chains[c + 1]))
        attend(c, u, kvh, *stats[c])


def _band_bias(t_ctx):
    L = BLOCK
    t = jnp.arange(L)[:, None]
    j = jnp.arange(3 * L)[None, :]
    tables = []
    for rel in (0, -L, -2 * L):
        ok = jnp.abs(j - t + rel) <= WINDOW
        tables.append(jnp.concatenate([jnp.where(ok, 0.0, -jnp.inf), jnp.zeros((L, t_ctx))], axis=1))
    return jnp.stack(tables).astype(F32)


def _swa(sink, q, k, v, k_ctx, v_ctx, has_window):
    batch, t, _ = q.shape
    t_ctx = k_ctx.shape[1]
    nb = t // BLOCK
    blocks = min(SWA_BLOCKS, nb)
    assert nb % blocks == 0 and (nb >= 3 or not has_window)
    kvw = B_KV_HEADS * LANES
    cur = lambda b, i: (b, i, 0)
    whole = lambda b, i: (b, 0, 0)
    ctx_spec = pl.BlockSpec((1, t_ctx, kvw), whole)
    in_specs = [pl.BlockSpec(memory_space=pltpu.SMEM), pl.BlockSpec((1, blocks * BLOCK, B_WIDTH), cur)]
    args = [sink, q]
    if has_window:
        bias = _band_bias(t_ctx)
        in_specs += [pl.BlockSpec(bias.shape, lambda b, i: (0, 0, 0)),
                     pl.BlockSpec((1, t, kvw), whole), pl.BlockSpec((1, t, kvw), whole)]
        args += [bias, k, v]
    in_specs += [ctx_spec, ctx_spec]
    args += [k_ctx, v_ctx]
    return pl.pallas_call(
        functools.partial(_swa_kernel, t_lat=t, has_window=has_window, blocks=blocks),
        grid=(batch, nb // blocks),
        in_specs=in_specs,
        out_specs=pl.BlockSpec((1, blocks * BLOCK, B_WIDTH), cur),
        scratch_shapes=[pltpu.VMEM((blocks * B_KV_HEADS, (B_HEADS // B_KV_HEADS) * BLOCK,
                                    (3 * BLOCK if has_window else 0) + t_ctx), F32)],
        out_shape=jax.ShapeDtypeStruct((batch, t, B_WIDTH), BF16),
        compiler_params=_params("arbitrary", "arbitrary"),
        name="swa" if has_window else "swa_ctx",
    )(*args)


def _flash_kernel(q_ref, k_ref, v_ref, o_ref, s0_scr, s1_scr, m0_scr, m1_scr, vaug_scr, *, chunks, tq, unroll):
    group = C_HEADS // C_KV_HEADS
    nq = q_ref.shape[1] // tq
    vaug_scr[:, 0:LANES] = v_ref[0]
    vaug_scr[:, LANES:] = jnp.ones((v_ref.shape[1], LANES), BF16)

    def q_rows(t):
        return pl.ds(pl.multiple_of(t * tq, tq), tq)

    def load_q(t):
        q = q_ref[0, q_rows(t), :]
        return jnp.concatenate([q[:, g * LANES:(g + 1) * LANES] for g in range(group)], axis=0)

    def stage_a(q4, s_ref, m, start, size):
        s = _dot_nt(q4, k_ref[0, start:start + size, :])
        s_ref[:, start:start + size] = s
        cm = jnp.max(s, axis=1, keepdims=True)
        return cm if m is None else jnp.maximum(m, cm)

    def stage_b(s_ref, m, acc, start, size):
        p = jnp.exp2(s_ref[:, start:start + size] - m)
        pv = _dot(p.astype(BF16), vaug_scr[start:start + size, :])
        return pv if acc is None else acc + pv

    def step(t, cur, nxt):
        q4 = load_q(jnp.minimum(t + 1, nq - 1))
        m_cur = cur[1][...]
        m_next, acc = None, None
        for start, size in chunks:
            m_next = stage_a(q4, nxt[0], m_next, start, size)
            acc = stage_b(cur[0], m_cur, acc, start, size)
        nxt[1][...] = m_next
        out = acc[:, 0:LANES] / acc[:, LANES:2 * LANES]
        for g in range(group):
            o_ref[0, q_rows(t), g * LANES:(g + 1) * LANES] = out[g * tq:(g + 1) * tq].astype(BF16)

    bufs = ((s0_scr, m0_scr), (s1_scr, m1_scr))
    q4 = load_q(0)
    m = None
    for start, size in chunks:
        m = stage_a(q4, s0_scr, m, start, size)
    m0_scr[...] = m

    def body(u, _):
        for k in range(unroll):
            step(unroll * u + k, bufs[k % 2], bufs[(k + 1) % 2])
        return 0

    lax.fori_loop(0, nq // unroll, body, 0)


def _key_chunks(total, target):
    chunks = []
    start = 0
    while start < total:
        size = min(target, total - start)
        chunks.append((start, size))
        start += size
    return tuple(chunks)


def _flash(q, k, v, t_ctx):
    batch, t, _ = q.shape
    tk_all = k.shape[1]
    tq = _row_tile(t, FLASH_ROWS)
    parts = FLASH_PARTS if t % (FLASH_PARTS * 2 * tq) == 0 else 1
    tp = t // parts
    unroll = FLASH_UNROLL if (tp // tq) % FLASH_UNROLL == 0 else 2
    assert unroll % 2 == 0 and (tp // tq) % unroll == 0
    group = C_HEADS // C_KV_HEADS
    gw = group * C_HEAD_DIM
    chunks = (_key_chunks(t_ctx, FLASH_KEYS)
              + tuple((t_ctx + a, n) for a, n in _key_chunks(tk_all - t_ctx, FLASH_KEYS)))
    return pl.pallas_call(
        functools.partial(_flash_kernel, chunks=chunks, tq=tq, unroll=unroll),
        grid=(batch, C_KV_HEADS, parts),
        in_specs=[pl.BlockSpec((1, tp, gw), lambda b, h, p: (b, p, h)),
                  pl.BlockSpec((1, tk_all, C_HEAD_DIM), lambda b, h, p: (b, 0, h)),
                  pl.BlockSpec((1, tk_all, C_HEAD_DIM), lambda b, h, p: (b, 0, h))],
        out_specs=pl.BlockSpec((1, tp, gw), lambda b, h, p: (b, p, h)),
        out_shape=jax.ShapeDtypeStruct((batch, t, C_WIDTH), BF16),
        scratch_shapes=[pltpu.VMEM((group * tq, tk_all), F32), pltpu.VMEM((group * tq, tk_all), F32),
                        pltpu.VMEM((group * tq, 1), F32), pltpu.VMEM((group * tq, 1), F32),
                        pltpu.VMEM((tk_all, 2 * C_HEAD_DIM), BF16)],
        compiler_params=_params("arbitrary", "arbitrary", "arbitrary"),
        name="flash_c",
    )(q, k, v)


def _out_mlp_kernel(*refs, n_mix, ff_chunk):
    x_ref, mod_ref, g_ref = refs[0:3]
    mix_refs = refs[3:3 + n_mix]
    wo_refs = refs[3 + n_mix:3 + 2 * n_mix]
    w1_ref, w2_ref, o_ref = refs[3 + 2 * n_mix:]
    mod = mod_ref[0]
    y = None
    for a_ref, w_ref in zip(mix_refs, wo_refs):
        part = _dot(a_ref[0], w_ref[...])
        y = part if y is None else y + part
    x1 = x_ref[0] + mod[2:3] * y
    h = _modulated_norm(x1, g_ref[...], mod[3:4], mod[4:5]).astype(BF16)
    d_ff = w1_ref.shape[1]
    acc = None
    for c in range(d_ff // ff_chunk):
        a = _dot(h, w1_ref[:, c * ff_chunk:(c + 1) * ff_chunk])
        a = jnp.square(jnp.maximum(a, 0.0)).astype(BF16)
        part = _dot(a, w2_ref[c * ff_chunk:(c + 1) * ff_chunk, :])
        acc = part if acc is None else acc + part
    o_ref[0] = x1 + mod[5:6] * acc


def _out_mlp(x, mod, g, mixes, w_outs, w1, w2):
    batch, t, d = x.shape
    tm = _row_tile(t, MLP_ROWS)
    mod_map = (lambda b, i: (b, 0, 0)) if mod.shape[0] == batch else (lambda b, i: (0, 0, 0))
    const = lambda b, i: (0, 0)
    resident = lambda a: pl.BlockSpec(a.shape, const, pipeline_mode=pl.Buffered(1))
    row = lambda width: pl.BlockSpec((1, tm, width), lambda b, i: (b, i, 0))
    return pl.pallas_call(
        functools.partial(_out_mlp_kernel, n_mix=len(mixes), ff_chunk=1024),
        grid=(batch, t // tm),
        in_specs=([row(d), pl.BlockSpec((1, MOD_ROWS, d), mod_map), pl.BlockSpec((1, d), const)]
                  + [row(a.shape[2]) for a in mixes] + [resident(w) for w in w_outs]
                  + [resident(w1), resident(w2)]),
        out_specs=row(d),
        out_shape=jax.ShapeDtypeStruct((batch, t, d), F32),
        compiler_params=_params("arbitrary", "arbitrary"),
        name="out_mlp",
    )(x, mod, g, *mixes, *w_outs, w1, w2)


def kernel(x, c, ctx, c_ctx, ada_w, ada_b, norm1_g, norm2_g, ab_w_in, ab_gate_b, mlstm_norm_g, swa_q_norm_g,
           swa_k_norm_g, swa_sink, ab_w_out, c_w_in, c_q_norm_g, c_k_norm_g, c_w_out, mlp_w1, mlp_w2):
    depth = ada_w.shape[0]
    batch, _, d = x.shape
    mods = _mods(c, c_ctx, ada_w, ada_b)
    for layer in range(depth):
        last = layer == depth - 1
        mod_l = mods[layer, :batch]
        mod_c = mods[layer, batch:batch + 1]
        g1 = norm1_g[layer].reshape(1, d)
        g2 = norm2_g[layer].reshape(1, d)
        w1 = mlp_w1[layer].astype(BF16)
        w2 = mlp_w2[layer].astype(BF16)
        j = layer // 2
        if layer % 2 == 0:
            w, wt, gb = _ab_weights(ab_w_in[j], ab_gate_b[j])
            qn = jnp.tile(swa_q_norm_g[j], LANES // B_HEAD_DIM).reshape(1, LANES)
            kn = jnp.tile(swa_k_norm_g[j], LANES // B_HEAD_DIM).reshape(1, LANES)
            qa_l, kat_l, va_l, oa_l, qb_l, kb_l, vb_l, gr_l = _proj_ab(x, mod_l, g1, w, wt, gb, qn, kn, True)
            qa_c, kat_c, va_c, oa_c, qb_c, kb_c, vb_c, gr_c = _proj_ab(ctx, mod_c, g1, w, wt, gb, qn, kn, False)
            ha_l, ha_c = _mlstm(qa_l, kat_l, va_l, oa_l, gr_l, qa_c, kat_c, va_c, oa_c, gr_c, mlstm_norm_g[j])
            ob_l = _swa(swa_sink[j], qb_l, kb_l, vb_l, kb_c, vb_c, True)
            w_out = ab_w_out[j].astype(BF16)
            w_outs = [w_out[:A_WIDTH], w_out[A_WIDTH:]]
            x = _out_mlp(x, mod_l, g2, [ha_l, ob_l], w_outs, w1, w2)
            if not last:
                ob_c = _swa(swa_sink[j], qb_c, kb_c, vb_c, kb_c, vb_c, False)
                ctx = _out_mlp(ctx, mod_c, g2, [ha_c, ob_c], w_outs, w1, w2)
        else:
            w = c_w_in[j].astype(BF16)
            qn = c_q_norm_g[j].reshape(1, LANES)
            kn = c_k_norm_g[j].reshape(1, LANES)
            q_l, k_l, v_l = _proj_c(x, mod_l, g1, w, qn, kn, True)
            q_c, k_c, v_c = _proj_c(ctx, mod_c, g1, w, qn, kn, False)
            k_all = jnp.concatenate([k_c, k_l], axis=1)
            v_all = jnp.concatenate([v_c, v_l], axis=1)
            w_out = c_w_out[j].astype(BF16)
            o_l = _flash(q_l, k_all, v_all, k_c.shape[1])
            x = _out_mlp(x, mod_l, g2, [o_l], [w_out], w1, w2)
            if not last:
                o_c = _flash(q_c, k_c, v_c, 0)
                ctx = _out_mlp(ctx, mod_c, g2, [o_c], [w_out], w1, w2)
    return x
```

```python
import functools

import numpy as np
import jax
import jax.numpy as jnp
from jax import lax
from jax.experimental import pallas as pl
from jax.experimental.pallas import tpu as pltpu

F32 = jnp.float32
BF16 = jnp.bfloat16

GRID_W = 64
BLOCK = 128
WINDOW = 128
ROPE_BASE = 10000.0
EPS = 1e-6
LOG2_E = 1.4426950408889634
PROJ_ROWS = 512
MLP_ROWS = 512
SWA_BLOCKS = 4
FLASH_KEYS = 512
FLASH_UNROLL = 4
FLASH_PARTS = 1
FLASH_ROWS = 128
N_MOD = 6
MOD_ROWS = 8
LANES = 128
MXU_WIDTH = 256

A_HEADS = 4
A_HEAD_DIM = 128
A_WIDTH = A_HEADS * A_HEAD_DIM
A_GATES = 4 * A_HEADS
GATE_ROWS = 8
B_HEADS = 8
B_KV_HEADS = 2
B_HEAD_DIM = 64
B_WIDTH = B_HEADS * B_HEAD_DIM
B_KV_WIDTH = B_KV_HEADS * B_HEAD_DIM
C_HEADS = 8
C_KV_HEADS = 2
C_HEAD_DIM = 128
C_WIDTH = C_HEADS * C_HEAD_DIM
C_KV_WIDTH = C_KV_HEADS * C_HEAD_DIM

VMEM_LIMIT = 56 * 1024 * 1024

NT_DIMS = (((1,), (1,)), ((), ()))
TN_DIMS = (((0,), (0,)), ((), ()))


def _dot(a, b):
    return jnp.dot(a, b, preferred_element_type=F32)


def _dot_nt(a, b):
    return lax.dot_general(a, b, NT_DIMS, preferred_element_type=F32)


def _dot_tn(a, b):
    return lax.dot_general(a, b, TN_DIMS, preferred_element_type=F32)


def _params(*sem):
    return pltpu.CompilerParams(dimension_semantics=sem, vmem_limit_bytes=VMEM_LIMIT)


def _mods_kernel(c_ref, w_ref, b_ref, o_ref):
    cf = c_ref[...]
    s = (cf * jax.nn.sigmoid(cf)).astype(BF16)
    o_ref[0] = _dot(s, w_ref[0].astype(BF16)) + b_ref[0]


def _mods(c, c_ctx, ada_w, ada_b):
    depth, d, _ = ada_w.shape
    batch = c.shape[0]
    rows = -(-(batch + 1) // 8) * 8
    cc = jnp.zeros((rows, d), F32).at[:batch].set(c).at[batch].set(c_ctx)
    out = pl.pallas_call(
        _mods_kernel,
        grid=(depth, N_MOD),
        in_specs=[
            pl.BlockSpec((rows, d), lambda l, j: (0, 0)),
            pl.BlockSpec((1, d, d), lambda l, j: (l, 0, j)),
            pl.BlockSpec((1, 1, d), lambda l, j: (l, 0, j)),
        ],
        out_specs=pl.BlockSpec((1, rows, d), lambda l, j: (l, 0, j)),
        out_shape=jax.ShapeDtypeStruct((depth, rows, N_MOD * d), F32),
        compiler_params=_params("arbitrary", "arbitrary"),
        name="ada_mods",
    )(cc, ada_w, ada_b.reshape(depth, 1, N_MOD * d))
    out = out.reshape(depth, rows, N_MOD, d)
    return jnp.pad(out, ((0, 0), (0, 0), (0, MOD_ROWS - N_MOD), (0, 0)))


def _modulated_norm(x, g, shift, scale):
    ms = jnp.mean(x * x, axis=-1, keepdims=True)
    return (x * lax.rsqrt(ms + EPS) * g) * (1.0 + scale) + shift


def _split3(a):
    hi = a.astype(BF16)
    r = a - hi.astype(F32)
    mid = r.astype(BF16)
    lo = (r - mid.astype(F32)).astype(BF16)
    return hi, mid, lo


def _head_norm_rope(xh, gain, cos, sin, group, out_scale):
    sq = xh * xh
    half = group // 2
    if group == LANES:
        ssq = jnp.sum(sq, axis=-1, keepdims=True)
    else:
        assert 2 * group == LANES
        lane = lax.broadcasted_iota(jnp.int32, xh.shape, 1)
        left = lane < group
        ssq = jnp.where(left, jnp.sum(jnp.where(left, sq, 0.0), axis=-1, keepdims=True),
                        jnp.sum(jnp.where(left, 0.0, sq), axis=-1, keepdims=True))
    xn = xh * lax.rsqrt(ssq * (1.0 / group) + EPS) * gain
    if group == LANES:
        rot = pltpu.roll(xn, half, 1)
    else:
        first = (lane & (group - 1)) < half
        rot = jnp.where(first, pltpu.roll(xn, LANES - half, 1), pltpu.roll(xn, half, 1))
    y = xn * cos + rot * sin
    if out_scale != 1.0:
        y = y * out_scale
    return y


def _rope_tables(n_tokens, head_dim, use_rope):
    reps = LANES // head_dim
    if not use_rope:
        return np.ones((n_tokens, LANES), np.float32), np.zeros((n_tokens, LANES), np.float32)
    rows = n_tokens // GRID_W
    row = np.repeat(np.arange(rows, dtype=np.float64), GRID_W)
    col = np.tile(np.arange(GRID_W, dtype=np.float64), rows)
    pairs = head_dim // 4
    inv_freq = ROPE_BASE ** (-np.arange(pairs, dtype=np.float64) / pairs)
    ang = np.concatenate([row[:, None] * inv_freq, col[:, None] * inv_freq], axis=-1)
    cos, sin = np.cos(ang), np.sin(ang)
    cos_full = np.concatenate([cos, cos], axis=-1)
    sin_signed = np.concatenate([-sin, sin], axis=-1)
    return (np.tile(cos_full, (1, reps)).astype(np.float32), np.tile(sin_signed, (1, reps)).astype(np.float32))


def _row_tile(t, target):
    tm = min(t, target)
    assert t % tm == 0
    return tm


def _proj_ab_kernel(x_ref, mod_ref, g_ref, w_ref, wt_ref, gb_ref, cos_ref, sin_ref, qn_ref, kn_ref,
                    qa_ref, kat_ref, va_ref, oa_ref, qb_ref, kb_ref, vb_ref, gr_ref):
    mod = mod_ref[0]
    h = _modulated_norm(x_ref[0], g_ref[...], mod[0:1], mod[1:2]).astype(BF16)

    def mm(lo, n):
        return _dot(h, w_ref[:, lo:lo + n])

    qa_ref[0] = mm(0, A_WIDTH).astype(BF16)
    va_ref[0] = mm(A_WIDTH, A_WIDTH).astype(BF16)
    oa_ref[0] = mm(2 * A_WIDTH, A_WIDTH)
    cos, sin = cos_ref[...], sin_ref[...]
    base = 3 * A_WIDTH
    for j in range(B_WIDTH // LANES):
        y = _head_norm_rope(mm(base + LANES * j, LANES), qn_ref[...], cos, sin, B_HEAD_DIM,
                            B_HEAD_DIM ** -0.5 * LOG2_E)
        qb_ref[0, :, LANES * j:LANES * (j + 1)] = y.astype(BF16)
    base += B_WIDTH
    for j in range(B_KV_HEADS):
        y = _head_norm_rope(mm(base + LANES * j, LANES), kn_ref[...], cos, sin, B_HEAD_DIM, 1.0)
        kb_ref[0, :, LANES * j:LANES * (j + 1)] = y.astype(BF16)
    base += B_KV_HEADS * LANES
    vb_ref[0] = mm(base, B_KV_HEADS * LANES).astype(BF16)
    tr = _dot_nt(wt_ref[...], h)
    kat_ref[0] = (tr[0:A_WIDTH] * A_HEAD_DIM ** -0.5).astype(BF16)
    gt = tr[A_WIDTH:] + gb_ref[...]
    typ = lax.broadcasted_iota(jnp.int32, gt.shape, 0) & (GATE_ROWS - 1)
    log_sig = jnp.minimum(gt, 0.0) - jnp.log1p(jnp.exp(-jnp.abs(gt)))
    gr_ref[0] = jnp.where((typ == 1) | (typ == 3), log_sig, gt)


def _proj_ab(x, mod, g, w, wg, gb, qn, kn, use_rope):
    batch, t, d = x.shape
    tm = _row_tile(t, PROJ_ROWS)
    cos, sin = _rope_tables(t, B_HEAD_DIM, use_rope)
    mod_map = (lambda b, i: (b, 0, 0)) if mod.shape[0] == batch else (lambda b, i: (0, 0, 0))
    const = lambda b, i: (0, 0)
    row = lambda width: pl.BlockSpec((1, tm, width), lambda b, i: (b, i, 0))
    kvw = B_KV_HEADS * LANES
    out_shape = [
        jax.ShapeDtypeStruct((batch, t, A_WIDTH), BF16),
        jax.ShapeDtypeStruct((batch, A_WIDTH, t), BF16),
        jax.ShapeDtypeStruct((batch, t, A_WIDTH), BF16),
        jax.ShapeDtypeStruct((batch, t, A_WIDTH), F32),
        jax.ShapeDtypeStruct((batch, t, B_WIDTH), BF16),
        jax.ShapeDtypeStruct((batch, t, kvw), BF16),
        jax.ShapeDtypeStruct((batch, t, kvw), BF16),
        jax.ShapeDtypeStruct((batch, A_HEADS * GATE_ROWS, t), F32),
    ]
    col = lambda height: pl.BlockSpec((1, height, tm), lambda b, i: (b, 0, i))
    out_specs = [row(A_WIDTH), col(A_WIDTH), row(A_WIDTH), row(A_WIDTH), row(B_WIDTH), row(kvw), row(kvw),
                 col(A_HEADS * GATE_ROWS)]
    return pl.pallas_call(
        _proj_ab_kernel,
        grid=(batch, t // tm),
        in_specs=[
            row(d),
            pl.BlockSpec((1, MOD_ROWS, d), mod_map),
            pl.BlockSpec((1, d), const),
            pl.BlockSpec(w.shape, const),
            pl.BlockSpec(wg.shape, const),
            pl.BlockSpec(gb.shape, const),
            pl.BlockSpec((tm, LANES), lambda b, i: (i, 0)),
            pl.BlockSpec((tm, LANES), lambda b, i: (i, 0)),
            pl.BlockSpec((1, LANES), const),
            pl.BlockSpec((1, LANES), const),
        ],
        out_specs=out_specs,
        out_shape=out_shape,
        compiler_params=_params("arbitrary", "arbitrary"),
        name="proj_ab",
    )(x, mod, g, w, wg, gb, cos, sin, qn, kn)


def _ab_weights(w_in, gate_b):
    bounds = [A_WIDTH * 4, A_WIDTH * 4 + A_GATES, A_WIDTH * 4 + A_GATES + B_WIDTH,
              A_WIDTH * 4 + A_GATES + B_WIDTH + B_KV_WIDTH]
    wa, wgate, wqb, wkb, wvb = jnp.split(w_in, bounds, axis=1)
    wqa, wka, wva_oa = wa[:, :A_WIDTH], wa[:, A_WIDTH:2 * A_WIDTH], wa[:, 2 * A_WIDTH:]

    def dup(wk):
        parts = []
        for hh in range(B_KV_HEADS):
            blk = wk[:, hh * B_HEAD_DIM:(hh + 1) * B_HEAD_DIM]
            parts += [blk] * (LANES // B_HEAD_DIM)
        return jnp.concatenate(parts, axis=1)

    w = jnp.concatenate([wqa, wva_oa, wqb, dup(wkb), dup(wvb)], axis=1).astype(BF16)
    d = w_in.shape[0]
    wg = wgate.reshape(d, 4, A_HEADS).transpose(2, 1, 0)
    wg = jnp.pad(wg, ((0, 0), (0, GATE_ROWS - 4), (0, 0))).reshape(A_HEADS * GATE_ROWS, d)
    wt = jnp.concatenate([wka.T, wg], axis=0).astype(BF16)
    gb = gate_b.reshape(4, A_HEADS).T
    gb = jnp.pad(gb, ((0, 0), (0, GATE_ROWS - 4))).reshape(A_HEADS * GATE_ROWS, 1).astype(F32)
    return w, wt, gb


def _proj_c_kernel(x_ref, mod_ref, g_ref, w_ref, cos_ref, sin_ref, qn_ref, kn_ref, q_ref, k_ref, v_ref):
    mod = mod_ref[0]
    h = _modulated_norm(x_ref[0], g_ref[...], mod[0:1], mod[1:2]).astype(BF16)

    def mm(lo, n):
        return _dot(h, w_ref[:, lo:lo + n])

    cos, sin = cos_ref[...], sin_ref[...]
    q_scale = C_HEAD_DIM ** -0.5 * LOG2_E
    for j in range(C_HEADS):
        y = _head_norm_rope(mm(LANES * j, LANES), qn_ref[...], cos, sin, C_HEAD_DIM, q_scale)
        q_ref[0, :, LANES * j:LANES * (j + 1)] = y.astype(BF16)
    for j in range(C_KV_HEADS):
        y = _head_norm_rope(mm(C_WIDTH + LANES * j, LANES), kn_ref[...], cos, sin, C_HEAD_DIM, 1.0)
        k_ref[0, :, LANES * j:LANES * (j + 1)] = y.astype(BF16)
    v_ref[0] = mm(C_WIDTH + C_KV_WIDTH, C_KV_WIDTH).astype(BF16)


def _proj_c(x, mod, g, w, qn, kn, use_rope):
    batch, t, d = x.shape
    tm = _row_tile(t, PROJ_ROWS)
    cos, sin = _rope_tables(t, C_HEAD_DIM, use_rope)
    mod_map = (lambda b, i: (b, 0, 0)) if mod.shape[0] == batch else (lambda b, i: (0, 0, 0))
    const = lambda b, i: (0, 0)
    row = lambda width: pl.BlockSpec((1, tm, width), lambda b, i: (b, i, 0))
    return pl.pallas_call(
        _proj_c_kernel,
        grid=(batch, t // tm),
        in_specs=[
            row(d),
            pl.BlockSpec((1, MOD_ROWS, d), mod_map),
            pl.BlockSpec((1, d), const),
            pl.BlockSpec(w.shape, const),
            pl.BlockSpec((tm, LANES), lambda b, i: (i, 0)),
            pl.BlockSpec((tm, LANES), lambda b, i: (i, 0)),
            pl.BlockSpec((1, LANES), const),
            pl.BlockSpec((1, LANES), const),
        ],
        out_specs=[row(C_WIDTH), row(C_KV_WIDTH), row(C_KV_WIDTH)],
        out_shape=[jax.ShapeDtypeStruct((batch, t, C_WIDTH), BF16),
                   jax.ShapeDtypeStruct((batch, t, C_KV_WIDTH), BF16),
                   jax.ShapeDtypeStruct((batch, t, C_KV_WIDTH), BF16)],
        compiler_params=_params("arbitrary", "arbitrary"),
        name="proj_c",
    )(x, mod, g, w, cos, sin, qn, kn)


def _mlstm_kernel(ql_ref, ktl_ref, vl_ref, oal_ref, gl_ref, qc_ref, ktc_ref, vc_ref, oac_ref, gc_ref, ng_ref,
                  hl_ref, hc_ref,
                  qs, kts, vs, gs, rowq, rowg, cols, stats, kvb, cst, mst, *, t_ctx, t_lat):
    L = BLOCK
    ncc = t_ctx // L
    ncl = t_lat // L
    nc = ncc + ncl

    qs[0:t_lat] = ql_ref[0]
    qs[t_lat:] = qc_ref[0]
    kts[:, 0:t_lat] = ktl_ref[0]
    kts[:, t_lat:] = ktc_ref[0]
    vs[0:t_lat] = vl_ref[0]
    vs[t_lat:] = vc_ref[0]
    for r in range(4):
        gs[r, 0:ncl] = gl_ref[0, r]
        gs[r, ncl:nc] = gc_ref[0, r]

    ri = lax.broadcasted_iota(jnp.int32, (L, L), 0)
    ci = lax.broadcasted_iota(jnp.int32, (L, L), 1)
    lower = ci <= ri
    upper = ci >= ri
    ones_blk = jnp.ones((L, L), BF16)

    def chunk(i):
        return pl.ds(pl.multiple_of(i * L, L), L)

    def cum(x, mat):
        hi, mid, lo = _split3(x)
        return _dot(hi, mat) + _dot(mid, mat) + _dot(lo, mat)

    li_f, lf_f, li_b, lf_b = (gs[r, 0:nc] for r in range(4))
    bcum_f = cum(lf_f, jnp.where(upper, 1.0, 0.0).astype(BF16))
    bsuf_b = cum(lf_b, jnp.where(lower, 1.0, 0.0).astype(BF16))
    bl_f = bcum_f[:, L - 1:L]
    bl_b = bsuf_b[:, 0:1]
    wl_f = bl_f - bcum_f + li_f
    wl_b = bl_b - bsuf_b + li_b
    mx_f = jnp.max(wl_f, axis=1, keepdims=True)
    mx_b = jnp.max(wl_b, axis=1, keepdims=True)
    rb_f = li_f - bcum_f
    rb_b = li_b - bsuf_b
    lane = lax.broadcasted_iota(jnp.int32, (nc, L), 1)
    pm_f, pm_b = rb_f, rb_b
    step = 1
    while step < L:
        pm_f = jnp.where(lane >= step, jnp.maximum(pm_f, pltpu.roll(pm_f, step, 1)), pm_f)
        pm_b = jnp.where(lane < L - step, jnp.maximum(pm_b, pltpu.roll(pm_b, L - step, 1)), pm_b)
        step *= 2
    for n, val in enumerate((bcum_f, bsuf_b, rb_f, rb_b, jnp.exp(wl_f - mx_f), jnp.exp(wl_b - mx_b), pm_f, pm_b)):
        rowq[n, 0:nc] = val
    for n, val in enumerate((bl_f, bl_b, mx_f, mx_b)):
        stats[n, 0:nc] = jnp.broadcast_to(val, (nc, L))

    def prep(i, _):
        kt = kts[:, chunk(i)].astype(F32)
        vaug = jnp.concatenate([vs[chunk(i), :], ones_blk], axis=1)
        for d in range(2):
            kvb[d, i] = _dot((kt * rowq[4 + d, pl.ds(i, 1), :]).astype(BF16), vaug)
        return 0

    lax.fori_loop(0, nc, prep, 0, unroll=2)

    def visit(d, j):
        if d == 0:
            return jnp.where(j < ncc, ncl + j, j - ncc)
        return nc - 1 - j

    for d in range(2):
        cst[d, visit(d, 0)] = jnp.zeros((L, 2 * L), F32)

    def scan_step(j, carry):
        new = []
        for d in range(2):
            m = carry[d]
            i = visit(d, j)
            i_next = jnp.where(j + 1 < nc, visit(d, j + 1), nc)
            mst[d, pl.ds(i, 1), :] = jnp.broadcast_to(m, (1, L))
            bl = stats[d, pl.ds(i, 1), 0:1]
            mx = stats[2 + d, pl.ds(i, 1), 0:1]
            m_new = jnp.maximum(bl + m, mx)
            cst[d, i_next] = jnp.exp(bl + m - m_new) * cst[d, i] + jnp.exp(mx - m_new) * kvb[d, i]
            new.append(m_new)
        return tuple(new)

    m0 = jnp.zeros((1, 1), F32)
    lax.fori_loop(0, nc, scan_step, (m0, m0))

    for d in range(2):
        g = jnp.maximum(mst[d, 0:nc], rowq[6 + d, 0:nc])
        rowg[d, 0:nc] = g
        rowg[2 + d, 0:nc] = jnp.exp(-rowq[d, 0:nc] - g)

    def to_cols(i, _):
        r8 = jnp.concatenate([rowg[n, pl.ds(i, 1), :] for n in range(4)] + [jnp.zeros((4, L), F32)], axis=0)
        cols[chunk(i), :] = jnp.concatenate([r8, jnp.zeros((L - 8, L), F32)], axis=0).T
        return 0

    lax.fori_loop(0, nc, to_cols, 0, unroll=2)

    ng = ng_ref[...]

    def emit(i, oa, out_ref, out_rows):
        q = qs[chunk(i), :]
        vaug = jnp.concatenate([vs[chunk(i), :], ones_blk], axis=1)
        qk = _dot(q, kts[:, chunk(i)])
        colblk = cols[chunk(i), :]
        h = None
        for d in range(2):
            g = jnp.broadcast_to(colblk[:, d:d + 1], (L, L))
            floor = colblk[:, 2 + d:3 + d]
            rb = rowq[2 + d, pl.ds(i, 1), :]
            mask = lower if d == 0 else upper
            dm = jnp.where(mask, jnp.exp(rb - g), 0.0)
            s = (qk * dm).astype(BF16)
            iw = jnp.exp(mst[d, pl.ds(i, 1), 0:1] - g)
            nd = jnp.concatenate([iw, iw], axis=1) * _dot(q, cst[d, i].astype(BF16)) + _dot(s, vaug)
            hd = nd[:, 0:L] / jnp.maximum(jnp.abs(nd[:, L:2 * L]), floor)
            h = hd if h is None else h + hd
        ms = jnp.mean(h * h, axis=-1, keepdims=True)
        y = (h * lax.rsqrt(ms + EPS) * ng) * jax.nn.sigmoid(oa)
        out_ref[0, out_rows, :] = y.astype(BF16)

    def emit_ctx(i, _):
        emit(i + ncl, oac_ref[0, chunk(i), :], hc_ref, chunk(i))
        return 0

    def emit_lat(i, _):
        emit(i, oal_ref[0, chunk(i), :], hl_ref, chunk(i))
        return 0

    lax.fori_loop(0, ncc, emit_ctx, 0, unroll=2)
    lax.fori_loop(0, ncl, emit_lat, 0, unroll=4)


def _mlstm(qa_l, kat_l, va_l, oa_l, gr_l, qa_c, kat_c, va_c, oa_c, gr_c, norm_g):
    batch, t_lat, _ = qa_l.shape
    t_ctx = qa_c.shape[1]
    t_all = t_ctx + t_lat
    nc = t_all // BLOCK
    nc_pad = -(-nc // 8) * 8
    head = lambda t: pl.BlockSpec((1, t, A_HEAD_DIM), lambda b, h: (b, 0, h))
    head_t = lambda t: pl.BlockSpec((1, A_HEAD_DIM, t), lambda b, h: (b, h, 0))
    gates = lambda t: pl.BlockSpec((1, GATE_ROWS, t // BLOCK, BLOCK), lambda b, h: (b, h, 0, 0))
    chunked = lambda g: g.reshape(batch, A_HEADS * GATE_ROWS, g.shape[2] // BLOCK, BLOCK)
    return pl.pallas_call(
        functools.partial(_mlstm_kernel, t_ctx=t_ctx, t_lat=t_lat),
        grid=(batch, A_HEADS),
        in_specs=[head(t_lat), head_t(t_lat), head(t_lat), head(t_lat), gates(t_lat),
                  head(t_ctx), head_t(t_ctx), head(t_ctx), head(t_ctx), gates(t_ctx),
                  pl.BlockSpec((1, A_HEAD_DIM), lambda b, h: (0, h))],
        out_specs=[head(t_lat), head(t_ctx)],
        out_shape=[jax.ShapeDtypeStruct((batch, t_lat, A_WIDTH), BF16),
                   jax.ShapeDtypeStruct((batch, t_ctx, A_WIDTH), BF16)],
        scratch_shapes=[
            pltpu.VMEM((t_all, A_HEAD_DIM), BF16),
            pltpu.VMEM((A_HEAD_DIM, t_all), BF16),
            pltpu.VMEM((t_all, A_HEAD_DIM), BF16),
            pltpu.VMEM((4, nc_pad, BLOCK), F32),
            pltpu.VMEM((8, nc_pad, BLOCK), F32),
            pltpu.VMEM((4, nc_pad, BLOCK), F32),
            pltpu.VMEM((t_all, LANES), F32),
            pltpu.VMEM((4, nc_pad, LANES), F32),
            pltpu.VMEM((2, nc, A_HEAD_DIM, 2 * A_HEAD_DIM), F32),
            pltpu.VMEM((2, nc + 1, A_HEAD_DIM, 2 * A_HEAD_DIM), F32),
            pltpu.VMEM((2, nc_pad, LANES), F32),
        ],
        compiler_params=_params("arbitrary", "arbitrary"),
        name="mlstm",
    )(qa_l, kat_l, va_l, oa_l, chunked(gr_l), qa_c, kat_c, va_c, oa_c, chunked(gr_c),
      norm_g.reshape(1, A_WIDTH))


def _swa_kernel(sink_ref, q_ref, *refs, t_lat, has_window, blocks):
    if has_window:
        bias_ref, k_ref, v_ref, kx_ref, vx_ref, o_ref, s_scr = refs
    else:
        kx_ref, vx_ref, o_ref, s_scr = refs
    L = BLOCK
    nb = t_lat // L
    group = B_HEADS // B_KV_HEADS
    lane = lax.broadcasted_iota(jnp.int32, (L, LANES), 1)
    lo = lane < B_HEAD_DIM
    zero = jnp.zeros((L, LANES), BF16)
    def window(u):
        qblk = pl.program_id(1) * blocks + u
        start = pl.multiple_of(jnp.clip((qblk - 1) * L, 0, t_lat - 3 * L), L)
        return qblk, pl.ds(start, 3 * L)

    def logits(c, u, kvh):
        sl = slice(kvh * LANES, (kvh + 1) * LANES)
        q = q_ref[0, u * L:(u + 1) * L, :]
        qa = q[:, (2 * kvh) * LANES:(2 * kvh + 1) * LANES]
        qb = q[:, (2 * kvh + 1) * LANES:(2 * kvh + 2) * LANES]
        q4 = jnp.concatenate([jnp.where(lo, qa, zero), jnp.where(lo, zero, qa),
                              jnp.where(lo, qb, zero), jnp.where(lo, zero, qb)], axis=0)
        sink = jnp.concatenate([jnp.full((L, 1), sink_ref[kvh * group + g] * LOG2_E, F32)
                                for g in range(group)], axis=0)
        if has_window:
            qblk, win = window(u)
            bias = bias_ref[jnp.where(qblk == 0, 0, jnp.where(qblk == nb - 1, 2, 1))]
            keys = jnp.concatenate([k_ref[0, win, sl], kx_ref[0, :, sl]], axis=0)
            s = _dot_nt(q4, keys)
            s = (s.reshape(group, L, s.shape[1]) + bias[None]).reshape(s.shape)
        else:
            s = _dot_nt(q4, kx_ref[0, :, sl])
        s_scr[c] = s
        return jnp.maximum(jnp.max(s, axis=1, keepdims=True), sink), sink

    def attend(c, u, kvh, m, sink):
        sl = slice(kvh * LANES, (kvh + 1) * LANES)
        if has_window:
            vals = jnp.concatenate([v_ref[0, window(u)[1], sl], vx_ref[0, :, sl]], axis=0)
        else:
            vals = vx_ref[0, :, sl]
        e = jnp.exp2(s_scr[c] - m)
        denom = jnp.sum(e, axis=1, keepdims=True) + jnp.exp2(sink - m)
        o4 = _dot(e.astype(BF16), vals) / denom
        rows = slice(u * L, (u + 1) * L)
        o_ref[0, rows, (2 * kvh) * LANES:(2 * kvh + 1) * LANES] = (
            jnp.where(lo, o4[0:L], o4[L:2 * L]).astype(BF16))
        o_ref[0, rows, (2 * kvh + 1) * LANES:(2 * kvh + 2) * LANES] = (
            jnp.where(lo, o4[2 * L:3 * L], o4[3 * L:4 * L]).astype(BF16))

    chains = [(u, kvh) for u in range(blocks) for kvh in range(B_KV_HEADS)]
    stats = [logits(0, *chains[0])]
    for c, (u, kvh) in enumerate(chains):
        if c + 1 < len(chains):
            stats.append(logits(c + 1, *chains[c + 1]))
        attend(c, u, kvh, *stats[c])


def _band_bias(t_ctx):
    L = BLOCK
    t = np.arange(L)[:, None]
    j = np.arange(3 * L)[None, :]
    tables = []
    for rel in (0, -L, -2 * L):
        ok = np.abs(j - t + rel) <= WINDOW
        tables.append(np.concatenate([np.where(ok, 0.0, -np.inf), np.zeros((L, t_ctx))], axis=1))
    return np.stack(tables).astype(np.float32)


def _swa(sink, q, k, v, k_ctx, v_ctx, has_window):
    batch, t, _ = q.shape
    t_ctx = k_ctx.shape[1]
    nb = t // BLOCK
    blocks = min(SWA_BLOCKS, nb)
    assert nb % blocks == 0 and (nb >= 3 or not has_window)
    kvw = B_KV_HEADS * LANES
    cur = lambda b, i: (b, i, 0)
    whole = lambda b, i: (b, 0, 0)
    ctx_spec = pl.BlockSpec((1, t_ctx, kvw), whole)
    in_specs = [pl.BlockSpec(memory_space=pltpu.SMEM), pl.BlockSpec((1, blocks * BLOCK, B_WIDTH), cur)]
    args = [sink, q]
    if has_window:
        bias = _band_bias(t_ctx)
        in_specs += [pl.BlockSpec(bias.shape, lambda b, i: (0, 0, 0)),
                     pl.BlockSpec((1, t, kvw), whole), pl.BlockSpec((1, t, kvw), whole)]
        args += [bias, k, v]
    in_specs += [ctx_spec, ctx_spec]
    args += [k_ctx, v_ctx]
    return pl.pallas_call(
        functools.partial(_swa_kernel, t_lat=t, has_window=has_window, blocks=blocks),
        grid=(batch, nb // blocks),
        in_specs=in_specs,
        out_specs=pl.BlockSpec((1, blocks * BLOCK, B_WIDTH), cur),
        scratch_shapes=[pltpu.VMEM((blocks * B_KV_HEADS, (B_HEADS // B_KV_HEADS) * BLOCK,
                                    (3 * BLOCK if has_window else 0) + t_ctx), F32)],
        out_shape=jax.ShapeDtypeStruct((batch, t, B_WIDTH), BF16),
        compiler_params=_params("arbitrary", "arbitrary"),
        name="swa" if has_window else "swa_ctx",
    )(*args)


def _flash_kernel(q_ref, k_ref, v_ref, o_ref, s0_scr, s1_scr, m0_scr, m1_scr, vaug_scr, *, chunks, tq, unroll):
    group = C_HEADS // C_KV_HEADS
    nq = q_ref.shape[1] // tq
    vaug_scr[:, 0:LANES] = v_ref[0]
    vaug_scr[:, LANES:] = jnp.ones((v_ref.shape[1], LANES), BF16)

    def q_rows(t):
        return pl.ds(pl.multiple_of(t * tq, tq), tq)

    def load_q(t):
        q = q_ref[0, q_rows(t), :]
        return jnp.concatenate([q[:, g * LANES:(g + 1) * LANES] for g in range(group)], axis=0)

    def stage_a(q4, s_ref, m, start, size):
        s = _dot_nt(q4, k_ref[0, start:start + size, :])
        s_ref[:, start:start + size] = s
        cm = jnp.max(s, axis=1, keepdims=True)
        return cm if m is None else jnp.maximum(m, cm)

    def stage_b(s_ref, m, acc, start, size):
        p = jnp.exp2(s_ref[:, start:start + size] - m)
        pv = _dot(p.astype(BF16), vaug_scr[start:start + size, :])
        return pv if acc is None else acc + pv

    def step(t, cur, nxt):
        q4 = load_q(jnp.minimum(t + 1, nq - 1))
        m_cur = cur[1][...]
        m_next, acc = None, None
        for start, size in chunks:
            m_next = stage_a(q4, nxt[0], m_next, start, size)
            acc = stage_b(cur[0], m_cur, acc, start, size)
        nxt[1][...] = m_next
        out = acc[:, 0:LANES] / acc[:, LANES:2 * LANES]
        for g in range(group):
            o_ref[0, q_rows(t), g * LANES:(g + 1) * LANES] = out[g * tq:(g + 1) * tq].astype(BF16)

    bufs = ((s0_scr, m0_scr), (s1_scr, m1_scr))
    q4 = load_q(0)
    m = None
    for start, size in chunks:
        m = stage_a(q4, s0_scr, m, start, size)
    m0_scr[...] = m

    def body(u, _):
        for k in range(unroll):
            step(unroll * u + k, bufs[k % 2], bufs[(k + 1) % 2])
        return 0

    lax.fori_loop(0, nq // unroll, body, 0)


def _key_chunks(total, target):
    chunks = []
    start = 0
    while start < total:
        size = min(target, total - start)
        chunks.append((start, size))
        start += size
    return tuple(chunks)


def _flash(q, k, v, t_ctx):
    batch, t, _ = q.shape
    tk_all = k.shape[1]
    tq = _row_tile(t, FLASH_ROWS)
    parts = FLASH_PARTS if t % (FLASH_PARTS * 2 * tq) == 0 else 1
    tp = t // parts
    unroll = FLASH_UNROLL if (tp // tq) % FLASH_UNROLL == 0 else 2
    assert unroll % 2 == 0 and (tp // tq) % unroll == 0
    group = C_HEADS // C_KV_HEADS
    gw = group * C_HEAD_DIM
    chunks = (_key_chunks(t_ctx, FLASH_KEYS)
              + tuple((t_ctx + a, n) for a, n in _key_chunks(tk_all - t_ctx, FLASH_KEYS)))
    return pl.pallas_call(
        functools.partial(_flash_kernel, chunks=chunks, tq=tq, unroll=unroll),
        grid=(batch, C_KV_HEADS, parts),
        in_specs=[pl.BlockSpec((1, tp, gw), lambda b, h, p: (b, p, h)),
                  pl.BlockSpec((1, tk_all, C_HEAD_DIM), lambda b, h, p: (b, 0, h)),
                  pl.BlockSpec((1, tk_all, C_HEAD_DIM), lambda b, h, p: (b, 0, h))],
        out_specs=pl.BlockSpec((1, tp, gw), lambda b, h, p: (b, p, h)),
        out_shape=jax.ShapeDtypeStruct((batch, t, C_WIDTH), BF16),
        scratch_shapes=[pltpu.VMEM((group * tq, tk_all), F32), pltpu.VMEM((group * tq, tk_all), F32),
                        pltpu.VMEM((group * tq, 1), F32), pltpu.VMEM((group * tq, 1), F32),
                        pltpu.VMEM((tk_all, 2 * C_HEAD_DIM), BF16)],
        compiler_params=_params("arbitrary", "arbitrary", "arbitrary"),
        name="flash_c",
    )(q, k, v)


def _out_mlp_kernel(*refs, n_mix, ff_chunk):
    x_ref, mod_ref, g_ref = refs[0:3]
    mix_refs = refs[3:3 + n_mix]
    wo_refs = refs[3 + n_mix:3 + 2 * n_mix]
    w1_ref, w2_ref, o_ref = refs[3 + 2 * n_mix:]
    mod = mod_ref[0]
    y = None
    for a_ref, w_ref in zip(mix_refs, wo_refs):
        part = _dot(a_ref[0], w_ref[...])
        y = part if y is None else y + part
    x1 = x_ref[0] + mod[2:3] * y
    h = _modulated_norm(x1, g_ref[...], mod[3:4], mod[4:5]).astype(BF16)
    d_ff = w1_ref.shape[1]
    acc = None
    for c in range(d_ff // ff_chunk):
        a = _dot(h, w1_ref[:, c * ff_chunk:(c + 1) * ff_chunk])
        a = jnp.square(jnp.maximum(a, 0.0)).astype(BF16)
        part = _dot(a, w2_ref[c * ff_chunk:(c + 1) * ff_chunk, :])
        acc = part if acc is None else acc + part
    o_ref[0] = x1 + mod[5:6] * acc


def _out_mlp(x, mod, g, mixes, w_outs, w1, w2):
    batch, t, d = x.shape
    tm = _row_tile(t, MLP_ROWS)
    mod_map = (lambda b, i: (b, 0, 0)) if mod.shape[0] == batch else (lambda b, i: (0, 0, 0))
    const = lambda b, i: (0, 0)
    resident = lambda a: pl.BlockSpec(a.shape, const, pipeline_mode=pl.Buffered(1))
    row = lambda width: pl.BlockSpec((1, tm, width), lambda b, i: (b, i, 0))
    return pl.pallas_call(
        functools.partial(_out_mlp_kernel, n_mix=len(mixes), ff_chunk=1024),
        grid=(batch, t // tm),
        in_specs=([row(d), pl.BlockSpec((1, MOD_ROWS, d), mod_map), pl.BlockSpec((1, d), const)]
                  + [row(a.shape[2]) for a in mixes] + [resident(w) for w in w_outs]
                  + [resident(w1), resident(w2)]),
        out_specs=row(d),
        out_shape=jax.ShapeDtypeStruct((batch, t, d), F32),
        compiler_params=_params("arbitrary", "arbitrary"),
        name="out_mlp",
    )(x, mod, g, *mixes, *w_outs, w1, w2)


def kernel(x, c, ctx, c_ctx, ada_w, ada_b, norm1_g, norm2_g, ab_w_in, ab_gate_b, mlstm_norm_g, swa_q_norm_g,
           swa_k_norm_g, swa_sink, ab_w_out, c_w_in, c_q_norm_g, c_k_norm_g, c_w_out, mlp_w1, mlp_w2):
    depth = ada_w.shape[0]
    batch, _, d = x.shape
    mods = _mods(c, c_ctx, ada_w, ada_b)
    for layer in range(depth):
        last = layer == depth - 1
        mod_l = mods[layer, :batch]
        mod_c = mods[layer, batch:batch + 1]
        g1 = norm1_g[layer].reshape(1, d)
        g2 = norm2_g[layer].reshape(1, d)
        w1 = mlp_w1[layer].astype(BF16)
        w2 = mlp_w2[layer].astype(BF16)
        j = layer // 2
        if layer % 2 == 0:
            w, wt, gb = _ab_weights(ab_w_in[j], ab_gate_b[j])
            qn = jnp.tile(swa_q_norm_g[j], LANES // B_HEAD_DIM).reshape(1, LANES)
            kn = jnp.tile(swa_k_norm_g[j], LANES // B_HEAD_DIM).reshape(1, LANES)
            qa_l, kat_l, va_l, oa_l, qb_l, kb_l, vb_l, gr_l = _proj_ab(x, mod_l, g1, w, wt, gb, qn, kn, True)
            qa_c, kat_c, va_c, oa_c, qb_c, kb_c, vb_c, gr_c = _proj_ab(ctx, mod_c, g1, w, wt, gb, qn, kn, False)
            ha_l, ha_c = _mlstm(qa_l, kat_l, va_l, oa_l, gr_l, qa_c, kat_c, va_c, oa_c, gr_c, mlstm_norm_g[j])
            ob_l = _swa(swa_sink[j], qb_l, kb_l, vb_l, kb_c, vb_c, True)
            w_out = ab_w_out[j].astype(BF16)
            w_outs = [w_out[:A_WIDTH], w_out[A_WIDTH:]]
            x = _out_mlp(x, mod_l, g2, [ha_l, ob_l], w_outs, w1, w2)
            if not last:
                ob_c = _swa(swa_sink[j], qb_c, kb_c, vb_c, kb_c, vb_c, False)
                ctx = _out_mlp(ctx, mod_c, g2, [ha_c, ob_c], w_outs, w1, w2)
        else:
            w = c_w_in[j].astype(BF16)
            qn = c_q_norm_g[j].reshape(1, LANES)
            kn = c_k_norm_g[j].reshape(1, LANES)
            q_l, k_l, v_l = _proj_c(x, mod_l, g1, w, qn, kn, True)
            q_c, k_c, v_c = _proj_c(ctx, mod_c, g1, w, qn, kn, False)
            k_all = jnp.concatenate([k_c, k_l], axis=1)
            v_all = jnp.concatenate([v_c, v_l], axis=1)
            w_out = c_w_out[j].astype(BF16)
            o_l = _flash(q_l, k_all, v_all, k_c.shape[1])
            x = _out_mlp(x, mod_l, g2, [o_l], [w_out], w1, w2)
            if not last:
                o_c = _flash(q_c, k_c, v_c, 0)
                ctx = _out_mlp(ctx, mod_c, g2, [o_c], [w_out], w1, w2)
    return x
```

```python
import functools

import numpy as np
import jax
import jax.numpy as jnp
from jax import lax
from jax.experimental import pallas as pl
from jax.experimental.pallas import tpu as pltpu

F32 = jnp.float32
BF16 = jnp.bfloat16

GRID_W = 64
BLOCK = 128
WINDOW = 128
ROPE_BASE = 10000.0
EPS = 1e-6
LOG2_E = 1.4426950408889634
PROJ_ROWS = 512
MLP_ROWS = 512
SWA_BLOCKS = 4
FLASH_KEYS = 512
FLASH_UNROLL = 4
FLASH_PARTS = 1
FLASH_ROWS = 128
N_MOD = 6
MOD_ROWS = 8
LANES = 128
MXU_WIDTH = 256

A_HEADS = 4
A_HEAD_DIM = 128
A_WIDTH = A_HEADS * A_HEAD_DIM
A_GATES = 4 * A_HEADS
GATE_ROWS = 8
B_HEADS = 8
B_KV_HEADS = 2
B_HEAD_DIM = 64
B_WIDTH = B_HEADS * B_HEAD_DIM
B_KV_WIDTH = B_KV_HEADS * B_HEAD_DIM
C_HEADS = 8
C_KV_HEADS = 2
C_HEAD_DIM = 128
C_WIDTH = C_HEADS * C_HEAD_DIM
C_KV_WIDTH = C_KV_HEADS * C_HEAD_DIM

VMEM_LIMIT = 56 * 1024 * 1024

NT_DIMS = (((1,), (1,)), ((), ()))
TN_DIMS = (((0,), (0,)), ((), ()))


def _dot(a, b):
    return jnp.dot(a, b, preferred_element_type=F32)


def _dot_nt(a, b):
    return lax.dot_general(a, b, NT_DIMS, preferred_element_type=F32)


def _dot_tn(a, b):
    return lax.dot_general(a, b, TN_DIMS, preferred_element_type=F32)


def _params(*sem):
    return pltpu.CompilerParams(dimension_semantics=sem, vmem_limit_bytes=VMEM_LIMIT)


def _mods_kernel(c_ref, w_ref, b_ref, o_ref):
    cf = c_ref[...]
    s = (cf * jax.nn.sigmoid(cf)).astype(BF16)
    o_ref[0] = _dot(s, w_ref[0].astype(BF16)) + b_ref[0]


def _mods(c, c_ctx, ada_w, ada_b):
    depth, d, _ = ada_w.shape
    batch = c.shape[0]
    rows = -(-(batch + 1) // 8) * 8
    cc = jnp.zeros((rows, d), F32).at[:batch].set(c).at[batch].set(c_ctx)
    out = pl.pallas_call(
        _mods_kernel,
        grid=(depth, N_MOD),
        in_specs=[
            pl.BlockSpec((rows, d), lambda l, j: (0, 0)),
            pl.BlockSpec((1, d, d), lambda l, j: (l, 0, j)),
            pl.BlockSpec((1, 1, d), lambda l, j: (l, 0, j)),
        ],
        out_specs=pl.BlockSpec((1, rows, d), lambda l, j: (l, 0, j)),
        out_shape=jax.ShapeDtypeStruct((depth, rows, N_MOD * d), F32),
        compiler_params=_params("arbitrary", "arbitrary"),
        name="ada_mods",
    )(cc, ada_w, ada_b.reshape(depth, 1, N_MOD * d))
    out = out.reshape(depth, rows, N_MOD, d)
    return jnp.pad(out, ((0, 0), (0, 0), (0, MOD_ROWS - N_MOD), (0, 0)))


def _modulated_norm(x, g, shift, scale):
    ms = jnp.mean(x * x, axis=-1, keepdims=True)
    return (x * lax.rsqrt(ms + EPS) * g) * (1.0 + scale) + shift


def _split3(a):
    hi = a.astype(BF16)
    r = a - hi.astype(F32)
    mid = r.astype(BF16)
    lo = (r - mid.astype(F32)).astype(BF16)
    return hi, mid, lo


def _head_norm_rope(xh, gain, cos, sin, group, out_scale):
    sq = xh * xh
    half = group // 2
    if group == LANES:
        ssq = jnp.sum(sq, axis=-1, keepdims=True)
    else:
        assert 2 * group == LANES
        lane = lax.broadcasted_iota(jnp.int32, xh.shape, 1)
        left = lane < group
        ssq = jnp.where(left, jnp.sum(jnp.where(left, sq, 0.0), axis=-1, keepdims=True),
                        jnp.sum(jnp.where(left, 0.0, sq), axis=-1, keepdims=True))
    xn = xh * lax.rsqrt(ssq * (1.0 / group) + EPS) * gain
    if group == LANES:
        rot = pltpu.roll(xn, half, 1)
    else:
        first = (lane & (group - 1)) < half
        rot = jnp.where(first, pltpu.roll(xn, LANES - half, 1), pltpu.roll(xn, half, 1))
    y = xn * cos + rot * sin
    if out_scale != 1.0:
        y = y * out_scale
    return y


def _rope_tables(n_tokens, head_dim, use_rope):
    reps = LANES // head_dim
    if not use_rope:
        return np.ones((n_tokens, LANES), np.float32), np.zeros((n_tokens, LANES), np.float32)
    rows = n_tokens // GRID_W
    row = np.repeat(np.arange(rows, dtype=np.float64), GRID_W)
    col = np.tile(np.arange(GRID_W, dtype=np.float64), rows)
    pairs = head_dim // 4
    inv_freq = ROPE_BASE ** (-np.arange(pairs, dtype=np.float64) / pairs)
    ang = np.concatenate([row[:, None] * inv_freq, col[:, None] * inv_freq], axis=-1)
    cos, sin = np.cos(ang), np.sin(ang)
    cos_full = np.concatenate([cos, cos], axis=-1)
    sin_signed = np.concatenate([-sin, sin], axis=-1)
    return (np.tile(cos_full, (1, reps)).astype(np.float32), np.tile(sin_signed, (1, reps)).astype(np.float32))


def _row_tile(t, target):
    tm = min(t, target)
    assert t % tm == 0
    return tm


def _proj_ab_kernel(x_ref, mod_ref, g_ref, w_ref, wt_ref, gb_ref, cos_ref, sin_ref, qn_ref, kn_ref,
                    qa_ref, kat_ref, va_ref, oa_ref, qb_ref, kb_ref, vb_ref, gr_ref, acc_scr):
    @pl.when(pl.program_id(0) == 0)
    def _():
        acc_scr[...] = jnp.zeros_like(acc_scr)

    cos, sin = cos_ref[...], sin_ref[...]
    for j in range(B_WIDTH // LANES):
        y = _head_norm_rope(acc_scr[:, LANES * j:LANES * (j + 1)], qn_ref[...], cos, sin, B_HEAD_DIM,
                            B_HEAD_DIM ** -0.5 * LOG2_E)
        qb_ref[0, :, LANES * j:LANES * (j + 1)] = y.astype(BF16)
    for j in range(B_KV_HEADS):
        y = _head_norm_rope(acc_scr[:, B_WIDTH + LANES * j:B_WIDTH + LANES * (j + 1)], kn_ref[...], cos, sin,
                            B_HEAD_DIM, 1.0)
        kb_ref[0, :, LANES * j:LANES * (j + 1)] = y.astype(BF16)

    mod = mod_ref[0]
    h = _modulated_norm(x_ref[0], g_ref[...], mod[0:1], mod[1:2]).astype(BF16)

    def mm(lo, n):
        return _dot(h, w_ref[:, lo:lo + n])

    qa_ref[0] = mm(0, A_WIDTH).astype(BF16)
    va_ref[0] = mm(A_WIDTH, A_WIDTH).astype(BF16)
    oa_ref[0] = mm(2 * A_WIDTH, A_WIDTH)
    base = 3 * A_WIDTH
    staged = B_WIDTH + B_KV_HEADS * LANES
    for lo in range(0, staged, MXU_WIDTH):
        acc_scr[:, lo:lo + MXU_WIDTH] = mm(base + lo, MXU_WIDTH)
    base += staged
    vb_ref[0] = mm(base, B_KV_HEADS * LANES).astype(BF16)
    tr = _dot_nt(wt_ref[...], h)
    kat_ref[0] = (tr[0:A_WIDTH] * A_HEAD_DIM ** -0.5).astype(BF16)
    gt = tr[A_WIDTH:] + gb_ref[...]
    typ = lax.broadcasted_iota(jnp.int32, gt.shape, 0) & (GATE_ROWS - 1)
    log_sig = jnp.minimum(gt, 0.0) - jnp.log1p(jnp.exp(-jnp.abs(gt)))
    gr_ref[0] = jnp.where((typ == 1) | (typ == 3), log_sig, gt)


def _proj_ab(x, mod, g, w, wg, gb, qn, kn, use_rope):
    batch, t, d = x.shape
    tm = _row_tile(t, PROJ_ROWS)
    nt = t // tm
    last = batch * nt - 1
    cos, sin = _rope_tables(t, B_HEAD_DIM, use_rope)
    per_batch = mod.shape[0] == batch
    cur = lambda j: jnp.minimum(j, last)
    prev = lambda j: jnp.maximum(j - 1, 0)
    const = lambda j: (0, 0)
    row_at = lambda width, tile: pl.BlockSpec((1, tm, width), lambda j: (tile(j) // nt, tile(j) % nt, 0))
    row = lambda width: row_at(width, cur)
    table = pl.BlockSpec((tm, LANES), lambda j: (prev(j) % nt, 0))
    kvw = B_KV_HEADS * LANES
    out_shape = [
        jax.ShapeDtypeStruct((batch, t, A_WIDTH), BF16),
        jax.ShapeDtypeStruct((batch, A_WIDTH, t), BF16),
        jax.ShapeDtypeStruct((batch, t, A_WIDTH), BF16),
        jax.ShapeDtypeStruct((batch, t, A_WIDTH), F32),
        jax.ShapeDtypeStruct((batch, t, B_WIDTH), BF16),
        jax.ShapeDtypeStruct((batch, t, kvw), BF16),
        jax.ShapeDtypeStruct((batch, t, kvw), BF16),
        jax.ShapeDtypeStruct((batch, A_HEADS * GATE_ROWS, t), F32),
    ]
    col = lambda height: pl.BlockSpec((1, height, tm), lambda j: (cur(j) // nt, 0, cur(j) % nt))
    out_specs = [row(A_WIDTH), col(A_WIDTH), row(A_WIDTH), row(A_WIDTH), row_at(B_WIDTH, prev), row_at(kvw, prev),
                 row(kvw), col(A_HEADS * GATE_ROWS)]
    return pl.pallas_call(
        _proj_ab_kernel,
        grid=(batch * nt + 1,),
        in_specs=[
            row(d),
            pl.BlockSpec((1, MOD_ROWS, d), lambda j: ((cur(j) // nt) if per_batch else 0, 0, 0)),
            pl.BlockSpec((1, d), const),
            pl.BlockSpec(w.shape, const),
            pl.BlockSpec(wg.shape, const),
            pl.BlockSpec(gb.shape, const),
            table, table,
            pl.BlockSpec((1, LANES), const),
            pl.BlockSpec((1, LANES), const),
        ],
        out_specs=out_specs,
        out_shape=out_shape,
        scratch_shapes=[pltpu.VMEM((tm, B_WIDTH + kvw), F32)],
        compiler_params=_params("arbitrary"),
        name="proj_ab",
    )(x, mod, g, w, wg, gb, cos, sin, qn, kn)


def _ab_weights(w_in, gate_b):
    bounds = [A_WIDTH * 4, A_WIDTH * 4 + A_GATES, A_WIDTH * 4 + A_GATES + B_WIDTH,
              A_WIDTH * 4 + A_GATES + B_WIDTH + B_KV_WIDTH]
    wa, wgate, wqb, wkb, wvb = jnp.split(w_in, bounds, axis=1)
    wqa, wka, wva_oa = wa[:, :A_WIDTH], wa[:, A_WIDTH:2 * A_WIDTH], wa[:, 2 * A_WIDTH:]

    def dup(wk):
        parts = []
        for hh in range(B_KV_HEADS):
            blk = wk[:, hh * B_HEAD_DIM:(hh + 1) * B_HEAD_DIM]
            parts += [blk] * (LANES // B_HEAD_DIM)
        return jnp.concatenate(parts, axis=1)

    w = jnp.concatenate([wqa, wva_oa, wqb, dup(wkb), dup(wvb)], axis=1).astype(BF16)
    d = w_in.shape[0]
    wg = wgate.reshape(d, 4, A_HEADS).transpose(2, 1, 0)
    wg = jnp.pad(wg, ((0, 0), (0, GATE_ROWS - 4), (0, 0))).reshape(A_HEADS * GATE_ROWS, d)
    wt = jnp.concatenate([wka.T, wg], axis=0).astype(BF16)
    gb = gate_b.reshape(4, A_HEADS).T
    gb = jnp.pad(gb, ((0, 0), (0, GATE_ROWS - 4))).reshape(A_HEADS * GATE_ROWS, 1).astype(F32)
    return w, wt, gb


def _proj_c_kernel(x_ref, mod_ref, g_ref, w_ref, cos_ref, sin_ref, qn_ref, kn_ref, q_ref, k_ref, v_ref, acc_scr):
    @pl.when(pl.program_id(0) == 0)
    def _():
        acc_scr[...] = jnp.zeros_like(acc_scr)

    cos, sin = cos_ref[...], sin_ref[...]
    q_scale = C_HEAD_DIM ** -0.5 * LOG2_E
    for j in range(C_HEADS):
        y = _head_norm_rope(acc_scr[:, LANES * j:LANES * (j + 1)], qn_ref[...], cos, sin, C_HEAD_DIM, q_scale)
        q_ref[0, :, LANES * j:LANES * (j + 1)] = y.astype(BF16)
    for j in range(C_KV_HEADS):
        y = _head_norm_rope(acc_scr[:, C_WIDTH + LANES * j:C_WIDTH + LANES * (j + 1)], kn_ref[...], cos, sin,
                            C_HEAD_DIM, 1.0)
        k_ref[0, :, LANES * j:LANES * (j + 1)] = y.astype(BF16)

    mod = mod_ref[0]
    h = _modulated_norm(x_ref[0], g_ref[...], mod[0:1], mod[1:2]).astype(BF16)
    qk_width = C_WIDTH + C_KV_WIDTH
    for lo in range(0, qk_width, MXU_WIDTH):
        acc_scr[:, lo:lo + MXU_WIDTH] = _dot(h, w_ref[:, lo:lo + MXU_WIDTH])
    v_ref[0] = _dot(h, w_ref[:, qk_width:qk_width + C_KV_WIDTH]).astype(BF16)


def _proj_c(x, mod, g, w, qn, kn, use_rope):
    batch, t, d = x.shape
    tm = _row_tile(t, PROJ_ROWS)
    nt = t // tm
    last = batch * nt - 1
    cos, sin = _rope_tables(t, C_HEAD_DIM, use_rope)
    per_batch = mod.shape[0] == batch
    cur = lambda j: jnp.minimum(j, last)
    prev = lambda j: jnp.maximum(j - 1, 0)
    const = lambda j: (0, 0)
    row_at = lambda width, tile: pl.BlockSpec((1, tm, width), lambda j: (tile(j) // nt, tile(j) % nt, 0))
    table = pl.BlockSpec((tm, LANES), lambda j: (prev(j) % nt, 0))
    return pl.pallas_call(
        _proj_c_kernel,
        grid=(batch * nt + 1,),
        in_specs=[
            row_at(d, cur),
            pl.BlockSpec((1, MOD_ROWS, d), lambda j: ((cur(j) // nt) if per_batch else 0, 0, 0)),
            pl.BlockSpec((1, d), const),
            pl.BlockSpec(w.shape, const),
            table, table,
            pl.BlockSpec((1, LANES), const),
            pl.BlockSpec((1, LANES), const),
        ],
        out_specs=[row_at(C_WIDTH, prev), row_at(C_KV_WIDTH, prev), row_at(C_KV_WIDTH, cur)],
        out_shape=[jax.ShapeDtypeStruct((batch, t, C_WIDTH), BF16),
                   jax.ShapeDtypeStruct((batch, t, C_KV_WIDTH), BF16),
                   jax.ShapeDtypeStruct((batch, t, C_KV_WIDTH), BF16)],
        scratch_shapes=[pltpu.VMEM((tm, C_WIDTH + C_KV_WIDTH), F32)],
        compiler_params=_params("arbitrary"),
        name="proj_c",
    )(x, mod, g, w, cos, sin, qn, kn)


def _mlstm_kernel(ql_ref, ktl_ref, vl_ref, oal_ref, gl_ref, qc_ref, ktc_ref, vc_ref, oac_ref, gc_ref, ng_ref,
                  hl_ref, hc_ref,
                  qs, kts, vs, gs, rowq, rowg, cols, stats, kvb, cst, mst, *, t_ctx, t_lat):
    L = BLOCK
    ncc = t_ctx // L
    ncl = t_lat // L
    nc = ncc + ncl

    qs[0:t_lat] = ql_ref[0]
    qs[t_lat:] = qc_ref[0]
    kts[:, 0:t_lat] = ktl_ref[0]
    kts[:, t_lat:] = ktc_ref[0]
    vs[0:t_lat] = vl_ref[0]
    vs[t_lat:] = vc_ref[0]
    for r in range(4):
        gs[r, 0:ncl] = gl_ref[0, r]
        gs[r, ncl:nc] = gc_ref[0, r]

    ri = lax.broadcasted_iota(jnp.int32, (L, L), 0)
    ci = lax.broadcasted_iota(jnp.int32, (L, L), 1)
    lower = ci <= ri
    upper = ci >= ri
    ones_blk = jnp.ones((L, L), BF16)

    def chunk(i):
        return pl.ds(pl.multiple_of(i * L, L), L)

    def cum(x, mat):
        hi, mid, lo = _split3(x)
        return _dot(hi, mat) + _dot(mid, mat) + _dot(lo, mat)

    li_f, lf_f, li_b, lf_b = (gs[r, 0:nc] for r in range(4))
    bcum_f = cum(lf_f, jnp.where(upper, 1.0, 0.0).astype(BF16))
    bsuf_b = cum(lf_b, jnp.where(lower, 1.0, 0.0).astype(BF16))
    bl_f = bcum_f[:, L - 1:L]
    bl_b = bsuf_b[:, 0:1]
    wl_f = bl_f - bcum_f + li_f
    wl_b = bl_b - bsuf_b + li_b
    mx_f = jnp.max(wl_f, axis=1, keepdims=True)
    mx_b = jnp.max(wl_b, axis=1, keepdims=True)
    rb_f = li_f - bcum_f
    rb_b = li_b - bsuf_b
    lane = lax.broadcasted_iota(jnp.int32, (nc, L), 1)
    pm_f, pm_b = rb_f, rb_b
    step = 1
    while step < L:
        pm_f = jnp.where(lane >= step, jnp.maximum(pm_f, pltpu.roll(pm_f, step, 1)), pm_f)
        pm_b = jnp.where(lane < L - step, jnp.maximum(pm_b, pltpu.roll(pm_b, L - step, 1)), pm_b)
        step *= 2
    for n, val in enumerate((bcum_f, bsuf_b, rb_f, rb_b, jnp.exp(wl_f - mx_f), jnp.exp(wl_b - mx_b), pm_f, pm_b)):
        rowq[n, 0:nc] = val
    for n, val in enumerate((bl_f, bl_b, mx_f, mx_b)):
        stats[n, 0:nc] = jnp.broadcast_to(val, (nc, L))

    def prep(i, _):
        kt = kts[:, chunk(i)].astype(F32)
        vaug = jnp.concatenate([vs[chunk(i), :], ones_blk], axis=1)
        for d in range(2):
            kvb[d, i] = _dot((kt * rowq[4 + d, pl.ds(i, 1), :]).astype(BF16), vaug)
        return 0

    lax.fori_loop(0, nc, prep, 0, unroll=2)

    def visit(d, j):
        if d == 0:
            return jnp.where(j < ncc, ncl + j, j - ncc)
        return nc - 1 - j

    for d in range(2):
        cst[d, visit(d, 0)] = jnp.zeros((L, 2 * L), F32)

    def scan_step(j, carry):
        new = []
        for d in range(2):
            m = carry[d]
            i = visit(d, j)
            i_next = jnp.where(j + 1 < nc, visit(d, j + 1), nc)
            mst[d, pl.ds(i, 1), :] = jnp.broadcast_to(m, (1, L))
            bl = stats[d, pl.ds(i, 1), 0:1]
            mx = stats[2 + d, pl.ds(i, 1), 0:1]
            m_new = jnp.maximum(bl + m, mx)
            cst[d, i_next] = jnp.exp(bl + m - m_new) * cst[d, i] + jnp.exp(mx - m_new) * kvb[d, i]
            new.append(m_new)
        return tuple(new)

    m0 = jnp.zeros((1, 1), F32)
    lax.fori_loop(0, nc, scan_step, (m0, m0))

    for d in range(2):
        g = jnp.maximum(mst[d, 0:nc], rowq[6 + d, 0:nc])
        rowg[d, 0:nc] = g
        rowg[2 + d, 0:nc] = jnp.exp(-rowq[d, 0:nc] - g)

    def to_cols(i, _):
        r8 = jnp.concatenate([rowg[n, pl.ds(i, 1), :] for n in range(4)] + [jnp.zeros((4, L), F32)], axis=0)
        cols[chunk(i), :] = jnp.concatenate([r8, jnp.zeros((L - 8, L), F32)], axis=0).T
        return 0

    lax.fori_loop(0, nc, to_cols, 0, unroll=2)

    ng = ng_ref[...]

    def emit(i, oa, out_ref, out_rows):
        q = qs[chunk(i), :]
        vaug = jnp.concatenate([vs[chunk(i), :], ones_blk], axis=1)
        qk = _dot(q, kts[:, chunk(i)])
        colblk = cols[chunk(i), :]
        h = None
        for d in range(2):
            g = jnp.broadcast_to(colblk[:, d:d + 1], (L, L))
            floor = colblk[:, 2 + d:3 + d]
            rb = rowq[2 + d, pl.ds(i, 1), :]
            mask = lower if d == 0 else upper
            dm = jnp.where(mask, jnp.exp(rb - g), 0.0)
            s = (qk * dm).astype(BF16)
            iw = jnp.exp(mst[d, pl.ds(i, 1), 0:1] - g)
            nd = jnp.concatenate([iw, iw], axis=1) * _dot(q, cst[d, i].astype(BF16)) + _dot(s, vaug)
            hd = nd[:, 0:L] / jnp.maximum(jnp.abs(nd[:, L:2 * L]), floor)
            h = hd if h is None else h + hd
        ms = jnp.mean(h * h, axis=-1, keepdims=True)
        y = (h * lax.rsqrt(ms + EPS) * ng) * jax.nn.sigmoid(oa)
        out_ref[0, out_rows, :] = y.astype(BF16)

    def emit_ctx(i, _):
        emit(i + ncl, oac_ref[0, chunk(i), :], hc_ref, chunk(i))
        return 0

    def emit_lat(i, _):
        emit(i, oal_ref[0, chunk(i), :], hl_ref, chunk(i))
        return 0

    lax.fori_loop(0, ncc, emit_ctx, 0, unroll=2)
    lax.fori_loop(0, ncl, emit_lat, 0, unroll=4)


def _mlstm(qa_l, kat_l, va_l, oa_l, gr_l, qa_c, kat_c, va_c, oa_c, gr_c, norm_g):
    batch, t_lat, _ = qa_l.shape
    t_ctx = qa_c.shape[1]
    t_all = t_ctx + t_lat
    nc = t_all // BLOCK
    nc_pad = -(-nc // 8) * 8
    head = lambda t: pl.BlockSpec((1, t, A_HEAD_DIM), lambda b, h: (b, 0, h))
    head_t = lambda t: pl.BlockSpec((1, A_HEAD_DIM, t), lambda b, h: (b, h, 0))
    gates = lambda t: pl.BlockSpec((1, GATE_ROWS, t // BLOCK, BLOCK), lambda b, h: (b, h, 0, 0))
    chunked = lambda g: g.reshape(batch, A_HEADS * GATE_ROWS, g.shape[2] // BLOCK, BLOCK)
    return pl.pallas_call(
        functools.partial(_mlstm_kernel, t_ctx=t_ctx, t_lat=t_lat),
        grid=(batch, A_HEADS),
        in_specs=[head(t_lat), head_t(t_lat), head(t_lat), head(t_lat), gates(t_lat),
                  head(t_ctx), head_t(t_ctx), head(t_ctx), head(t_ctx), gates(t_ctx),
                  pl.BlockSpec((1, A_HEAD_DIM), lambda b, h: (0, h))],
        out_specs=[head(t_lat), head(t_ctx)],
        out_shape=[jax.ShapeDtypeStruct((batch, t_lat, A_WIDTH), BF16),
                   jax.ShapeDtypeStruct((batch, t_ctx, A_WIDTH), BF16)],
        scratch_shapes=[
            pltpu.VMEM((t_all, A_HEAD_DIM), BF16),
            pltpu.VMEM((A_HEAD_DIM, t_all), BF16),
            pltpu.VMEM((t_all, A_HEAD_DIM), BF16),
            pltpu.VMEM((4, nc_pad, BLOCK), F32),
            pltpu.VMEM((8, nc_pad, BLOCK), F32),
            pltpu.VMEM((4, nc_pad, BLOCK), F32),
            pltpu.VMEM((t_all, LANES), F32),
            pltpu.VMEM((4, nc_pad, LANES), F32),
            pltpu.VMEM((2, nc, A_HEAD_DIM, 2 * A_HEAD_DIM), F32),
            pltpu.VMEM((2, nc + 1, A_HEAD_DIM, 2 * A_HEAD_DIM), F32),
            pltpu.VMEM((2, nc_pad, LANES), F32),
        ],
        compiler_params=_params("arbitrary", "arbitrary"),
        name="mlstm",
    )(qa_l, kat_l, va_l, oa_l, chunked(gr_l), qa_c, kat_c, va_c, oa_c, chunked(gr_c),
      norm_g.reshape(1, A_WIDTH))


def _swa_kernel(sink_ref, q_ref, *refs, t_lat, has_window, blocks):
    if has_window:
        bias_ref, k_ref, v_ref, kx_ref, vx_ref, o_ref, s_scr = refs
    else:
        kx_ref, vx_ref, o_ref, s_scr = refs
    L = BLOCK
    nb = t_lat // L
    group = B_HEADS // B_KV_HEADS
    lane = lax.broadcasted_iota(jnp.int32, (L, LANES), 1)
    lo = lane < B_HEAD_DIM
    zero = jnp.zeros((L, LANES), BF16)
    def window(u):
        qblk = pl.program_id(1) * blocks + u
        start = pl.multiple_of(jnp.clip((qblk - 1) * L, 0, t_lat - 3 * L), L)
        return qblk, pl.ds(start, 3 * L)

    def logits(c, u, kvh):
        sl = slice(kvh * LANES, (kvh + 1) * LANES)
        q = q_ref[0, u * L:(u + 1) * L, :]
        qa = q[:, (2 * kvh) * LANES:(2 * kvh + 1) * LANES]
        qb = q[:, (2 * kvh + 1) * LANES:(2 * kvh + 2) * LANES]
        q4 = jnp.concatenate([jnp.where(lo, qa, zero), jnp.where(lo, zero, qa),
                              jnp.where(lo, qb, zero), jnp.where(lo, zero, qb)], axis=0)
        sink = jnp.concatenate([jnp.full((L, 1), sink_ref[kvh * group + g] * LOG2_E, F32)
                                for g in range(group)], axis=0)
        if has_window:
            qblk, win = window(u)
            bias = bias_ref[jnp.where(qblk == 0, 0, jnp.where(qblk == nb - 1, 2, 1))]
            keys = jnp.concatenate([k_ref[0, win, sl], kx_ref[0, :, sl]], axis=0)
            s = _dot_nt(q4, keys)
            s = (s.reshape(group, L, s.shape[1]) + bias[None]).reshape(s.shape)
        else:
            s = _dot_nt(q4, kx_ref[0, :, sl])
        s_scr[c] = s
        return jnp.maximum(jnp.max(s, axis=1, keepdims=True), sink), sink

    def attend(c, u, kvh, m, sink):
        sl = slice(kvh * LANES, (kvh + 1) * LANES)
        if has_window:
            vals = jnp.concatenate([v_ref[0, window(u)[1], sl], vx_ref[0, :, sl]], axis=0)
        else:
            vals = vx_ref[0, :, sl]
        e = jnp.exp2(s_scr[c] - m)
        denom = jnp.sum(e, axis=1, keepdims=True) + jnp.exp2(sink - m)
        o4 = _dot(e.astype(BF16), vals) / denom
        rows = slice(u * L, (u + 1) * L)
        o_ref[0, rows, (2 * kvh) * LANES:(2 * kvh + 1) * LANES] = (
            jnp.where(lo, o4[0:L], o4[L:2 * L]).astype(BF16))
        o_ref[0, rows, (2 * kvh + 1) * LANES:(2 * kvh + 2) * LANES] = (
            jnp.where(lo, o4[2 * L:3 * L], o4[3 * L:4 * L]).astype(BF16))

    chains = [(u, kvh) for u in range(blocks) for kvh in range(B_KV_HEADS)]
    stats = [logits(0, *chains[0])]
    for c, (u, kvh) in enumerate(chains):
        if c + 1 < len(chains):
            stats.append(logits(c + 1, *chains[c + 1]))
        attend(c, u, kvh, *stats[c])


def _band_bias(t_ctx):
    L = BLOCK
    t = np.arange(L)[:, None]
    j = np.arange(3 * L)[None, :]
    tables = []
    for rel in (0, -L, -2 * L):
        ok = np.abs(j - t + rel) <= WINDOW
        tables.append(np.concatenate([np.where(ok, 0.0, -np.inf), np.zeros((L, t_ctx))], axis=1))
    return np.stack(tables).astype(np.float32)


def _swa(sink, q, k, v, k_ctx, v_ctx, has_window):
    batch, t, _ = q.shape
    t_ctx = k_ctx.shape[1]
    nb = t // BLOCK
    blocks = min(SWA_BLOCKS, nb)
    assert nb % blocks == 0 and (nb >= 3 or not has_window)
    kvw = B_KV_HEADS * LANES
    cur = lambda b, i: (b, i, 0)
    whole = lambda b, i: (b, 0, 0)
    ctx_spec = pl.BlockSpec((1, t_ctx, kvw), whole)
    in_specs = [pl.BlockSpec(memory_space=pltpu.SMEM), pl.BlockSpec((1, blocks * BLOCK, B_WIDTH), cur)]
    args = [sink, q]
    if has_window:
        bias = _band_bias(t_ctx)
        in_specs += [pl.BlockSpec(bias.shape, lambda b, i: (0, 0, 0)),
                     pl.BlockSpec((1, t, kvw), whole), pl.BlockSpec((1, t, kvw), whole)]
        args += [bias, k, v]
    in_specs += [ctx_spec, ctx_spec]
    args += [k_ctx, v_ctx]
    return pl.pallas_call(
        functools.partial(_swa_kernel, t_lat=t, has_window=has_window, blocks=blocks),
        grid=(batch, nb // blocks),
        in_specs=in_specs,
        out_specs=pl.BlockSpec((1, blocks * BLOCK, B_WIDTH), cur),
        scratch_shapes=[pltpu.VMEM((blocks * B_KV_HEADS, (B_HEADS // B_KV_HEADS) * BLOCK,
                                    (3 * BLOCK if has_window else 0) + t_ctx), F32)],
        out_shape=jax.ShapeDtypeStruct((batch, t, B_WIDTH), BF16),
        compiler_params=_params("arbitrary", "arbitrary"),
        name="swa" if has_window else "swa_ctx",
    )(*args)


def _flash_kernel(q_ref, k_ref, v_ref, o_ref, s0_scr, s1_scr, m0_scr, m1_scr, vaug_scr, *, chunks, tq, unroll):
    group = C_HEADS // C_KV_HEADS
    nq = q_ref.shape[1] // tq
    vaug_scr[:, 0:LANES] = v_ref[0]
    vaug_scr[:, LANES:] = jnp.ones((v_ref.shape[1], LANES), BF16)

    def q_rows(t):
        return pl.ds(pl.multiple_of(t * tq, tq), tq)

    def load_q(t):
        q = q_ref[0, q_rows(t), :]
        return jnp.concatenate([q[:, g * LANES:(g + 1) * LANES] for g in range(group)], axis=0)

    def stage_a(q4, s_ref, m, start, size):
        s = _dot_nt(q4, k_ref[0, start:start + size, :])
        s_ref[:, start:start + size] = s
        cm = jnp.max(s, axis=1, keepdims=True)
        return cm if m is None else jnp.maximum(m, cm)

    def stage_b(s_ref, m, acc, start, size):
        p = jnp.exp2(s_ref[:, start:start + size] - m)
        pv = _dot(p.astype(BF16), vaug_scr[start:start + size, :])
        return pv if acc is None else acc + pv

    def step(t, cur, nxt):
        q4 = load_q(jnp.minimum(t + 1, nq - 1))
        m_cur = cur[1][...]
        m_next, acc = None, None
        for start, size in chunks:
            m_next = stage_a(q4, nxt[0], m_next, start, size)
            acc = stage_b(cur[0], m_cur, acc, start, size)
        nxt[1][...] = m_next
        out = acc[:, 0:LANES] / acc[:, LANES:2 * LANES]
        for g in range(group):
            o_ref[0, q_rows(t), g * LANES:(g + 1) * LANES] = out[g * tq:(g + 1) * tq].astype(BF16)

    bufs = ((s0_scr, m0_scr), (s1_scr, m1_scr))
    q4 = load_q(0)
    m = None
    for start, size in chunks:
        m = stage_a(q4, s0_scr, m, start, size)
    m0_scr[...] = m

    def body(u, _):
        for k in range(unroll):
            step(unroll * u + k, bufs[k % 2], bufs[(k + 1) % 2])
        return 0

    lax.fori_loop(0, nq // unroll, body, 0)


def _key_chunks(total, target):
    chunks = []
    start = 0
    while start < total:
        size = min(target, total - start)
        chunks.append((start, size))
        start += size
    return tuple(chunks)


def _flash(q, k, v, t_ctx):
    batch, t, _ = q.shape
    tk_all = k.shape[1]
    tq = _row_tile(t, FLASH_ROWS)
    parts = FLASH_PARTS if t % (FLASH_PARTS * 2 * tq) == 0 else 1
    tp = t // parts
    unroll = FLASH_UNROLL if (tp // tq) % FLASH_UNROLL == 0 else 2
    assert unroll % 2 == 0 and (tp // tq) % unroll == 0
    group = C_HEADS // C_KV_HEADS
    gw = group * C_HEAD_DIM
    chunks = (_key_chunks(t_ctx, FLASH_KEYS)
              + tuple((t_ctx + a, n) for a, n in _key_chunks(tk_all - t_ctx, FLASH_KEYS)))
    return pl.pallas_call(
        functools.partial(_flash_kernel, chunks=chunks, tq=tq, unroll=unroll),
        grid=(batch, C_KV_HEADS, parts),
        in_specs=[pl.BlockSpec((1, tp, gw), lambda b, h, p: (b, p, h)),
                  pl.BlockSpec((1, tk_all, C_HEAD_DIM), lambda b, h, p: (b, 0, h)),
                  pl.BlockSpec((1, tk_all, C_HEAD_DIM), lambda b, h, p: (b, 0, h))],
        out_specs=pl.BlockSpec((1, tp, gw), lambda b, h, p: (b, p, h)),
        out_shape=jax.ShapeDtypeStruct((batch, t, C_WIDTH), BF16),
        scratch_shapes=[pltpu.VMEM((group * tq, tk_all), F32), pltpu.VMEM((group * tq, tk_all), F32),
                        pltpu.VMEM((group * tq, 1), F32), pltpu.VMEM((group * tq, 1), F32),
                        pltpu.VMEM((tk_all, 2 * C_HEAD_DIM), BF16)],
        compiler_params=_params("arbitrary", "arbitrary", "arbitrary"),
        name="flash_c",
    )(q, k, v)


def _out_mlp_kernel(*refs, n_mix, ff_chunk):
    x_ref, mod_ref, g_ref = refs[0:3]
    mix_refs = refs[3:3 + n_mix]
    wo_refs = refs[3 + n_mix:3 + 2 * n_mix]
    w1_ref, w2_ref, o_ref = refs[3 + 2 * n_mix:]
    mod = mod_ref[0]
    y = None
    for a_ref, w_ref in zip(mix_refs, wo_refs):
        part = _dot(a_ref[0], w_ref[...])
        y = part if y is None else y + part
    x1 = x_ref[0] + mod[2:3] * y
    h = _modulated_norm(x1, g_ref[...], mod[3:4], mod[4:5]).astype(BF16)
    d_ff = w1_ref.shape[1]
    acc = None
    for c in range(d_ff // ff_chunk):
        a = _dot(h, w1_ref[:, c * ff_chunk:(c + 1) * ff_chunk])
        a = jnp.square(jnp.maximum(a, 0.0)).astype(BF16)
        part = _dot(a, w2_ref[c * ff_chunk:(c + 1) * ff_chunk, :])
        acc = part if acc is None else acc + part
    o_ref[0] = x1 + mod[5:6] * acc


def _out_mlp(x, mod, g, mixes, w_outs, w1, w2):
    batch, t, d = x.shape
    tm = _row_tile(t, MLP_ROWS)
    mod_map = (lambda b, i: (b, 0, 0)) if mod.shape[0] == batch else (lambda b, i: (0, 0, 0))
    const = lambda b, i: (0, 0)
    resident = lambda a: pl.BlockSpec(a.shape, const, pipeline_mode=pl.Buffered(1))
    row = lambda width: pl.BlockSpec((1, tm, width), lambda b, i: (b, i, 0))
    return pl.pallas_call(
        functools.partial(_out_mlp_kernel, n_mix=len(mixes), ff_chunk=1024),
        grid=(batch, t // tm),
        in_specs=([row(d), pl.BlockSpec((1, MOD_ROWS, d), mod_map), pl.BlockSpec((1, d), const)]
                  + [row(a.shape[2]) for a in mixes] + [resident(w) for w in w_outs]
                  + [resident(w1), resident(w2)]),
        out_specs=row(d),
        out_shape=jax.ShapeDtypeStruct((batch, t, d), F32),
        compiler_params=_params("arbitrary", "arbitrary"),
        name="out_mlp",
    )(x, mod, g, *mixes, *w_outs, w1, w2)


def kernel(x, c, ctx, c_ctx, ada_w, ada_b, norm1_g, norm2_g, ab_w_in, ab_gate_b, mlstm_norm_g, swa_q_norm_g,
           swa_k_norm_g, swa_sink, ab_w_out, c_w_in, c_q_norm_g, c_k_norm_g, c_w_out, mlp_w1, mlp_w2):
    depth = ada_w.shape[0]
    batch, _, d = x.shape
    mods = _mods(c, c_ctx, ada_w, ada_b)
    for layer in range(depth):
        last = layer == depth - 1
        mod_l = mods[layer, :batch]
        mod_c = mods[layer, batch:batch + 1]
        g1 = norm1_g[layer].reshape(1, d)
        g2 = norm2_g[layer].reshape(1, d)
        w1 = mlp_w1[layer].astype(BF16)
        w2 = mlp_w2[layer].astype(BF16)
        j = layer // 2
        if layer % 2 == 0:
            w, wt, gb = _ab_weights(ab_w_in[j], ab_gate_b[j])
            qn = jnp.tile(swa_q_norm_g[j], LANES // B_HEAD_DIM).reshape(1, LANES)
            kn = jnp.tile(swa_k_norm_g[j], LANES // B_HEAD_DIM).reshape(1, LANES)
            qa_l, kat_l, va_l, oa_l, qb_l, kb_l, vb_l, gr_l = _proj_ab(x, mod_l, g1, w, wt, gb, qn, kn, True)
            qa_c, kat_c, va_c, oa_c, qb_c, kb_c, vb_c, gr_c = _proj_ab(ctx, mod_c, g1, w, wt, gb, qn, kn, False)
            ha_l, ha_c = _mlstm(qa_l, kat_l, va_l, oa_l, gr_l, qa_c, kat_c, va_c, oa_c, gr_c, mlstm_norm_g[j])
            ob_l = _swa(swa_sink[j], qb_l, kb_l, vb_l, kb_c, vb_c, True)
            w_out = ab_w_out[j].astype(BF16)
            w_outs = [w_out[:A_WIDTH], w_out[A_WIDTH:]]
            x = _out_mlp(x, mod_l, g2, [ha_l, ob_l], w_outs, w1, w2)
            if not last:
                ob_c = _swa(swa_sink[j], qb_c, kb_c, vb_c, kb_c, vb_c, False)
                ctx = _out_mlp(ctx, mod_c, g2, [ha_c, ob_c], w_outs, w1, w2)
        else:
            w = c_w_in[j].astype(BF16)
            qn = c_q_norm_g[j].reshape(1, LANES)
            kn = c_k_norm_g[j].reshape(1, LANES)
            q_l, k_l, v_l = _proj_c(x, mod_l, g1, w, qn, kn, True)
            q_c, k_c, v_c = _proj_c(ctx, mod_c, g1, w, qn, kn, False)
            k_all = jnp.concatenate([k_c, k_l], axis=1)
            v_all = jnp.concatenate([v_c, v_l], axis=1)
            w_out = c_w_out[j].astype(BF16)
            o_l = _flash(q_l, k_all, v_all, k_c.shape[1])
            x = _out_mlp(x, mod_l, g2, [o_l], [w_out], w1, w2)
            if not last:
                o_c = _flash(q_c, k_c, v_c, 0)
                ctx = _out_mlp(ctx, mod_c, g2, [o_c], [w_out], w1, w2)
    return x
```

```python
import functools

import numpy as np
import jax
import jax.numpy as jnp
from jax import lax
from jax.experimental import pallas as pl
from jax.experimental.pallas import tpu as pltpu

F32 = jnp.float32
BF16 = jnp.bfloat16

GRID_W = 64
BLOCK = 128
WINDOW = 128
ROPE_BASE = 10000.0
EPS = 1e-6
LOG2_E = 1.4426950408889634
PROJ_ROWS = 512
MLP_ROWS = 512
SWA_BLOCKS = 8
FLASH_KEYS = 512
FLASH_UNROLL = 4
FLASH_ROWS = 128
N_MOD = 6
MOD_ROWS = 8
LANES = 128
MXU_WIDTH = 256

A_HEADS = 4
A_HEAD_DIM = 128
A_WIDTH = A_HEADS * A_HEAD_DIM
A_GATES = 4 * A_HEADS
GATE_ROWS = 8
B_HEADS = 8
B_KV_HEADS = 2
B_HEAD_DIM = 64
B_WIDTH = B_HEADS * B_HEAD_DIM
B_KV_WIDTH = B_KV_HEADS * B_HEAD_DIM
C_HEADS = 8
C_KV_HEADS = 2
C_HEAD_DIM = 128
C_WIDTH = C_HEADS * C_HEAD_DIM
C_KV_WIDTH = C_KV_HEADS * C_HEAD_DIM

VMEM_LIMIT = 56 * 1024 * 1024

NT_DIMS = (((1,), (1,)), ((), ()))


def _dot(a, b):
    return jnp.dot(a, b, preferred_element_type=F32)


def _dot_nt(a, b):
    return lax.dot_general(a, b, NT_DIMS, preferred_element_type=F32)


def _params(*sem):
    return pltpu.CompilerParams(dimension_semantics=sem, vmem_limit_bytes=VMEM_LIMIT)


def _mods_kernel(c_ref, w_ref, b_ref, o_ref):
    cf = c_ref[...]
    s = (cf * jax.nn.sigmoid(cf)).astype(BF16)
    o_ref[0] = _dot(s, w_ref[0].astype(BF16)) + b_ref[0]


def _mods(c, c_ctx, ada_w, ada_b):
    depth, d, _ = ada_w.shape
    batch = c.shape[0]
    rows = -(-(batch + 1) // 8) * 8
    cc = jnp.zeros((rows, d), F32).at[:batch].set(c).at[batch].set(c_ctx)
    out = pl.pallas_call(
        _mods_kernel,
        grid=(depth, N_MOD),
        in_specs=[
            pl.BlockSpec((rows, d), lambda l, j: (0, 0)),
            pl.BlockSpec((1, d, d), lambda l, j: (l, 0, j)),
            pl.BlockSpec((1, 1, d), lambda l, j: (l, 0, j)),
        ],
        out_specs=pl.BlockSpec((1, rows, d), lambda l, j: (l, 0, j)),
        out_shape=jax.ShapeDtypeStruct((depth, rows, N_MOD * d), F32),
        compiler_params=_params("arbitrary", "arbitrary"),
        name="ada_mods",
    )(cc, ada_w, ada_b.reshape(depth, 1, N_MOD * d))
    out = out.reshape(depth, rows, N_MOD, d)
    return jnp.pad(out, ((0, 0), (0, 0), (0, MOD_ROWS - N_MOD), (0, 0)))


def _modulated_norm(x, g, shift, scale):
    ms = jnp.mean(x * x, axis=-1, keepdims=True)
    return (x * lax.rsqrt(ms + EPS) * g) * (1.0 + scale) + shift


def _split3(a):
    hi = a.astype(BF16)
    r = a - hi.astype(F32)
    mid = r.astype(BF16)
    lo = (r - mid.astype(F32)).astype(BF16)
    return hi, mid, lo


def _head_norm_rope(xh, gain, cos, sin, group, out_scale):
    sq = xh * xh
    half = group // 2
    if group == LANES:
        ssq = jnp.sum(sq, axis=-1, keepdims=True)
    else:
        assert 2 * group == LANES
        lane = lax.broadcasted_iota(jnp.int32, xh.shape, 1)
        left = lane < group
        ssq = jnp.where(left, jnp.sum(jnp.where(left, sq, 0.0), axis=-1, keepdims=True),
                        jnp.sum(jnp.where(left, 0.0, sq), axis=-1, keepdims=True))
    xn = xh * lax.rsqrt(ssq * (1.0 / group) + EPS) * gain
    if group == LANES:
        rot = pltpu.roll(xn, half, 1)
    else:
        first = (lane & (group - 1)) < half
        rot = jnp.where(first, pltpu.roll(xn, LANES - half, 1), pltpu.roll(xn, half, 1))
    y = xn * cos + rot * sin
    if out_scale != 1.0:
        y = y * out_scale
    return y


def _rope_tables(n_tokens, head_dim, use_rope):
    reps = LANES // head_dim
    if not use_rope:
        return np.ones((n_tokens, LANES), np.float32), np.zeros((n_tokens, LANES), np.float32)
    rows = n_tokens // GRID_W
    row = np.repeat(np.arange(rows, dtype=np.float64), GRID_W)
    col = np.tile(np.arange(GRID_W, dtype=np.float64), rows)
    pairs = head_dim // 4
    inv_freq = ROPE_BASE ** (-np.arange(pairs, dtype=np.float64) / pairs)
    ang = np.concatenate([row[:, None] * inv_freq, col[:, None] * inv_freq], axis=-1)
    cos, sin = np.cos(ang), np.sin(ang)
    cos_full = np.concatenate([cos, cos], axis=-1)
    sin_signed = np.concatenate([-sin, sin], axis=-1)
    return (np.tile(cos_full, (1, reps)).astype(np.float32), np.tile(sin_signed, (1, reps)).astype(np.float32))


def _row_tile(t, target):
    tm = min(t, target)
    assert t % tm == 0
    return tm


def _proj_ab_kernel(x_ref, mod_ref, g_ref, w_ref, wt_ref, gb_ref, cos_ref, sin_ref, qn_ref, kn_ref,
                    qa_ref, kat_ref, va_ref, oa_ref, qb_ref, kb_ref, vb_ref, gr_ref, acc_scr):
    @pl.when(pl.program_id(0) == 0)
    def _():
        acc_scr[...] = jnp.zeros_like(acc_scr)

    cos, sin = cos_ref[...], sin_ref[...]
    for j in range(B_WIDTH // LANES):
        y = _head_norm_rope(acc_scr[:, LANES * j:LANES * (j + 1)], qn_ref[...], cos, sin, B_HEAD_DIM,
                            B_HEAD_DIM ** -0.5 * LOG2_E)
        qb_ref[0, :, LANES * j:LANES * (j + 1)] = y.astype(BF16)
    for j in range(B_KV_HEADS):
        y = _head_norm_rope(acc_scr[:, B_WIDTH + LANES * j:B_WIDTH + LANES * (j + 1)], kn_ref[...], cos, sin,
                            B_HEAD_DIM, 1.0)
        kb_ref[0, :, LANES * j:LANES * (j + 1)] = y.astype(BF16)

    mod = mod_ref[0]
    h = _modulated_norm(x_ref[0], g_ref[...], mod[0:1], mod[1:2]).astype(BF16)

    def mm(lo, n):
        return _dot(h, w_ref[:, lo:lo + n])

    qa_ref[0] = mm(0, A_WIDTH).astype(BF16)
    va_ref[0] = mm(A_WIDTH, A_WIDTH).astype(BF16)
    oa_ref[0] = mm(2 * A_WIDTH, A_WIDTH)
    base = 3 * A_WIDTH
    staged = B_WIDTH + B_KV_HEADS * LANES
    for lo in range(0, staged, MXU_WIDTH):
        acc_scr[:, lo:lo + MXU_WIDTH] = mm(base + lo, MXU_WIDTH)
    base += staged
    vb_ref[0] = mm(base, B_KV_HEADS * LANES).astype(BF16)
    tr = _dot_nt(wt_ref[...], h)
    kat_ref[0] = (tr[0:A_WIDTH] * A_HEAD_DIM ** -0.5).astype(BF16)
    gt = tr[A_WIDTH:] + gb_ref[...]
    typ = lax.broadcasted_iota(jnp.int32, gt.shape, 0) & (GATE_ROWS - 1)
    log_sig = jnp.minimum(gt, 0.0) - jnp.log1p(jnp.exp(-jnp.abs(gt)))
    gr_ref[0] = jnp.where((typ == 1) | (typ == 3), log_sig, gt)


def _proj_ab(x, mod, g, w, wg, gb, qn, kn, use_rope):
    batch, t, d = x.shape
    tm = _row_tile(t, PROJ_ROWS)
    nt = t // tm
    last = batch * nt - 1
    cos, sin = _rope_tables(t, B_HEAD_DIM, use_rope)
    per_batch = mod.shape[0] == batch
    cur = lambda j: jnp.minimum(j, last)
    prev = lambda j: jnp.maximum(j - 1, 0)
    const = lambda j: (0, 0)
    row_at = lambda width, tile: pl.BlockSpec((1, tm, width), lambda j: (tile(j) // nt, tile(j) % nt, 0))
    row = lambda width: row_at(width, cur)
    table = pl.BlockSpec((tm, LANES), lambda j: (prev(j) % nt, 0))
    kvw = B_KV_HEADS * LANES
    out_shape = [
        jax.ShapeDtypeStruct((batch, t, A_WIDTH), BF16),
        jax.ShapeDtypeStruct((batch, A_WIDTH, t), BF16),
        jax.ShapeDtypeStruct((batch, t, A_WIDTH), BF16),
        jax.ShapeDtypeStruct((batch, t, A_WIDTH), F32),
        jax.ShapeDtypeStruct((batch, t, B_WIDTH), BF16),
        jax.ShapeDtypeStruct((batch, t, kvw), BF16),
        jax.ShapeDtypeStruct((batch, t, kvw), BF16),
        jax.ShapeDtypeStruct((batch, A_HEADS * GATE_ROWS, t), F32),
    ]
    col = lambda height: pl.BlockSpec((1, height, tm), lambda j: (cur(j) // nt, 0, cur(j) % nt))
    out_specs = [row(A_WIDTH), col(A_WIDTH), row(A_WIDTH), row(A_WIDTH), row_at(B_WIDTH, prev), row_at(kvw, prev),
                 row(kvw), col(A_HEADS * GATE_ROWS)]
    return pl.pallas_call(
        _proj_ab_kernel,
        grid=(batch * nt + 1,),
        in_specs=[
            row(d),
            pl.BlockSpec((1, MOD_ROWS, d), lambda j: ((cur(j) // nt) if per_batch else 0, 0, 0)),
            pl.BlockSpec((1, d), const),
            pl.BlockSpec(w.shape, const),
            pl.BlockSpec(wg.shape, const),
            pl.BlockSpec(gb.shape, const),
            table, table,
            pl.BlockSpec((1, LANES), const),
            pl.BlockSpec((1, LANES), const),
        ],
        out_specs=out_specs,
        out_shape=out_shape,
        scratch_shapes=[pltpu.VMEM((tm, B_WIDTH + kvw), F32)],
        compiler_params=_params("arbitrary"),
        name="proj_ab",
    )(x, mod, g, w, wg, gb, cos, sin, qn, kn)


def _ab_weights(w_in, gate_b):
    bounds = [A_WIDTH * 4, A_WIDTH * 4 + A_GATES, A_WIDTH * 4 + A_GATES + B_WIDTH,
              A_WIDTH * 4 + A_GATES + B_WIDTH + B_KV_WIDTH]
    wa, wgate, wqb, wkb, wvb = jnp.split(w_in, bounds, axis=1)
    wqa, wka, wva_oa = wa[:, :A_WIDTH], wa[:, A_WIDTH:2 * A_WIDTH], wa[:, 2 * A_WIDTH:]

    def dup(wk):
        parts = []
        for hh in range(B_KV_HEADS):
            blk = wk[:, hh * B_HEAD_DIM:(hh + 1) * B_HEAD_DIM]
            parts += [blk] * (LANES // B_HEAD_DIM)
        return jnp.concatenate(parts, axis=1)

    w = jnp.concatenate([wqa, wva_oa, wqb, dup(wkb), dup(wvb)], axis=1).astype(BF16)
    d = w_in.shape[0]
    wg = wgate.reshape(d, 4, A_HEADS).transpose(2, 1, 0)
    wg = jnp.pad(wg, ((0, 0), (0, GATE_ROWS - 4), (0, 0))).reshape(A_HEADS * GATE_ROWS, d)
    wt = jnp.concatenate([wka.T, wg], axis=0).astype(BF16)
    gb = gate_b.reshape(4, A_HEADS).T
    gb = jnp.pad(gb, ((0, 0), (0, GATE_ROWS - 4))).reshape(A_HEADS * GATE_ROWS, 1).astype(F32)
    return w, wt, gb


def _proj_c_kernel(x_ref, mod_ref, g_ref, w_ref, cos_ref, sin_ref, qn_ref, kn_ref, q_ref, k_ref, v_ref, acc_scr):
    @pl.when(pl.program_id(0) == 0)
    def _():
        acc_scr[...] = jnp.zeros_like(acc_scr)

    cos, sin = cos_ref[...], sin_ref[...]
    q_scale = C_HEAD_DIM ** -0.5 * LOG2_E
    for j in range(C_HEADS):
        y = _head_norm_rope(acc_scr[:, LANES * j:LANES * (j + 1)], qn_ref[...], cos, sin, C_HEAD_DIM, q_scale)
        q_ref[0, :, LANES * j:LANES * (j + 1)] = y.astype(BF16)
    for j in range(C_KV_HEADS):
        y = _head_norm_rope(acc_scr[:, C_WIDTH + LANES * j:C_WIDTH + LANES * (j + 1)], kn_ref[...], cos, sin,
                            C_HEAD_DIM, 1.0)
        k_ref[0, :, LANES * j:LANES * (j + 1)] = y.astype(BF16)

    mod = mod_ref[0]
    h = _modulated_norm(x_ref[0], g_ref[...], mod[0:1], mod[1:2]).astype(BF16)
    qk_width = C_WIDTH + C_KV_WIDTH
    for lo in range(0, qk_width, MXU_WIDTH):
        acc_scr[:, lo:lo + MXU_WIDTH] = _dot(h, w_ref[:, lo:lo + MXU_WIDTH])
    v_ref[0] = _dot(h, w_ref[:, qk_width:qk_width + C_KV_WIDTH]).astype(BF16)


def _proj_c(x, mod, g, w, qn, kn, use_rope):
    batch, t, d = x.shape
    tm = _row_tile(t, PROJ_ROWS)
    nt = t // tm
    last = batch * nt - 1
    cos, sin = _rope_tables(t, C_HEAD_DIM, use_rope)
    per_batch = mod.shape[0] == batch
    cur = lambda j: jnp.minimum(j, last)
    prev = lambda j: jnp.maximum(j - 1, 0)
    const = lambda j: (0, 0)
    row_at = lambda width, tile: pl.BlockSpec((1, tm, width), lambda j: (tile(j) // nt, tile(j) % nt, 0))
    table = pl.BlockSpec((tm, LANES), lambda j: (prev(j) % nt, 0))
    return pl.pallas_call(
        _proj_c_kernel,
        grid=(batch * nt + 1,),
        in_specs=[
            row_at(d, cur),
            pl.BlockSpec((1, MOD_ROWS, d), lambda j: ((cur(j) // nt) if per_batch else 0, 0, 0)),
            pl.BlockSpec((1, d), const),
            pl.BlockSpec(w.shape, const),
            table, table,
            pl.BlockSpec((1, LANES), const),
            pl.BlockSpec((1, LANES), const),
        ],
        out_specs=[row_at(C_WIDTH, prev), row_at(C_KV_WIDTH, prev), row_at(C_KV_WIDTH, cur)],
        out_shape=[jax.ShapeDtypeStruct((batch, t, C_WIDTH), BF16),
                   jax.ShapeDtypeStruct((batch, t, C_KV_WIDTH), BF16),
                   jax.ShapeDtypeStruct((batch, t, C_KV_WIDTH), BF16)],
        scratch_shapes=[pltpu.VMEM((tm, C_WIDTH + C_KV_WIDTH), F32)],
        compiler_params=_params("arbitrary"),
        name="proj_c",
    )(x, mod, g, w, cos, sin, qn, kn)


def _mlstm_kernel(ql_ref, ktl_ref, vl_ref, oal_ref, gl_ref, qc_ref, ktc_ref, vc_ref, oac_ref, gc_ref, ng_ref,
                  hl_ref, hc_ref,
                  qs, kts, vs, gs, rowq, rowg, cols, stats, kvb, cst, mst, *, t_ctx, t_lat):
    L = BLOCK
    ncc = t_ctx // L
    ncl = t_lat // L
    nc = ncc + ncl

    qs[0:t_lat] = ql_ref[0]
    qs[t_lat:] = qc_ref[0]
    kts[:, 0:t_lat] = ktl_ref[0]
    kts[:, t_lat:] = ktc_ref[0]
    vs[0:t_lat] = vl_ref[0]
    vs[t_lat:] = vc_ref[0]
    for r in range(4):
        gs[r, 0:ncl] = gl_ref[0, r]
        gs[r, ncl:nc] = gc_ref[0, r]

    ri = lax.broadcasted_iota(jnp.int32, (L, L), 0)
    ci = lax.broadcasted_iota(jnp.int32, (L, L), 1)
    lower = ci <= ri
    upper = ci >= ri
    ones_blk = jnp.ones((L, L), BF16)

    def chunk(i):
        return pl.ds(pl.multiple_of(i * L, L), L)

    def cum(x, mat):
        hi, mid, lo = _split3(x)
        return _dot(hi, mat) + _dot(mid, mat) + _dot(lo, mat)

    li_f, lf_f, li_b, lf_b = (gs[r, 0:nc] for r in range(4))
    bcum_f = cum(lf_f, jnp.where(upper, 1.0, 0.0).astype(BF16))
    bsuf_b = cum(lf_b, jnp.where(lower, 1.0, 0.0).astype(BF16))
    bl_f = bcum_f[:, L - 1:L]
    bl_b = bsuf_b[:, 0:1]
    wl_f = bl_f - bcum_f + li_f
    wl_b = bl_b - bsuf_b + li_b
    mx_f = jnp.max(wl_f, axis=1, keepdims=True)
    mx_b = jnp.max(wl_b, axis=1, keepdims=True)
    rb_f = li_f - bcum_f
    rb_b = li_b - bsuf_b
    lane = lax.broadcasted_iota(jnp.int32, (nc, L), 1)
    pm_f, pm_b = rb_f, rb_b
    step = 1
    while step < L:
        pm_f = jnp.where(lane >= step, jnp.maximum(pm_f, pltpu.roll(pm_f, step, 1)), pm_f)
        pm_b = jnp.where(lane < L - step, jnp.maximum(pm_b, pltpu.roll(pm_b, L - step, 1)), pm_b)
        step *= 2
    for n, val in enumerate((bcum_f, bsuf_b, rb_f, rb_b, jnp.exp(wl_f - mx_f), jnp.exp(wl_b - mx_b), pm_f, pm_b)):
        rowq[n, 0:nc] = val
    for n, val in enumerate((bl_f, bl_b, mx_f, mx_b)):
        stats[n, 0:nc] = jnp.broadcast_to(val, (nc, L))

    def prep(i, _):
        kt = kts[:, chunk(i)].astype(F32)
        vaug = jnp.concatenate([vs[chunk(i), :], ones_blk], axis=1)
        for d in range(2):
            kvb[d, i] = _dot((kt * rowq[4 + d, pl.ds(i, 1), :]).astype(BF16), vaug)
        return 0

    lax.fori_loop(0, nc, prep, 0, unroll=2)

    def visit(d, j):
        if d == 0:
            return jnp.where(j < ncc, ncl + j, j - ncc)
        return nc - 1 - j

    for d in range(2):
        cst[d, visit(d, 0)] = jnp.zeros((L, 2 * L), F32)

    def scan_step(j, carry):
        new = []
        for d in range(2):
            m = carry[d]
            i = visit(d, j)
            i_next = jnp.where(j + 1 < nc, visit(d, j + 1), nc)
            mst[d, pl.ds(i, 1), :] = jnp.broadcast_to(m, (1, L))
            bl = stats[d, pl.ds(i, 1), 0:1]
            mx = stats[2 + d, pl.ds(i, 1), 0:1]
            m_new = jnp.maximum(bl + m, mx)
            cst[d, i_next] = jnp.exp(bl + m - m_new) * cst[d, i] + jnp.exp(mx - m_new) * kvb[d, i]
            new.append(m_new)
        return tuple(new)

    m0 = jnp.zeros((1, 1), F32)
    lax.fori_loop(0, nc, scan_step, (m0, m0))

    for d in range(2):
        g = jnp.maximum(mst[d, 0:nc], rowq[6 + d, 0:nc])
        rowg[d, 0:nc] = g * LOG2_E
        rowg[2 + d, 0:nc] = jnp.exp(-rowq[d, 0:nc] - g)
        rowq[2 + d, 0:nc] = rowq[2 + d, 0:nc] * LOG2_E

    def to_cols(i, _):
        r8 = jnp.concatenate([rowg[n, pl.ds(i, 1), :] for n in range(4)] + [jnp.zeros((4, L), F32)], axis=0)
        cols[chunk(i), :] = jnp.concatenate([r8, jnp.zeros((L - 8, L), F32)], axis=0).T
        return 0

    lax.fori_loop(0, nc, to_cols, 0, unroll=2)

    ng = ng_ref[...]

    def emit(i, oa, out_ref, out_rows):
        q = qs[chunk(i), :]
        vaug = jnp.concatenate([vs[chunk(i), :], ones_blk], axis=1)
        qk = _dot(q, kts[:, chunk(i)])
        colblk = cols[chunk(i), :]
        h = None
        for d in range(2):
            g = jnp.broadcast_to(colblk[:, d:d + 1], (L, L))
            floor = colblk[:, 2 + d:3 + d]
            rb = rowq[2 + d, pl.ds(i, 1), :]
            mask = lower if d == 0 else upper
            dm = jnp.where(mask, jnp.exp2(rb - g), 0.0)
            s = (qk * dm).astype(BF16)
            iw = jnp.exp2(mst[d, pl.ds(i, 1), 0:1] * LOG2_E - g)
            nd = jnp.concatenate([iw, iw], axis=1) * _dot(q, cst[d, i].astype(BF16)) + _dot(s, vaug)
            hd = nd[:, 0:L] / jnp.maximum(jnp.abs(nd[:, L:2 * L]), floor)
            h = hd if h is None else h + hd
        ms = jnp.mean(h * h, axis=-1, keepdims=True)
        y = (h * lax.rsqrt(ms + EPS) * ng) * jax.nn.sigmoid(oa)
        out_ref[0, out_rows, :] = y.astype(BF16)

    def emit_ctx(i, _):
        emit(i + ncl, oac_ref[0, chunk(i), :], hc_ref, chunk(i))
        return 0

    def emit_lat(i, _):
        emit(i, oal_ref[0, chunk(i), :], hl_ref, chunk(i))
        return 0

    lax.fori_loop(0, ncc, emit_ctx, 0, unroll=2)
    lax.fori_loop(0, ncl, emit_lat, 0, unroll=4)


def _mlstm(qa_l, kat_l, va_l, oa_l, gr_l, qa_c, kat_c, va_c, oa_c, gr_c, norm_g):
    batch, t_lat, _ = qa_l.shape
    t_ctx = qa_c.shape[1]
    t_all = t_ctx + t_lat
    nc = t_all // BLOCK
    nc_pad = -(-nc // 8) * 8
    head = lambda t: pl.BlockSpec((1, t, A_HEAD_DIM), lambda b, h: (b, 0, h))
    head_t = lambda t: pl.BlockSpec((1, A_HEAD_DIM, t), lambda b, h: (b, h, 0))
    gates = lambda t: pl.BlockSpec((1, GATE_ROWS, t // BLOCK, BLOCK), lambda b, h: (b, h, 0, 0))
    chunked = lambda g: g.reshape(batch, A_HEADS * GATE_ROWS, g.shape[2] // BLOCK, BLOCK)
    return pl.pallas_call(
        functools.partial(_mlstm_kernel, t_ctx=t_ctx, t_lat=t_lat),
        grid=(batch, A_HEADS),
        in_specs=[head(t_lat), head_t(t_lat), head(t_lat), head(t_lat), gates(t_lat),
                  head(t_ctx), head_t(t_ctx), head(t_ctx), head(t_ctx), gates(t_ctx),
                  pl.BlockSpec((1, A_HEAD_DIM), lambda b, h: (0, h))],
        out_specs=[head(t_lat), head(t_ctx)],
        out_shape=[jax.ShapeDtypeStruct((batch, t_lat, A_WIDTH), BF16),
                   jax.ShapeDtypeStruct((batch, t_ctx, A_WIDTH), BF16)],
        scratch_shapes=[
            pltpu.VMEM((t_all, A_HEAD_DIM), BF16),
            pltpu.VMEM((A_HEAD_DIM, t_all), BF16),
            pltpu.VMEM((t_all, A_HEAD_DIM), BF16),
            pltpu.VMEM((4, nc_pad, BLOCK), F32),
            pltpu.VMEM((8, nc_pad, BLOCK), F32),
            pltpu.VMEM((4, nc_pad, BLOCK), F32),
            pltpu.VMEM((t_all, LANES), F32),
            pltpu.VMEM((4, nc_pad, LANES), F32),
            pltpu.VMEM((2, nc, A_HEAD_DIM, 2 * A_HEAD_DIM), F32),
            pltpu.VMEM((2, nc + 1, A_HEAD_DIM, 2 * A_HEAD_DIM), F32),
            pltpu.VMEM((2, nc_pad, LANES), F32),
        ],
        compiler_params=_params("arbitrary", "arbitrary"),
        name="mlstm",
    )(qa_l, kat_l, va_l, oa_l, chunked(gr_l), qa_c, kat_c, va_c, oa_c, chunked(gr_c),
      norm_g.reshape(1, A_WIDTH))


def _swa_kernel(sink_ref, q_ref, *refs, t_lat, has_window, blocks):
    if has_window:
        bias_ref, k_ref, v_ref, kx_ref, vx_ref, o_ref, s_scr = refs
    else:
        kx_ref, vx_ref, o_ref, s_scr = refs
    L = BLOCK
    nb = t_lat // L
    group = B_HEADS // B_KV_HEADS
    lane = lax.broadcasted_iota(jnp.int32, (L, LANES), 1)
    lo = lane < B_HEAD_DIM
    zero = jnp.zeros((L, LANES), BF16)
    def window(u):
        qblk = pl.program_id(1) * blocks + u
        start = pl.multiple_of(jnp.clip((qblk - 1) * L, 0, t_lat - 3 * L), L)
        return qblk, pl.ds(start, 3 * L)

    def logits(c, u, kvh):
        sl = slice(kvh * LANES, (kvh + 1) * LANES)
        q = q_ref[0, u * L:(u + 1) * L, :]
        qa = q[:, (2 * kvh) * LANES:(2 * kvh + 1) * LANES]
        qb = q[:, (2 * kvh + 1) * LANES:(2 * kvh + 2) * LANES]
        q4 = jnp.concatenate([jnp.where(lo, qa, zero), jnp.where(lo, zero, qa),
                              jnp.where(lo, qb, zero), jnp.where(lo, zero, qb)], axis=0)
        sink = jnp.concatenate([jnp.full((L, 1), sink_ref[kvh * group + g] * LOG2_E, F32)
                                for g in range(group)], axis=0)
        if has_window:
            qblk, win = window(u)
            bias = bias_ref[jnp.where(qblk == 0, 0, jnp.where(qblk == nb - 1, 2, 1))]
            keys = jnp.concatenate([k_ref[0, win, sl], kx_ref[0, :, sl]], axis=0)
            s = _dot_nt(q4, keys)
            s = (s.reshape(group, L, s.shape[1]) + bias[None]).reshape(s.shape)
        else:
            s = _dot_nt(q4, kx_ref[0, :, sl])
        s_scr[c] = s
        return jnp.maximum(jnp.max(s, axis=1, keepdims=True), sink), sink

    def attend(c, u, kvh, m, sink):
        sl = slice(kvh * LANES, (kvh + 1) * LANES)
        if has_window:
            vals = jnp.concatenate([v_ref[0, window(u)[1], sl], vx_ref[0, :, sl]], axis=0)
        else:
            vals = vx_ref[0, :, sl]
        e = jnp.exp2(s_scr[c] - m)
        denom = jnp.sum(e, axis=1, keepdims=True) + jnp.exp2(sink - m)
        o4 = _dot(e.astype(BF16), vals) / denom
        rows = slice(u * L, (u + 1) * L)
        o_ref[0, rows, (2 * kvh) * LANES:(2 * kvh + 1) * LANES] = (
            jnp.where(lo, o4[0:L], o4[L:2 * L]).astype(BF16))
        o_ref[0, rows, (2 * kvh + 1) * LANES:(2 * kvh + 2) * LANES] = (
            jnp.where(lo, o4[2 * L:3 * L], o4[3 * L:4 * L]).astype(BF16))

    chains = [(u, kvh) for u in range(blocks) for kvh in range(B_KV_HEADS)]
    stats = [logits(0, *chains[0])]
    for c, (u, kvh) in enumerate(chains):
        if c + 1 < len(chains):
            stats.append(logits(c + 1, *chains[c + 1]))
        attend(c, u, kvh, *stats[c])


def _band_bias(t_ctx):
    L = BLOCK
    t = np.arange(L)[:, None]
    j = np.arange(3 * L)[None, :]
    tables = []
    for rel in (0, -L, -2 * L):
        ok = np.abs(j - t + rel) <= WINDOW
        tables.append(np.concatenate([np.where(ok, 0.0, -np.inf), np.zeros((L, t_ctx))], axis=1))
    return np.stack(tables).astype(np.float32)


def _swa(sink, q, k, v, k_ctx, v_ctx, has_window):
    batch, t, _ = q.shape
    t_ctx = k_ctx.shape[1]
    nb = t // BLOCK
    blocks = min(SWA_BLOCKS, nb)
    assert nb % blocks == 0 and (nb >= 3 or not has_window)
    kvw = B_KV_HEADS * LANES
    cur = lambda b, i: (b, i, 0)
    whole = lambda b, i: (b, 0, 0)
    ctx_spec = pl.BlockSpec((1, t_ctx, kvw), whole)
    in_specs = [pl.BlockSpec(memory_space=pltpu.SMEM), pl.BlockSpec((1, blocks * BLOCK, B_WIDTH), cur)]
    args = [sink, q]
    if has_window:
        bias = _band_bias(t_ctx)
        in_specs += [pl.BlockSpec(bias.shape, lambda b, i: (0, 0, 0)),
                     pl.BlockSpec((1, t, kvw), whole), pl.BlockSpec((1, t, kvw), whole)]
        args += [bias, k, v]
    in_specs += [ctx_spec, ctx_spec]
    args += [k_ctx, v_ctx]
    return pl.pallas_call(
        functools.partial(_swa_kernel, t_lat=t, has_window=has_window, blocks=blocks),
        grid=(batch, nb // blocks),
        in_specs=in_specs,
        out_specs=pl.BlockSpec((1, blocks * BLOCK, B_WIDTH), cur),
        scratch_shapes=[pltpu.VMEM((blocks * B_KV_HEADS, (B_HEADS // B_KV_HEADS) * BLOCK,
                                    (3 * BLOCK if has_window else 0) + t_ctx), F32)],
        out_shape=jax.ShapeDtypeStruct((batch, t, B_WIDTH), BF16),
        compiler_params=_params("arbitrary", "arbitrary"),
        name="swa" if has_window else "swa_ctx",
    )(*args)


def _flash_kernel(q_ref, k_ref, v_ref, o_ref, s0_scr, s1_scr, m0_scr, m1_scr, vaug_scr, *, chunks, tq, unroll):
    group = C_HEADS // C_KV_HEADS
    nq = q_ref.shape[1] // tq
    vaug_scr[:, 0:LANES] = v_ref[0]
    vaug_scr[:, LANES:] = jnp.ones((v_ref.shape[1], LANES), BF16)

    def q_rows(t):
        return pl.ds(pl.multiple_of(t * tq, tq), tq)

    def load_q(t):
        q = q_ref[0, q_rows(t), :]
        return jnp.concatenate([q[:, g * LANES:(g + 1) * LANES] for g in range(group)], axis=0)

    def stage_a(q4, s_ref, m, start, size):
        s = _dot_nt(q4, k_ref[0, start:start + size, :])
        s_ref[:, start:start + size] = s
        cm = jnp.max(s, axis=1, keepdims=True)
        return cm if m is None else jnp.maximum(m, cm)

    def stage_b(s_ref, m, acc, start, size):
        p = jnp.exp2(s_ref[:, start:start + size] - m)
        pv = _dot(p.astype(BF16), vaug_scr[start:start + size, :])
        return pv if acc is None else acc + pv

    def step(t, cur, nxt):
        q4 = load_q(jnp.minimum(t + 1, nq - 1))
        m_cur = cur[1][...]
        m_next, acc = None, None
        for start, size in chunks:
            m_next = stage_a(q4, nxt[0], m_next, start, size)
            acc = stage_b(cur[0], m_cur, acc, start, size)
        nxt[1][...] = m_next
        out = acc[:, 0:LANES] / acc[:, LANES:2 * LANES]
        for g in range(group):
            o_ref[0, q_rows(t), g * LANES:(g + 1) * LANES] = out[g * tq:(g + 1) * tq].astype(BF16)

    bufs = ((s0_scr, m0_scr), (s1_scr, m1_scr))
    q4 = load_q(0)
    m = None
    for start, size in chunks:
        m = stage_a(q4, s0_scr, m, start, size)
    m0_scr[...] = m

    def body(u, _):
        for k in range(unroll):
            step(unroll * u + k, bufs[k % 2], bufs[(k + 1) % 2])
        return 0

    lax.fori_loop(0, nq // unroll, body, 0)


def _key_chunks(total, target):
    chunks = []
    start = 0
    while start < total:
        size = min(target, total - start)
        chunks.append((start, size))
        start += size
    return tuple(chunks)


def _flash(q, k, v, t_ctx):
    batch, t, _ = q.shape
    tk_all = k.shape[1]
    tq = _row_tile(t, FLASH_ROWS)
    unroll = FLASH_UNROLL if (t // tq) % FLASH_UNROLL == 0 else 2
    assert unroll % 2 == 0 and (t // tq) % unroll == 0
    group = C_HEADS // C_KV_HEADS
    gw = group * C_HEAD_DIM
    chunks = (_key_chunks(t_ctx, FLASH_KEYS)
              + tuple((t_ctx + a, n) for a, n in _key_chunks(tk_all - t_ctx, FLASH_KEYS)))
    return pl.pallas_call(
        functools.partial(_flash_kernel, chunks=chunks, tq=tq, unroll=unroll),
        grid=(batch, C_KV_HEADS),
        in_specs=[pl.BlockSpec((1, t, gw), lambda b, h: (b, 0, h)),
                  pl.BlockSpec((1, tk_all, C_HEAD_DIM), lambda b, h: (b, 0, h)),
                  pl.BlockSpec((1, tk_all, C_HEAD_DIM), lambda b, h: (b, 0, h))],
        out_specs=pl.BlockSpec((1, t, gw), lambda b, h: (b, 0, h)),
        out_shape=jax.ShapeDtypeStruct((batch, t, C_WIDTH), BF16),
        scratch_shapes=[pltpu.VMEM((group * tq, tk_all), F32), pltpu.VMEM((group * tq, tk_all), F32),
                        pltpu.VMEM((group * tq, 1), F32), pltpu.VMEM((group * tq, 1), F32),
                        pltpu.VMEM((tk_all, 2 * C_HEAD_DIM), BF16)],
        compiler_params=_params("arbitrary", "arbitrary"),
        name="flash_c",
    )(q, k, v)


def _out_mlp_kernel(*refs, n_mix, ff_chunk):
    x_ref, mod_ref, g_ref = refs[0:3]
    mix_refs = refs[3:3 + n_mix]
    wo_refs = refs[3 + n_mix:3 + 2 * n_mix]
    w1_ref, w2_ref, o_ref = refs[3 + 2 * n_mix:]
    mod = mod_ref[0]
    y = None
    for a_ref, w_ref in zip(mix_refs, wo_refs):
        part = _dot(a_ref[0], w_ref[...])
        y = part if y is None else y + part
    x1 = x_ref[0] + mod[2:3] * y
    h = _modulated_norm(x1, g_ref[...], mod[3:4], mod[4:5]).astype(BF16)
    d_ff = w1_ref.shape[1]
    acc = None
    for c in range(d_ff // ff_chunk):
        a = _dot(h, w1_ref[:, c * ff_chunk:(c + 1) * ff_chunk])
        a = jnp.square(jnp.maximum(a, 0.0)).astype(BF16)
        part = _dot(a, w2_ref[c * ff_chunk:(c + 1) * ff_chunk, :])
        acc = part if acc is None else acc + part
    o_ref[0] = x1 + mod[5:6] * acc


def _out_mlp(x, mod, g, mixes, w_outs, w1, w2):
    batch, t, d = x.shape
    tm = _row_tile(t, MLP_ROWS)
    mod_map = (lambda b, i: (b, 0, 0)) if mod.shape[0] == batch else (lambda b, i: (0, 0, 0))
    const = lambda b, i: (0, 0)
    resident = lambda a: pl.BlockSpec(a.shape, const, pipeline_mode=pl.Buffered(1))
    row = lambda width: pl.BlockSpec((1, tm, width), lambda b, i: (b, i, 0))
    return pl.pallas_call(
        functools.partial(_out_mlp_kernel, n_mix=len(mixes), ff_chunk=1024),
        grid=(batch, t // tm),
        in_specs=([row(d), pl.BlockSpec((1, MOD_ROWS, d), mod_map), pl.BlockSpec((1, d), const)]
                  + [row(a.shape[2]) for a in mixes] + [resident(w) for w in w_outs]
                  + [resident(w1), resident(w2)]),
        out_specs=row(d),
        out_shape=jax.ShapeDtypeStruct((batch, t, d), F32),
        compiler_params=_params("arbitrary", "arbitrary"),
        name="out_mlp",
    )(x, mod, g, *mixes, *w_outs, w1, w2)


def kernel(x, c, ctx, c_ctx, ada_w, ada_b, norm1_g, norm2_g, ab_w_in, ab_gate_b, mlstm_norm_g, swa_q_norm_g,
           swa_k_norm_g, swa_sink, ab_w_out, c_w_in, c_q_norm_g, c_k_norm_g, c_w_out, mlp_w1, mlp_w2):
    depth = ada_w.shape[0]
    batch, _, d = x.shape
    mods = _mods(c, c_ctx, ada_w, ada_b)
    for layer in range(depth):
        last = layer == depth - 1
        mod_l = mods[layer, :batch]
        mod_c = mods[layer, batch:batch + 1]
        g1 = norm1_g[layer].reshape(1, d)
        g2 = norm2_g[layer].reshape(1, d)
        w1 = mlp_w1[layer].astype(BF16)
        w2 = mlp_w2[layer].astype(BF16)
        j = layer // 2
        if layer % 2 == 0:
            w, wt, gb = _ab_weights(ab_w_in[j], ab_gate_b[j])
            qn = jnp.tile(swa_q_norm_g[j], LANES // B_HEAD_DIM).reshape(1, LANES)
            kn = jnp.tile(swa_k_norm_g[j], LANES // B_HEAD_DIM).reshape(1, LANES)
            qa_l, kat_l, va_l, oa_l, qb_l, kb_l, vb_l, gr_l = _proj_ab(x, mod_l, g1, w, wt, gb, qn, kn, True)
            qa_c, kat_c, va_c, oa_c, qb_c, kb_c, vb_c, gr_c = _proj_ab(ctx, mod_c, g1, w, wt, gb, qn, kn, False)
            ha_l, ha_c = _mlstm(qa_l, kat_l, va_l, oa_l, gr_l, qa_c, kat_c, va_c, oa_c, gr_c, mlstm_norm_g[j])
            ob_l = _swa(swa_sink[j], qb_l, kb_l, vb_l, kb_c, vb_c, True)
            w_out = ab_w_out[j].astype(BF16)
            w_outs = [w_out[:A_WIDTH], w_out[A_WIDTH:]]
            x = _out_mlp(x, mod_l, g2, [ha_l, ob_l], w_outs, w1, w2)
            if not last:
                ob_c = _swa(swa_sink[j], qb_c, kb_c, vb_c, kb_c, vb_c, False)
                ctx = _out_mlp(ctx, mod_c, g2, [ha_c, ob_c], w_outs, w1, w2)
        else:
            w = c_w_in[j].astype(BF16)
            qn = c_q_norm_g[j].reshape(1, LANES)
            kn = c_k_norm_g[j].reshape(1, LANES)
            q_l, k_l, v_l = _proj_c(x, mod_l, g1, w, qn, kn, True)
            q_c, k_c, v_c = _proj_c(ctx, mod_c, g1, w, qn, kn, False)
            k_all = jnp.concatenate([k_c, k_l], axis=1)
            v_all = jnp.concatenate([v_c, v_l], axis=1)
            w_out = c_w_out[j].astype(BF16)
            o_l = _flash(q_l, k_all, v_all, k_c.shape[1])
            x = _out_mlp(x, mod_l, g2, [o_l], [w_out], w1, w2)
            if not last:
                o_c = _flash(q_c, k_c, v_c, 0)
                ctx = _out_mlp(ctx, mod_c, g2, [o_c], [w_out], w1, w2)
    return x
```

```python
import functools

import numpy as np
import jax
import jax.numpy as jnp
from jax import lax
from jax.experimental import pallas as pl
from jax.experimental.pallas import tpu as pltpu

F32 = jnp.float32
BF16 = jnp.bfloat16

GRID_W = 64
BLOCK = 128
WINDOW = 128
ROPE_BASE = 10000.0
EPS = 1e-6
LOG2_E = 1.4426950408889634
PROJ_ROWS = 512
MLP_ROWS = 512
SWA_BLOCKS = 8
FLASH_KEYS = 512
FLASH_UNROLL = 4
FLASH_ROWS = 128
N_MOD = 6
MOD_ROWS = 8
LANES = 128
MXU_WIDTH = 256

A_HEADS = 4
A_HEAD_DIM = 128
A_WIDTH = A_HEADS * A_HEAD_DIM
A_GATES = 4 * A_HEADS
GATE_ROWS = 8
B_HEADS = 8
B_KV_HEADS = 2
B_HEAD_DIM = 64
B_WIDTH = B_HEADS * B_HEAD_DIM
B_KV_WIDTH = B_KV_HEADS * B_HEAD_DIM
C_HEADS = 8
C_KV_HEADS = 2
C_HEAD_DIM = 128
C_WIDTH = C_HEADS * C_HEAD_DIM
C_KV_WIDTH = C_KV_HEADS * C_HEAD_DIM

VMEM_LIMIT = 56 * 1024 * 1024

NT_DIMS = (((1,), (1,)), ((), ()))


def _dot(a, b):
    return jnp.dot(a, b, preferred_element_type=F32)


def _dot_nt(a, b):
    return lax.dot_general(a, b, NT_DIMS, preferred_element_type=F32)


def _params(*sem):
    return pltpu.CompilerParams(dimension_semantics=sem, vmem_limit_bytes=VMEM_LIMIT)


def _mods_kernel(c_ref, w_ref, b_ref, o_ref):
    cf = c_ref[...]
    s = (cf * jax.nn.sigmoid(cf)).astype(BF16)
    o_ref[0] = _dot(s, w_ref[0].astype(BF16)) + b_ref[0]


def _mods(c, c_ctx, ada_w, ada_b):
    depth, d, _ = ada_w.shape
    batch = c.shape[0]
    rows = -(-(batch + 1) // 8) * 8
    cc = jnp.zeros((rows, d), F32).at[:batch].set(c).at[batch].set(c_ctx)
    out = pl.pallas_call(
        _mods_kernel,
        grid=(depth, N_MOD),
        in_specs=[
            pl.BlockSpec((rows, d), lambda l, j: (0, 0)),
            pl.BlockSpec((1, d, d), lambda l, j: (l, 0, j)),
            pl.BlockSpec((1, 1, d), lambda l, j: (l, 0, j)),
        ],
        out_specs=pl.BlockSpec((1, rows, d), lambda l, j: (l, 0, j)),
        out_shape=jax.ShapeDtypeStruct((depth, rows, N_MOD * d), F32),
        compiler_params=_params("arbitrary", "arbitrary"),
        name="ada_mods",
    )(cc, ada_w, ada_b.reshape(depth, 1, N_MOD * d))
    out = out.reshape(depth, rows, N_MOD, d)
    return jnp.pad(out, ((0, 0), (0, 0), (0, MOD_ROWS - N_MOD), (0, 0)))


def _modulated_norm(x, g, shift, scale):
    ms = jnp.mean(x * x, axis=-1, keepdims=True)
    return (x * lax.rsqrt(ms + EPS) * g) * (1.0 + scale) + shift


def _split3(a):
    hi = a.astype(BF16)
    r = a - hi.astype(F32)
    mid = r.astype(BF16)
    lo = (r - mid.astype(F32)).astype(BF16)
    return hi, mid, lo


def _head_norm_rope(xh, gain, cos, sin, group, out_scale):
    sq = xh * xh
    half = group // 2
    if group == LANES:
        ssq = jnp.sum(sq, axis=-1, keepdims=True)
    else:
        assert 2 * group == LANES
        lane = lax.broadcasted_iota(jnp.int32, xh.shape, 1)
        left = lane < group
        ssq = jnp.where(left, jnp.sum(jnp.where(left, sq, 0.0), axis=-1, keepdims=True),
                        jnp.sum(jnp.where(left, 0.0, sq), axis=-1, keepdims=True))
    xn = xh * lax.rsqrt(ssq * (1.0 / group) + EPS) * gain
    if group == LANES:
        rot = pltpu.roll(xn, half, 1)
    else:
        first = (lane & (group - 1)) < half
        rot = jnp.where(first, pltpu.roll(xn, LANES - half, 1), pltpu.roll(xn, half, 1))
    y = xn * cos + rot * sin
    if out_scale != 1.0:
        y = y * out_scale
    return y


def _rope_tables(n_tokens, head_dim, use_rope):
    reps = LANES // head_dim
    if not use_rope:
        return np.ones((n_tokens, LANES), np.float32), np.zeros((n_tokens, LANES), np.float32)
    rows = n_tokens // GRID_W
    row = np.repeat(np.arange(rows, dtype=np.float64), GRID_W)
    col = np.tile(np.arange(GRID_W, dtype=np.float64), rows)
    pairs = head_dim // 4
    inv_freq = ROPE_BASE ** (-np.arange(pairs, dtype=np.float64) / pairs)
    ang = np.concatenate([row[:, None] * inv_freq, col[:, None] * inv_freq], axis=-1)
    cos, sin = np.cos(ang), np.sin(ang)
    cos_full = np.concatenate([cos, cos], axis=-1)
    sin_signed = np.concatenate([-sin, sin], axis=-1)
    return (np.tile(cos_full, (1, reps)).astype(np.float32), np.tile(sin_signed, (1, reps)).astype(np.float32))


def _row_tile(t, target):
    tm = min(t, target)
    assert t % tm == 0
    return tm


def _proj_ab_kernel(x_ref, mod_ref, g_ref, w_ref, wt_ref, gb_ref, cos_ref, sin_ref, qn_ref, kn_ref,
                    qa_ref, kat_ref, va_ref, oa_ref, qb_ref, kb_ref, vb_ref, gr_ref, acc_scr):
    @pl.when(pl.program_id(0) == 0)
    def _():
        acc_scr[...] = jnp.zeros_like(acc_scr)

    cos, sin = cos_ref[...], sin_ref[...]
    for j in range(B_WIDTH // LANES):
        y = _head_norm_rope(acc_scr[:, LANES * j:LANES * (j + 1)], qn_ref[...], cos, sin, B_HEAD_DIM,
                            B_HEAD_DIM ** -0.5 * LOG2_E)
        qb_ref[0, :, LANES * j:LANES * (j + 1)] = y.astype(BF16)
    for j in range(B_KV_HEADS):
        y = _head_norm_rope(acc_scr[:, B_WIDTH + LANES * j:B_WIDTH + LANES * (j + 1)], kn_ref[...], cos, sin,
                            B_HEAD_DIM, 1.0)
        kb_ref[0, :, LANES * j:LANES * (j + 1)] = y.astype(BF16)

    mod = mod_ref[0]
    h = _modulated_norm(x_ref[0], g_ref[...], mod[0:1], mod[1:2]).astype(BF16)

    def mm(lo, n):
        return _dot(h, w_ref[:, lo:lo + n])

    qa_ref[0] = mm(0, A_WIDTH).astype(BF16)
    va_ref[0] = mm(A_WIDTH, A_WIDTH).astype(BF16)
    oa_ref[0] = mm(2 * A_WIDTH, A_WIDTH)
    base = 3 * A_WIDTH
    staged = B_WIDTH + B_KV_HEADS * LANES
    for lo in range(0, staged, MXU_WIDTH):
        acc_scr[:, lo:lo + MXU_WIDTH] = mm(base + lo, MXU_WIDTH)
    base += staged
    vb_ref[0] = mm(base, B_KV_HEADS * LANES).astype(BF16)
    tr = _dot_nt(wt_ref[...], h)
    kat_ref[0] = (tr[0:A_WIDTH] * A_HEAD_DIM ** -0.5).astype(BF16)
    gt = tr[A_WIDTH:] + gb_ref[...]
    typ = lax.broadcasted_iota(jnp.int32, gt.shape, 0) & (GATE_ROWS - 1)
    log_sig = jnp.minimum(gt, 0.0) - jnp.log1p(jnp.exp(-jnp.abs(gt)))
    gr_ref[0] = jnp.where((typ == 1) | (typ == 3), log_sig, gt)


def _proj_ab(x, mod, g, w, wg, gb, qn, kn, use_rope):
    batch, t, d = x.shape
    tm = _row_tile(t, PROJ_ROWS)
    nt = t // tm
    last = batch * nt - 1
    cos, sin = _rope_tables(t, B_HEAD_DIM, use_rope)
    per_batch = mod.shape[0] == batch
    cur = lambda j: jnp.minimum(j, last)
    prev = lambda j: jnp.maximum(j - 1, 0)
    const = lambda j: (0, 0)
    row_at = lambda width, tile: pl.BlockSpec((1, tm, width), lambda j: (tile(j) // nt, tile(j) % nt, 0))
    row = lambda width: row_at(width, cur)
    table = pl.BlockSpec((tm, LANES), lambda j: (prev(j) % nt, 0))
    kvw = B_KV_HEADS * LANES
    out_shape = [
        jax.ShapeDtypeStruct((batch, t, A_WIDTH), BF16),
        jax.ShapeDtypeStruct((batch, A_WIDTH, t), BF16),
        jax.ShapeDtypeStruct((batch, t, A_WIDTH), BF16),
        jax.ShapeDtypeStruct((batch, t, A_WIDTH), F32),
        jax.ShapeDtypeStruct((batch, t, B_WIDTH), BF16),
        jax.ShapeDtypeStruct((batch, t, kvw), BF16),
        jax.ShapeDtypeStruct((batch, t, kvw), BF16),
        jax.ShapeDtypeStruct((batch, A_HEADS * GATE_ROWS, t), F32),
    ]
    col = lambda height: pl.BlockSpec((1, height, tm), lambda j: (cur(j) // nt, 0, cur(j) % nt))
    out_specs = [row(A_WIDTH), col(A_WIDTH), row(A_WIDTH), row(A_WIDTH), row_at(B_WIDTH, prev), row_at(kvw, prev),
                 row(kvw), col(A_HEADS * GATE_ROWS)]
    return pl.pallas_call(
        _proj_ab_kernel,
        grid=(batch * nt + 1,),
        in_specs=[
            row(d),
            pl.BlockSpec((1, MOD_ROWS, d), lambda j: ((cur(j) // nt) if per_batch else 0, 0, 0)),
            pl.BlockSpec((1, d), const),
            pl.BlockSpec(w.shape, const),
            pl.BlockSpec(wg.shape, const),
            pl.BlockSpec(gb.shape, const),
            table, table,
            pl.BlockSpec((1, LANES), const),
            pl.BlockSpec((1, LANES), const),
        ],
        out_specs=out_specs,
        out_shape=out_shape,
        scratch_shapes=[pltpu.VMEM((tm, B_WIDTH + kvw), F32)],
        compiler_params=_params("arbitrary"),
        name="proj_ab",
    )(x, mod, g, w, wg, gb, cos, sin, qn, kn)


def _ab_weights(w_in, gate_b):
    bounds = [A_WIDTH * 4, A_WIDTH * 4 + A_GATES, A_WIDTH * 4 + A_GATES + B_WIDTH,
              A_WIDTH * 4 + A_GATES + B_WIDTH + B_KV_WIDTH]
    wa, wgate, wqb, wkb, wvb = jnp.split(w_in, bounds, axis=1)
    wqa, wka, wva_oa = wa[:, :A_WIDTH], wa[:, A_WIDTH:2 * A_WIDTH], wa[:, 2 * A_WIDTH:]

    def dup(wk):
        parts = []
        for hh in range(B_KV_HEADS):
            blk = wk[:, hh * B_HEAD_DIM:(hh + 1) * B_HEAD_DIM]
            parts += [blk] * (LANES // B_HEAD_DIM)
        return jnp.concatenate(parts, axis=1)

    w = jnp.concatenate([wqa, wva_oa, wqb, dup(wkb), dup(wvb)], axis=1).astype(BF16)
    d = w_in.shape[0]
    wg = wgate.reshape(d, 4, A_HEADS).transpose(2, 1, 0)
    wg = jnp.pad(wg, ((0, 0), (0, GATE_ROWS - 4), (0, 0))).reshape(A_HEADS * GATE_ROWS, d)
    wt = jnp.concatenate([wka.T, wg], axis=0).astype(BF16)
    gb = gate_b.reshape(4, A_HEADS).T
    gb = jnp.pad(gb, ((0, 0), (0, GATE_ROWS - 4))).reshape(A_HEADS * GATE_ROWS, 1).astype(F32)
    return w, wt, gb


def _proj_c_kernel(x_ref, mod_ref, g_ref, w_ref, cos_ref, sin_ref, qn_ref, kn_ref, q_ref, k_ref, v_ref, acc_scr):
    @pl.when(pl.program_id(0) == 0)
    def _():
        acc_scr[...] = jnp.zeros_like(acc_scr)

    cos, sin = cos_ref[...], sin_ref[...]
    q_scale = C_HEAD_DIM ** -0.5 * LOG2_E
    for j in range(C_HEADS):
        y = _head_norm_rope(acc_scr[:, LANES * j:LANES * (j + 1)], qn_ref[...], cos, sin, C_HEAD_DIM, q_scale)
        q_ref[0, :, LANES * j:LANES * (j + 1)] = y.astype(BF16)
    for j in range(C_KV_HEADS):
        y = _head_norm_rope(acc_scr[:, C_WIDTH + LANES * j:C_WIDTH + LANES * (j + 1)], kn_ref[...], cos, sin,
                            C_HEAD_DIM, 1.0)
        k_ref[0, :, LANES * j:LANES * (j + 1)] = y.astype(BF16)

    mod = mod_ref[0]
    h = _modulated_norm(x_ref[0], g_ref[...], mod[0:1], mod[1:2]).astype(BF16)
    qk_width = C_WIDTH + C_KV_WIDTH
    for lo in range(0, qk_width, MXU_WIDTH):
        acc_scr[:, lo:lo + MXU_WIDTH] = _dot(h, w_ref[:, lo:lo + MXU_WIDTH])
    v_ref[0] = _dot(h, w_ref[:, qk_width:qk_width + C_KV_WIDTH]).astype(BF16)


def _proj_c(x, mod, g, w, qn, kn, use_rope):
    batch, t, d = x.shape
    tm = _row_tile(t, PROJ_ROWS)
    nt = t // tm
    last = batch * nt - 1
    cos, sin = _rope_tables(t, C_HEAD_DIM, use_rope)
    per_batch = mod.shape[0] == batch
    cur = lambda j: jnp.minimum(j, last)
    prev = lambda j: jnp.maximum(j - 1, 0)
    const = lambda j: (0, 0)
    row_at = lambda width, tile: pl.BlockSpec((1, tm, width), lambda j: (tile(j) // nt, tile(j) % nt, 0))
    table = pl.BlockSpec((tm, LANES), lambda j: (prev(j) % nt, 0))
    return pl.pallas_call(
        _proj_c_kernel,
        grid=(batch * nt + 1,),
        in_specs=[
            row_at(d, cur),
            pl.BlockSpec((1, MOD_ROWS, d), lambda j: ((cur(j) // nt) if per_batch else 0, 0, 0)),
            pl.BlockSpec((1, d), const),
            pl.BlockSpec(w.shape, const),
            table, table,
            pl.BlockSpec((1, LANES), const),
            pl.BlockSpec((1, LANES), const),
        ],
        out_specs=[row_at(C_WIDTH, prev), row_at(C_KV_WIDTH, prev), row_at(C_KV_WIDTH, cur)],
        out_shape=[jax.ShapeDtypeStruct((batch, t, C_WIDTH), BF16),
                   jax.ShapeDtypeStruct((batch, t, C_KV_WIDTH), BF16),
                   jax.ShapeDtypeStruct((batch, t, C_KV_WIDTH), BF16)],
        scratch_shapes=[pltpu.VMEM((tm, C_WIDTH + C_KV_WIDTH), F32)],
        compiler_params=_params("arbitrary"),
        name="proj_c",
    )(x, mod, g, w, cos, sin, qn, kn)


def _mlstm_kernel(ql_ref, ktl_ref, vl_ref, oal_ref, gl_ref, qc_ref, ktc_ref, vc_ref, oac_ref, gc_ref, ng_ref,
                  hl_ref, hc_ref,
                  qs, kts, vs, gs, rowq, rowg, cols, stats, kvb, cst, mst, *, t_ctx, t_lat):
    L = BLOCK
    ncc = t_ctx // L
    ncl = t_lat // L
    nc = ncc + ncl

    qs[0:t_lat] = ql_ref[0]
    qs[t_lat:] = qc_ref[0]
    kts[:, 0:t_lat] = ktl_ref[0]
    kts[:, t_lat:] = ktc_ref[0]
    vs[0:t_lat] = vl_ref[0]
    vs[t_lat:] = vc_ref[0]
    for r in range(4):
        gs[r, 0:ncl] = gl_ref[0, r]
        gs[r, ncl:nc] = gc_ref[0, r]

    ri = lax.broadcasted_iota(jnp.int32, (L, L), 0)
    ci = lax.broadcasted_iota(jnp.int32, (L, L), 1)
    lower = ci <= ri
    upper = ci >= ri
    ones_blk = jnp.ones((L, L), BF16)

    def chunk(i):
        return pl.ds(pl.multiple_of(i * L, L), L)

    def cum(x, mat):
        hi, mid, lo = _split3(x)
        return _dot(hi, mat) + _dot(mid, mat) + _dot(lo, mat)

    li_f, lf_f, li_b, lf_b = (gs[r, 0:nc] for r in range(4))
    bcum_f = cum(lf_f, jnp.where(upper, 1.0, 0.0).astype(BF16))
    bsuf_b = cum(lf_b, jnp.where(lower, 1.0, 0.0).astype(BF16))
    bl_f = bcum_f[:, L - 1:L]
    bl_b = bsuf_b[:, 0:1]
    wl_f = bl_f - bcum_f + li_f
    wl_b = bl_b - bsuf_b + li_b
    mx_f = jnp.max(wl_f, axis=1, keepdims=True)
    mx_b = jnp.max(wl_b, axis=1, keepdims=True)
    rb_f = li_f - bcum_f
    rb_b = li_b - bsuf_b
    lane = lax.broadcasted_iota(jnp.int32, (nc, L), 1)
    pm_f, pm_b = rb_f, rb_b
    step = 1
    while step < L:
        pm_f = jnp.where(lane >= step, jnp.maximum(pm_f, pltpu.roll(pm_f, step, 1)), pm_f)
        pm_b = jnp.where(lane < L - step, jnp.maximum(pm_b, pltpu.roll(pm_b, L - step, 1)), pm_b)
        step *= 2
    for n, val in enumerate((bcum_f, bsuf_b, rb_f, rb_b, jnp.exp(wl_f - mx_f), jnp.exp(wl_b - mx_b), pm_f, pm_b)):
        rowq[n, 0:nc] = val
    for n, val in enumerate((bl_f, bl_b, mx_f, mx_b)):
        stats[n, 0:nc] = jnp.broadcast_to(val, (nc, L))

    def prep(i, _):
        kt = kts[:, chunk(i)].astype(F32)
        vaug = jnp.concatenate([vs[chunk(i), :], ones_blk], axis=1)
        for d in range(2):
            kvb[d, i] = _dot((kt * rowq[4 + d, pl.ds(i, 1), :]).astype(BF16), vaug)
        return 0

    lax.fori_loop(0, nc, prep, 0, unroll=True)

    def visit(d, j):
        if d == 0:
            return jnp.where(j < ncc, ncl + j, j - ncc)
        return nc - 1 - j

    for d in range(2):
        cst[d, visit(d, 0)] = jnp.zeros((L, 2 * L), F32)

    def scan_step(j, carry):
        new = []
        for d in range(2):
            m = carry[d]
            i = visit(d, j)
            i_next = jnp.where(j + 1 < nc, visit(d, j + 1), nc)
            mst[d, pl.ds(i, 1), :] = jnp.broadcast_to(m, (1, L))
            bl = stats[d, pl.ds(i, 1), 0:1]
            mx = stats[2 + d, pl.ds(i, 1), 0:1]
            m_new = jnp.maximum(bl + m, mx)
            cst[d, i_next] = jnp.exp(bl + m - m_new) * cst[d, i] + jnp.exp(mx - m_new) * kvb[d, i]
            new.append(m_new)
        return tuple(new)

    m0 = jnp.zeros((1, 1), F32)
    lax.fori_loop(0, nc, scan_step, (m0, m0), unroll=True)

    for d in range(2):
        g = jnp.maximum(mst[d, 0:nc], rowq[6 + d, 0:nc])
        rowg[d, 0:nc] = g * LOG2_E
        rowg[2 + d, 0:nc] = jnp.exp(-rowq[d, 0:nc] - g)
        rowq[2 + d, 0:nc] = rowq[2 + d, 0:nc] * LOG2_E

    def to_cols(i, _):
        r8 = jnp.concatenate([rowg[n, pl.ds(i, 1), :] for n in range(4)] + [jnp.zeros((4, L), F32)], axis=0)
        cols[chunk(i), :] = jnp.concatenate([r8, jnp.zeros((L - 8, L), F32)], axis=0).T
        return 0

    lax.fori_loop(0, nc, to_cols, 0, unroll=True)

    ng = ng_ref[...]

    def emit(i, oa, out_ref, out_rows):
        q = qs[chunk(i), :]
        vaug = jnp.concatenate([vs[chunk(i), :], ones_blk], axis=1)
        qk = _dot(q, kts[:, chunk(i)])
        colblk = cols[chunk(i), :]
        h = None
        for d in range(2):
            g = jnp.broadcast_to(colblk[:, d:d + 1], (L, L))
            floor = colblk[:, 2 + d:3 + d]
            rb = rowq[2 + d, pl.ds(i, 1), :]
            mask = lower if d == 0 else upper
            dm = jnp.where(mask, jnp.exp2(rb - g), 0.0)
            s = (qk * dm).astype(BF16)
            iw = jnp.exp2(mst[d, pl.ds(i, 1), 0:1] * LOG2_E - g)
            nd = jnp.concatenate([iw, iw], axis=1) * _dot(q, cst[d, i].astype(BF16)) + _dot(s, vaug)
            hd = nd[:, 0:L] / jnp.maximum(jnp.abs(nd[:, L:2 * L]), floor)
            h = hd if h is None else h + hd
        ms = jnp.mean(h * h, axis=-1, keepdims=True)
        y = (h * lax.rsqrt(ms + EPS) * ng) * jax.nn.sigmoid(oa)
        out_ref[0, out_rows, :] = y.astype(BF16)

    def emit_ctx(i, _):
        emit(i + ncl, oac_ref[0, chunk(i), :], hc_ref, chunk(i))
        return 0

    def emit_lat(i, _):
        emit(i, oal_ref[0, chunk(i), :], hl_ref, chunk(i))
        return 0

    lax.fori_loop(0, ncc, emit_ctx, 0, unroll=2)
    lax.fori_loop(0, ncl, emit_lat, 0, unroll=4)


def _mlstm(qa_l, kat_l, va_l, oa_l, gr_l, qa_c, kat_c, va_c, oa_c, gr_c, norm_g):
    batch, t_lat, _ = qa_l.shape
    t_ctx = qa_c.shape[1]
    t_all = t_ctx + t_lat
    nc = t_all // BLOCK
    nc_pad = -(-nc // 8) * 8
    head = lambda t: pl.BlockSpec((1, t, A_HEAD_DIM), lambda b, h: (b, 0, h))
    head_t = lambda t: pl.BlockSpec((1, A_HEAD_DIM, t), lambda b, h: (b, h, 0))
    gates = lambda t: pl.BlockSpec((1, GATE_ROWS, t // BLOCK, BLOCK), lambda b, h: (b, h, 0, 0))
    chunked = lambda g: g.reshape(batch, A_HEADS * GATE_ROWS, g.shape[2] // BLOCK, BLOCK)
    return pl.pallas_call(
        functools.partial(_mlstm_kernel, t_ctx=t_ctx, t_lat=t_lat),
        grid=(batch, A_HEADS),
        in_specs=[head(t_lat), head_t(t_lat), head(t_lat), head(t_lat), gates(t_lat),
                  head(t_ctx), head_t(t_ctx), head(t_ctx), head(t_ctx), gates(t_ctx),
                  pl.BlockSpec((1, A_HEAD_DIM), lambda b, h: (0, h))],
        out_specs=[head(t_lat), head(t_ctx)],
        out_shape=[jax.ShapeDtypeStruct((batch, t_lat, A_WIDTH), BF16),
                   jax.ShapeDtypeStruct((batch, t_ctx, A_WIDTH), BF16)],
        scratch_shapes=[
            pltpu.VMEM((t_all, A_HEAD_DIM), BF16),
            pltpu.VMEM((A_HEAD_DIM, t_all), BF16),
            pltpu.VMEM((t_all, A_HEAD_DIM), BF16),
            pltpu.VMEM((4, nc_pad, BLOCK), F32),
            pltpu.VMEM((8, nc_pad, BLOCK), F32),
            pltpu.VMEM((4, nc_pad, BLOCK), F32),
            pltpu.VMEM((t_all, LANES), F32),
            pltpu.VMEM((4, nc_pad, LANES), F32),
            pltpu.VMEM((2, nc, A_HEAD_DIM, 2 * A_HEAD_DIM), F32),
            pltpu.VMEM((2, nc + 1, A_HEAD_DIM, 2 * A_HEAD_DIM), F32),
            pltpu.VMEM((2, nc_pad, LANES), F32),
        ],
        compiler_params=_params("arbitrary", "arbitrary"),
        name="mlstm",
    )(qa_l, kat_l, va_l, oa_l, chunked(gr_l), qa_c, kat_c, va_c, oa_c, chunked(gr_c),
      norm_g.reshape(1, A_WIDTH))


def _swa_kernel(sink_ref, q_ref, *refs, t_lat, has_window, blocks):
    if has_window:
        bias_ref, k_ref, v_ref, kx_ref, vx_ref, o_ref, s_scr = refs
    else:
        kx_ref, vx_ref, o_ref, s_scr = refs
    L = BLOCK
    nb = t_lat // L
    group = B_HEADS // B_KV_HEADS
    lane = lax.broadcasted_iota(jnp.int32, (L, LANES), 1)
    lo = lane < B_HEAD_DIM
    zero = jnp.zeros((L, LANES), BF16)
    def window(u):
        qblk = pl.program_id(1) * blocks + u
        start = pl.multiple_of(jnp.clip((qblk - 1) * L, 0, t_lat - 3 * L), L)
        return qblk, pl.ds(start, 3 * L)

    def logits(c, u, kvh):
        sl = slice(kvh * LANES, (kvh + 1) * LANES)
        q = q_ref[0, u * L:(u + 1) * L, :]
        qa = q[:, (2 * kvh) * LANES:(2 * kvh + 1) * LANES]
        qb = q[:, (2 * kvh + 1) * LANES:(2 * kvh + 2) * LANES]
        q4 = jnp.concatenate([jnp.where(lo, qa, zero), jnp.where(lo, zero, qa),
                              jnp.where(lo, qb, zero), jnp.where(lo, zero, qb)], axis=0)
        sink = jnp.concatenate([jnp.full((L, 1), sink_ref[kvh * group + g] * LOG2_E, F32)
                                for g in range(group)], axis=0)
        if has_window:
            qblk, win = window(u)
            bias = bias_ref[jnp.where(qblk == 0, 0, jnp.where(qblk == nb - 1, 2, 1))]
            keys = jnp.concatenate([k_ref[0, win, sl], kx_ref[0, :, sl]], axis=0)
            s = _dot_nt(q4, keys)
            s = (s.reshape(group, L, s.shape[1]) + bias[None]).reshape(s.shape)
        else:
            s = _dot_nt(q4, kx_ref[0, :, sl])
        s_scr[c] = s
        return jnp.maximum(jnp.max(s, axis=1, keepdims=True), sink), sink

    def attend(c, u, kvh, m, sink):
        sl = slice(kvh * LANES, (kvh + 1) * LANES)
        if has_window:
            vals = jnp.concatenate([v_ref[0, window(u)[1], sl], vx_ref[0, :, sl]], axis=0)
        else:
            vals = vx_ref[0, :, sl]
        e = jnp.exp2(s_scr[c] - m)
        denom = jnp.sum(e, axis=1, keepdims=True) + jnp.exp2(sink - m)
        o4 = _dot(e.astype(BF16), vals) / denom
        rows = slice(u * L, (u + 1) * L)
        o_ref[0, rows, (2 * kvh) * LANES:(2 * kvh + 1) * LANES] = (
            jnp.where(lo, o4[0:L], o4[L:2 * L]).astype(BF16))
        o_ref[0, rows, (2 * kvh + 1) * LANES:(2 * kvh + 2) * LANES] = (
            jnp.where(lo, o4[2 * L:3 * L], o4[3 * L:4 * L]).astype(BF16))

    chains = [(u, kvh) for u in range(blocks) for kvh in range(B_KV_HEADS)]
    stats = [logits(0, *chains[0])]
    for c, (u, kvh) in enumerate(chains):
        if c + 1 < len(chains):
            stats.append(logits(c + 1, *chains[c + 1]))
        attend(c, u, kvh, *stats[c])


def _band_bias(t_ctx):
    L = BLOCK
    t = np.arange(L)[:, None]
    j = np.arange(3 * L)[None, :]
    tables = []
    for rel in (0, -L, -2 * L):
        ok = np.abs(j - t + rel) <= WINDOW
        tables.append(np.concatenate([np.where(ok, 0.0, -np.inf), np.zeros((L, t_ctx))], axis=1))
    return np.stack(tables).astype(np.float32)


def _swa(sink, q, k, v, k_ctx, v_ctx, has_window):
    batch, t, _ = q.shape
    t_ctx = k_ctx.shape[1]
    nb = t // BLOCK
    blocks = min(SWA_BLOCKS, nb)
    assert nb % blocks == 0 and (nb >= 3 or not has_window)
    kvw = B_KV_HEADS * LANES
    cur = lambda b, i: (b, i, 0)
    whole = lambda b, i: (b, 0, 0)
    ctx_spec = pl.BlockSpec((1, t_ctx, kvw), whole)
    in_specs = [pl.BlockSpec(memory_space=pltpu.SMEM), pl.BlockSpec((1, blocks * BLOCK, B_WIDTH), cur)]
    args = [sink, q]
    if has_window:
        bias = _band_bias(t_ctx)
        in_specs += [pl.BlockSpec(bias.shape, lambda b, i: (0, 0, 0)),
                     pl.BlockSpec((1, t, kvw), whole), pl.BlockSpec((1, t, kvw), whole)]
        args += [bias, k, v]
    in_specs += [ctx_spec, ctx_spec]
    args += [k_ctx, v_ctx]
    return pl.pallas_call(
        functools.partial(_swa_kernel, t_lat=t, has_window=has_window, blocks=blocks),
        grid=(batch, nb // blocks),
        in_specs=in_specs,
        out_specs=pl.BlockSpec((1, blocks * BLOCK, B_WIDTH), cur),
        scratch_shapes=[pltpu.VMEM((blocks * B_KV_HEADS, (B_HEADS // B_KV_HEADS) * BLOCK,
                                    (3 * BLOCK if has_window else 0) + t_ctx), F32)],
        out_shape=jax.ShapeDtypeStruct((batch, t, B_WIDTH), BF16),
        compiler_params=_params("arbitrary", "arbitrary"),
        name="swa" if has_window else "swa_ctx",
    )(*args)


def _flash_kernel(q_ref, k_ref, v_ref, o_ref, s0_scr, s1_scr, m0_scr, m1_scr, vaug_scr, *, chunks, tq, unroll):
    group = C_HEADS // C_KV_HEADS
    nq = q_ref.shape[1] // tq
    vaug_scr[:, 0:LANES] = v_ref[0]
    vaug_scr[:, LANES:] = jnp.ones((v_ref.shape[1], LANES), BF16)

    def q_rows(t):
        return pl.ds(pl.multiple_of(t * tq, tq), tq)

    def load_q(t):
        q = q_ref[0, q_rows(t), :]
        return jnp.concatenate([q[:, g * LANES:(g + 1) * LANES] for g in range(group)], axis=0)

    def stage_a(q4, s_ref, m, start, size):
        s = _dot_nt(q4, k_ref[0, start:start + size, :])
        s_ref[:, start:start + size] = s
        cm = jnp.max(s, axis=1, keepdims=True)
        return cm if m is None else jnp.maximum(m, cm)

    def stage_b(s_ref, m, acc, start, size):
        p = jnp.exp2(s_ref[:, start:start + size] - m)
        pv = _dot(p.astype(BF16), vaug_scr[start:start + size, :])
        return pv if acc is None else acc + pv

    def step(t, cur, nxt):
        q4 = load_q(jnp.minimum(t + 1, nq - 1))
        m_cur = cur[1][...]
        m_next, acc = None, None
        for start, size in chunks:
            m_next = stage_a(q4, nxt[0], m_next, start, size)
            acc = stage_b(cur[0], m_cur, acc, start, size)
        nxt[1][...] = m_next
        out = acc[:, 0:LANES] / acc[:, LANES:2 * LANES]
        for g in range(group):
            o_ref[0, q_rows(t), g * LANES:(g + 1) * LANES] = out[g * tq:(g + 1) * tq].astype(BF16)

    bufs = ((s0_scr, m0_scr), (s1_scr, m1_scr))
    q4 = load_q(0)
    m = None
    for start, size in chunks:
        m = stage_a(q4, s0_scr, m, start, size)
    m0_scr[...] = m

    def body(u, _):
        for k in range(unroll):
            step(unroll * u + k, bufs[k % 2], bufs[(k + 1) % 2])
        return 0

    lax.fori_loop(0, nq // unroll, body, 0)


def _key_chunks(total, target):
    chunks = []
    start = 0
    while start < total:
        size = min(target, total - start)
        chunks.append((start, size))
        start += size
    return tuple(chunks)


def _flash(q, k, v, t_ctx):
    batch, t, _ = q.shape
    tk_all = k.shape[1]
    tq = _row_tile(t, FLASH_ROWS)
    unroll = FLASH_UNROLL if (t // tq) % FLASH_UNROLL == 0 else 2
    assert unroll % 2 == 0 and (t // tq) % unroll == 0
    group = C_HEADS // C_KV_HEADS
    gw = group * C_HEAD_DIM
    chunks = (_key_chunks(t_ctx, FLASH_KEYS)
              + tuple((t_ctx + a, n) for a, n in _key_chunks(tk_all - t_ctx, FLASH_KEYS)))
    return pl.pallas_call(
        functools.partial(_flash_kernel, chunks=chunks, tq=tq, unroll=unroll),
        grid=(batch, C_KV_HEADS),
        in_specs=[pl.BlockSpec((1, t, gw), lambda b, h: (b, 0, h)),
                  pl.BlockSpec((1, tk_all, C_HEAD_DIM), lambda b, h: (b, 0, h)),
                  pl.BlockSpec((1, tk_all, C_HEAD_DIM), lambda b, h: (b, 0, h))],
        out_specs=pl.BlockSpec((1, t, gw), lambda b, h: (b, 0, h)),
        out_shape=jax.ShapeDtypeStruct((batch, t, C_WIDTH), BF16),
        scratch_shapes=[pltpu.VMEM((group * tq, tk_all), F32), pltpu.VMEM((group * tq, tk_all), F32),
                        pltpu.VMEM((group * tq, 1), F32), pltpu.VMEM((group * tq, 1), F32),
                        pltpu.VMEM((tk_all, 2 * C_HEAD_DIM), BF16)],
        compiler_params=_params("arbitrary", "arbitrary"),
        name="flash_c",
    )(q, k, v)


def _out_mlp_kernel(*refs, n_mix, ff_chunk):
    x_ref, mod_ref, g_ref = refs[0:3]
    mix_refs = refs[3:3 + n_mix]
    wo_refs = refs[3 + n_mix:3 + 2 * n_mix]
    w1_ref, w2_ref, o_ref = refs[3 + 2 * n_mix:]
    mod = mod_ref[0]
    y = None
    for a_ref, w_ref in zip(mix_refs, wo_refs):
        part = _dot(a_ref[0], w_ref[...])
        y = part if y is None else y + part
    x1 = x_ref[0] + mod[2:3] * y
    h = _modulated_norm(x1, g_ref[...], mod[3:4], mod[4:5]).astype(BF16)
    d_ff = w1_ref.shape[1]
    acc = None
    for c in range(d_ff // ff_chunk):
        a = _dot(h, w1_ref[:, c * ff_chunk:(c + 1) * ff_chunk])
        a = jnp.square(jnp.maximum(a, 0.0)).astype(BF16)
        part = _dot(a, w2_ref[c * ff_chunk:(c + 1) * ff_chunk, :])
        acc = part if acc is None else acc + part
    o_ref[0] = x1 + mod[5:6] * acc


def _out_mlp(x, mod, g, mixes, w_outs, w1, w2):
    batch, t, d = x.shape
    tm = _row_tile(t, MLP_ROWS)
    mod_map = (lambda b, i: (b, 0, 0)) if mod.shape[0] == batch else (lambda b, i: (0, 0, 0))
    const = lambda b, i: (0, 0)
    resident = lambda a: pl.BlockSpec(a.shape, const, pipeline_mode=pl.Buffered(1))
    row = lambda width: pl.BlockSpec((1, tm, width), lambda b, i: (b, i, 0))
    return pl.pallas_call(
        functools.partial(_out_mlp_kernel, n_mix=len(mixes), ff_chunk=1024),
        grid=(batch, t // tm),
        in_specs=([row(d), pl.BlockSpec((1, MOD_ROWS, d), mod_map), pl.BlockSpec((1, d), const)]
                  + [row(a.shape[2]) for a in mixes] + [resident(w) for w in w_outs]
                  + [resident(w1), resident(w2)]),
        out_specs=row(d),
        out_shape=jax.ShapeDtypeStruct((batch, t, d), F32),
        compiler_params=_params("arbitrary", "arbitrary"),
        name="out_mlp",
    )(x, mod, g, *mixes, *w_outs, w1, w2)


def kernel(x, c, ctx, c_ctx, ada_w, ada_b, norm1_g, norm2_g, ab_w_in, ab_gate_b, mlstm_norm_g, swa_q_norm_g,
           swa_k_norm_g, swa_sink, ab_w_out, c_w_in, c_q_norm_g, c_k_norm_g, c_w_out, mlp_w1, mlp_w2):
    depth = ada_w.shape[0]
    batch, _, d = x.shape
    mods = _mods(c, c_ctx, ada_w, ada_b)
    for layer in range(depth):
        last = layer == depth - 1
        mod_l = mods[layer, :batch]
        mod_c = mods[layer, batch:batch + 1]
        g1 = norm1_g[layer].reshape(1, d)
        g2 = norm2_g[layer].reshape(1, d)
        w1 = mlp_w1[layer].astype(BF16)
        w2 = mlp_w2[layer].astype(BF16)
        j = layer // 2
        if layer % 2 == 0:
            w, wt, gb = _ab_weights(ab_w_in[j], ab_gate_b[j])
            qn = jnp.tile(swa_q_norm_g[j], LANES // B_HEAD_DIM).reshape(1, LANES)
            kn = jnp.tile(swa_k_norm_g[j], LANES // B_HEAD_DIM).reshape(1, LANES)
            qa_l, kat_l, va_l, oa_l, qb_l, kb_l, vb_l, gr_l = _proj_ab(x, mod_l, g1, w, wt, gb, qn, kn, True)
            qa_c, kat_c, va_c, oa_c, qb_c, kb_c, vb_c, gr_c = _proj_ab(ctx, mod_c, g1, w, wt, gb, qn, kn, False)
            ha_l, ha_c = _mlstm(qa_l, kat_l, va_l, oa_l, gr_l, qa_c, kat_c, va_c, oa_c, gr_c, mlstm_norm_g[j])
            ob_l = _swa(swa_sink[j], qb_l, kb_l, vb_l, kb_c, vb_c, True)
            w_out = ab_w_out[j].astype(BF16)
            w_outs = [w_out[:A_WIDTH], w_out[A_WIDTH:]]
            x = _out_mlp(x, mod_l, g2, [ha_l, ob_l], w_outs, w1, w2)
            if not last:
                ob_c = _swa(swa_sink[j], qb_c, kb_c, vb_c, kb_c, vb_c, False)
                ctx = _out_mlp(ctx, mod_c, g2, [ha_c, ob_c], w_outs, w1, w2)
        else:
            w = c_w_in[j].astype(BF16)
            qn = c_q_norm_g[j].reshape(1, LANES)
            kn = c_k_norm_g[j].reshape(1, LANES)
            q_l, k_l, v_l = _proj_c(x, mod_l, g1, w, qn, kn, True)
            q_c, k_c, v_c = _proj_c(ctx, mod_c, g1, w, qn, kn, False)
            k_all = jnp.concatenate([k_c, k_l], axis=1)
            v_all = jnp.concatenate([v_c, v_l], axis=1)
            w_out = c_w_out[j].astype(BF16)
            o_l = _flash(q_l, k_all, v_all, k_c.shape[1])
            x = _out_mlp(x, mod_l, g2, [o_l], [w_out], w1, w2)
            if not last:
                o_c = _flash(q_c, k_c, v_c, 0)
                ctx = _out_mlp(ctx, mod_c, g2, [o_c], [w_out], w1, w2)
    return x
```

```python
import functools

import numpy as np
import jax
import jax.numpy as jnp
from jax import lax
from jax.experimental import pallas as pl
from jax.experimental.pallas import tpu as pltpu

F32 = jnp.float32
BF16 = jnp.bfloat16

GRID_W = 64
BLOCK = 128
WINDOW = 128
ROPE_BASE = 10000.0
EPS = 1e-6
LOG2_E = 1.4426950408889634
PROJ_ROWS = 512
MLP_ROWS = 1024
SWA_BLOCKS = 8
FLASH_KEYS = 512
FLASH_UNROLL = 4
FLASH_ROWS = 128
N_MOD = 6
MOD_ROWS = 8
LANES = 128
MXU_WIDTH = 256

A_HEADS = 4
A_HEAD_DIM = 128
A_WIDTH = A_HEADS * A_HEAD_DIM
A_GATES = 4 * A_HEADS
GATE_ROWS = 8
B_HEADS = 8
B_KV_HEADS = 2
B_HEAD_DIM = 64
B_WIDTH = B_HEADS * B_HEAD_DIM
B_KV_WIDTH = B_KV_HEADS * B_HEAD_DIM
C_HEADS = 8
C_KV_HEADS = 2
C_HEAD_DIM = 128
C_WIDTH = C_HEADS * C_HEAD_DIM
C_KV_WIDTH = C_KV_HEADS * C_HEAD_DIM

VMEM_LIMIT = 56 * 1024 * 1024

NT_DIMS = (((1,), (1,)), ((), ()))


def _dot(a, b):
    return jnp.dot(a, b, preferred_element_type=F32)


def _dot_nt(a, b):
    return lax.dot_general(a, b, NT_DIMS, preferred_element_type=F32)


def _params(*sem):
    return pltpu.CompilerParams(dimension_semantics=sem, vmem_limit_bytes=VMEM_LIMIT)


def _mods_kernel(c_ref, w_ref, b_ref, o_ref):
    cf = c_ref[...]
    s = (cf * jax.nn.sigmoid(cf)).astype(BF16)
    o_ref[0] = _dot(s, w_ref[0].astype(BF16)) + b_ref[0]


def _mods(c, c_ctx, ada_w, ada_b):
    depth, d, _ = ada_w.shape
    batch = c.shape[0]
    rows = -(-(batch + 1) // 8) * 8
    cc = jnp.zeros((rows, d), F32).at[:batch].set(c).at[batch].set(c_ctx)
    out = pl.pallas_call(
        _mods_kernel,
        grid=(depth, N_MOD),
        in_specs=[
            pl.BlockSpec((rows, d), lambda l, j: (0, 0)),
            pl.BlockSpec((1, d, d), lambda l, j: (l, 0, j)),
            pl.BlockSpec((1, 1, d), lambda l, j: (l, 0, j)),
        ],
        out_specs=pl.BlockSpec((1, rows, d), lambda l, j: (l, 0, j)),
        out_shape=jax.ShapeDtypeStruct((depth, rows, N_MOD * d), F32),
        compiler_params=_params("arbitrary", "arbitrary"),
        name="ada_mods",
    )(cc, ada_w, ada_b.reshape(depth, 1, N_MOD * d))
    out = out.reshape(depth, rows, N_MOD, d)
    return jnp.pad(out, ((0, 0), (0, 0), (0, MOD_ROWS - N_MOD), (0, 0)))


def _modulated_norm(x, g, shift, scale):
    ms = jnp.mean(x * x, axis=-1, keepdims=True)
    return (x * lax.rsqrt(ms + EPS) * g) * (1.0 + scale) + shift


def _split3(a):
    hi = a.astype(BF16)
    r = a - hi.astype(F32)
    mid = r.astype(BF16)
    lo = (r - mid.astype(F32)).astype(BF16)
    return hi, mid, lo


def _head_norm_rope(xh, gain, cos, sin, group, out_scale):
    sq = xh * xh
    half = group // 2
    if group == LANES:
        ssq = jnp.sum(sq, axis=-1, keepdims=True)
    else:
        assert 2 * group == LANES
        lane = lax.broadcasted_iota(jnp.int32, xh.shape, 1)
        left = lane < group
        ssq = jnp.where(left, jnp.sum(jnp.where(left, sq, 0.0), axis=-1, keepdims=True),
                        jnp.sum(jnp.where(left, 0.0, sq), axis=-1, keepdims=True))
    xn = xh * lax.rsqrt(ssq * (1.0 / group) + EPS) * gain
    if group == LANES:
        rot = pltpu.roll(xn, half, 1)
    else:
        first = (lane & (group - 1)) < half
        rot = jnp.where(first, pltpu.roll(xn, LANES - half, 1), pltpu.roll(xn, half, 1))
    y = xn * cos + rot * sin
    if out_scale != 1.0:
        y = y * out_scale
    return y


def _rope_tables(n_tokens, head_dim, use_rope):
    reps = LANES // head_dim
    if not use_rope:
        return np.ones((n_tokens, LANES), np.float32), np.zeros((n_tokens, LANES), np.float32)
    rows = n_tokens // GRID_W
    row = np.repeat(np.arange(rows, dtype=np.float64), GRID_W)
    col = np.tile(np.arange(GRID_W, dtype=np.float64), rows)
    pairs = head_dim // 4
    inv_freq = ROPE_BASE ** (-np.arange(pairs, dtype=np.float64) / pairs)
    ang = np.concatenate([row[:, None] * inv_freq, col[:, None] * inv_freq], axis=-1)
    cos, sin = np.cos(ang), np.sin(ang)
    cos_full = np.concatenate([cos, cos], axis=-1)
    sin_signed = np.concatenate([-sin, sin], axis=-1)
    return (np.tile(cos_full, (1, reps)).astype(np.float32), np.tile(sin_signed, (1, reps)).astype(np.float32))


def _row_tile(t, target):
    tm = min(t, target)
    assert t % tm == 0
    return tm


def _proj_ab_kernel(x_ref, mod_ref, g_ref, w_ref, wt_ref, gb_ref, cos_ref, sin_ref, qn_ref, kn_ref,
                    qa_ref, kat_ref, va_ref, oa_ref, qb_ref, kb_ref, vb_ref, gr_ref, acc_scr):
    @pl.when(pl.program_id(0) == 0)
    def _():
        acc_scr[...] = jnp.zeros_like(acc_scr)

    cos, sin = cos_ref[...], sin_ref[...]
    for j in range(B_WIDTH // LANES):
        y = _head_norm_rope(acc_scr[:, LANES * j:LANES * (j + 1)], qn_ref[...], cos, sin, B_HEAD_DIM,
                            B_HEAD_DIM ** -0.5 * LOG2_E)
        qb_ref[0, :, LANES * j:LANES * (j + 1)] = y.astype(BF16)
    for j in range(B_KV_HEADS):
        y = _head_norm_rope(acc_scr[:, B_WIDTH + LANES * j:B_WIDTH + LANES * (j + 1)], kn_ref[...], cos, sin,
                            B_HEAD_DIM, 1.0)
        kb_ref[0, :, LANES * j:LANES * (j + 1)] = y.astype(BF16)

    mod = mod_ref[0]
    h = _modulated_norm(x_ref[0], g_ref[...], mod[0:1], mod[1:2]).astype(BF16)

    def mm(lo, n):
        return _dot(h, w_ref[:, lo:lo + n])

    qa_ref[0] = mm(0, A_WIDTH).astype(BF16)
    va_ref[0] = mm(A_WIDTH, A_WIDTH).astype(BF16)
    oa_ref[0] = mm(2 * A_WIDTH, A_WIDTH)
    base = 3 * A_WIDTH
    staged = B_WIDTH + B_KV_HEADS * LANES
    for lo in range(0, staged, MXU_WIDTH):
        acc_scr[:, lo:lo + MXU_WIDTH] = mm(base + lo, MXU_WIDTH)
    base += staged
    vb_ref[0] = mm(base, B_KV_HEADS * LANES).astype(BF16)
    tr = _dot_nt(wt_ref[...], h)
    kat_ref[0] = (tr[0:A_WIDTH] * A_HEAD_DIM ** -0.5).astype(BF16)
    gt = tr[A_WIDTH:] + gb_ref[...]
    typ = lax.broadcasted_iota(jnp.int32, gt.shape, 0) & (GATE_ROWS - 1)
    log_sig = jnp.minimum(gt, 0.0) - jnp.log1p(jnp.exp(-jnp.abs(gt)))
    gr_ref[0] = jnp.where((typ == 1) | (typ == 3), log_sig, gt)


def _proj_ab(x, mod, g, w, wg, gb, qn, kn, use_rope):
    batch, t, d = x.shape
    tm = _row_tile(t, PROJ_ROWS)
    nt = t // tm
    last = batch * nt - 1
    cos, sin = _rope_tables(t, B_HEAD_DIM, use_rope)
    per_batch = mod.shape[0] == batch
    cur = lambda j: jnp.minimum(j, last)
    prev = lambda j: jnp.maximum(j - 1, 0)
    const = lambda j: (0, 0)
    row_at = lambda width, tile: pl.BlockSpec((1, tm, width), lambda j: (tile(j) // nt, tile(j) % nt, 0))
    row = lambda width: row_at(width, cur)
    table = pl.BlockSpec((tm, LANES), lambda j: (prev(j) % nt, 0))
    kvw = B_KV_HEADS * LANES
    out_shape = [
        jax.ShapeDtypeStruct((batch, t, A_WIDTH), BF16),
        jax.ShapeDtypeStruct((batch, A_WIDTH, t), BF16),
        jax.ShapeDtypeStruct((batch, t, A_WIDTH), BF16),
        jax.ShapeDtypeStruct((batch, t, A_WIDTH), F32),
        jax.ShapeDtypeStruct((batch, t, B_WIDTH), BF16),
        jax.ShapeDtypeStruct((batch, t, kvw), BF16),
        jax.ShapeDtypeStruct((batch, t, kvw), BF16),
        jax.ShapeDtypeStruct((batch, A_HEADS * GATE_ROWS, t), F32),
    ]
    col = lambda height: pl.BlockSpec((1, height, tm), lambda j: (cur(j) // nt, 0, cur(j) % nt))
    out_specs = [row(A_WIDTH), col(A_WIDTH), row(A_WIDTH), row(A_WIDTH), row_at(B_WIDTH, prev), row_at(kvw, prev),
                 row(kvw), col(A_HEADS * GATE_ROWS)]
    return pl.pallas_call(
        _proj_ab_kernel,
        grid=(batch * nt + 1,),
        in_specs=[
            row(d),
            pl.BlockSpec((1, MOD_ROWS, d), lambda j: ((cur(j) // nt) if per_batch else 0, 0, 0)),
            pl.BlockSpec((1, d), const),
            pl.BlockSpec(w.shape, const),
            pl.BlockSpec(wg.shape, const),
            pl.BlockSpec(gb.shape, const),
            table, table,
            pl.BlockSpec((1, LANES), const),
            pl.BlockSpec((1, LANES), const),
        ],
        out_specs=out_specs,
        out_shape=out_shape,
        scratch_shapes=[pltpu.VMEM((tm, B_WIDTH + kvw), F32)],
        compiler_params=_params("arbitrary"),
        name="proj_ab",
    )(x, mod, g, w, wg, gb, cos, sin, qn, kn)


def _ab_weights(w_in, gate_b):
    bounds = [A_WIDTH * 4, A_WIDTH * 4 + A_GATES, A_WIDTH * 4 + A_GATES + B_WIDTH,
              A_WIDTH * 4 + A_GATES + B_WIDTH + B_KV_WIDTH]
    wa, wgate, wqb, wkb, wvb = jnp.split(w_in, bounds, axis=1)
    wqa, wka, wva_oa = wa[:, :A_WIDTH], wa[:, A_WIDTH:2 * A_WIDTH], wa[:, 2 * A_WIDTH:]

    def dup(wk):
        parts = []
        for hh in range(B_KV_HEADS):
            blk = wk[:, hh * B_HEAD_DIM:(hh + 1) * B_HEAD_DIM]
            parts += [blk] * (LANES // B_HEAD_DIM)
        return jnp.concatenate(parts, axis=1)

    w = jnp.concatenate([wqa, wva_oa, wqb, dup(wkb), dup(wvb)], axis=1).astype(BF16)
    d = w_in.shape[0]
    wg = wgate.reshape(d, 4, A_HEADS).transpose(2, 1, 0)
    wg = jnp.pad(wg, ((0, 0), (0, GATE_ROWS - 4), (0, 0))).reshape(A_HEADS * GATE_ROWS, d)
    wt = jnp.concatenate([wka.T, wg], axis=0).astype(BF16)
    gb = gate_b.reshape(4, A_HEADS).T
    gb = jnp.pad(gb, ((0, 0), (0, GATE_ROWS - 4))).reshape(A_HEADS * GATE_ROWS, 1).astype(F32)
    return w, wt, gb


def _proj_c_kernel(x_ref, mod_ref, g_ref, w_ref, cos_ref, sin_ref, qn_ref, kn_ref, q_ref, k_ref, v_ref, acc_scr):
    @pl.when(pl.program_id(0) == 0)
    def _():
        acc_scr[...] = jnp.zeros_like(acc_scr)

    cos, sin = cos_ref[...], sin_ref[...]
    q_scale = C_HEAD_DIM ** -0.5 * LOG2_E
    for j in range(C_HEADS):
        y = _head_norm_rope(acc_scr[:, LANES * j:LANES * (j + 1)], qn_ref[...], cos, sin, C_HEAD_DIM, q_scale)
        q_ref[0, :, LANES * j:LANES * (j + 1)] = y.astype(BF16)
    for j in range(C_KV_HEADS):
        y = _head_norm_rope(acc_scr[:, C_WIDTH + LANES * j:C_WIDTH + LANES * (j + 1)], kn_ref[...], cos, sin,
                            C_HEAD_DIM, 1.0)
        k_ref[0, :, LANES * j:LANES * (j + 1)] = y.astype(BF16)

    mod = mod_ref[0]
    h = _modulated_norm(x_ref[0], g_ref[...], mod[0:1], mod[1:2]).astype(BF16)
    qk_width = C_WIDTH + C_KV_WIDTH
    for lo in range(0, qk_width, MXU_WIDTH):
        acc_scr[:, lo:lo + MXU_WIDTH] = _dot(h, w_ref[:, lo:lo + MXU_WIDTH])
    v_ref[0] = _dot(h, w_ref[:, qk_width:qk_width + C_KV_WIDTH]).astype(BF16)


def _proj_c(x, mod, g, w, qn, kn, use_rope):
    batch, t, d = x.shape
    tm = _row_tile(t, PROJ_ROWS)
    nt = t // tm
    last = batch * nt - 1
    cos, sin = _rope_tables(t, C_HEAD_DIM, use_rope)
    per_batch = mod.shape[0] == batch
    cur = lambda j: jnp.minimum(j, last)
    prev = lambda j: jnp.maximum(j - 1, 0)
    const = lambda j: (0, 0)
    row_at = lambda width, tile: pl.BlockSpec((1, tm, width), lambda j: (tile(j) // nt, tile(j) % nt, 0))
    table = pl.BlockSpec((tm, LANES), lambda j: (prev(j) % nt, 0))
    return pl.pallas_call(
        _proj_c_kernel,
        grid=(batch * nt + 1,),
        in_specs=[
            row_at(d, cur),
            pl.BlockSpec((1, MOD_ROWS, d), lambda j: ((cur(j) // nt) if per_batch else 0, 0, 0)),
            pl.BlockSpec((1, d), const),
            pl.BlockSpec(w.shape, const),
            table, table,
            pl.BlockSpec((1, LANES), const),
            pl.BlockSpec((1, LANES), const),
        ],
        out_specs=[row_at(C_WIDTH, prev), row_at(C_KV_WIDTH, prev), row_at(C_KV_WIDTH, cur)],
        out_shape=[jax.ShapeDtypeStruct((batch, t, C_WIDTH), BF16),
                   jax.ShapeDtypeStruct((batch, t, C_KV_WIDTH), BF16),
                   jax.ShapeDtypeStruct((batch, t, C_KV_WIDTH), BF16)],
        scratch_shapes=[pltpu.VMEM((tm, C_WIDTH + C_KV_WIDTH), F32)],
        compiler_params=_params("arbitrary"),
        name="proj_c",
    )(x, mod, g, w, cos, sin, qn, kn)


def _mlstm_kernel(ql_ref, ktl_ref, vl_ref, oal_ref, gl_ref, qc_ref, ktc_ref, vc_ref, oac_ref, gc_ref, ng_ref,
                  hl_ref, hc_ref,
                  qs, kts, vs, gs, rowq, rowg, cols, stats, kvb, cst, mst, *, t_ctx, t_lat):
    L = BLOCK
    ncc = t_ctx // L
    ncl = t_lat // L
    nc = ncc + ncl

    qs[0:t_lat] = ql_ref[0]
    qs[t_lat:] = qc_ref[0]
    kts[:, 0:t_lat] = ktl_ref[0]
    kts[:, t_lat:] = ktc_ref[0]
    vs[0:t_lat] = vl_ref[0]
    vs[t_lat:] = vc_ref[0]
    for r in range(4):
        gs[r, 0:ncl] = gl_ref[0, r]
        gs[r, ncl:nc] = gc_ref[0, r]

    ri = lax.broadcasted_iota(jnp.int32, (L, L), 0)
    ci = lax.broadcasted_iota(jnp.int32, (L, L), 1)
    lower = ci <= ri
    upper = ci >= ri
    ones_blk = jnp.ones((L, L), BF16)

    def chunk(i):
        return pl.ds(pl.multiple_of(i * L, L), L)

    def cum(x, mat):
        hi, mid, lo = _split3(x)
        return _dot(hi, mat) + _dot(mid, mat) + _dot(lo, mat)

    li_f, lf_f, li_b, lf_b = (gs[r, 0:nc] for r in range(4))
    bcum_f = cum(lf_f, jnp.where(upper, 1.0, 0.0).astype(BF16))
    bsuf_b = cum(lf_b, jnp.where(lower, 1.0, 0.0).astype(BF16))
    bl_f = bcum_f[:, L - 1:L]
    bl_b = bsuf_b[:, 0:1]
    wl_f = bl_f - bcum_f + li_f
    wl_b = bl_b - bsuf_b + li_b
    mx_f = jnp.max(wl_f, axis=1, keepdims=True)
    mx_b = jnp.max(wl_b, axis=1, keepdims=True)
    rb_f = li_f - bcum_f
    rb_b = li_b - bsuf_b
    lane = lax.broadcasted_iota(jnp.int32, (nc, L), 1)
    pm_f, pm_b = rb_f, rb_b
    step = 1
    while step < L:
        pm_f = jnp.where(lane >= step, jnp.maximum(pm_f, pltpu.roll(pm_f, step, 1)), pm_f)
        pm_b = jnp.where(lane < L - step, jnp.maximum(pm_b, pltpu.roll(pm_b, L - step, 1)), pm_b)
        step *= 2
    for n, val in enumerate((bcum_f, bsuf_b, rb_f, rb_b, jnp.exp(wl_f - mx_f), jnp.exp(wl_b - mx_b), pm_f, pm_b)):
        rowq[n, 0:nc] = val
    for n, val in enumerate((bl_f, bl_b, mx_f, mx_b)):
        stats[n, 0:nc] = jnp.broadcast_to(val, (nc, L))

    def prep(i, _):
        kt = kts[:, chunk(i)].astype(F32)
        vaug = jnp.concatenate([vs[chunk(i), :], ones_blk], axis=1)
        for d in range(2):
            kvb[d, i] = _dot((kt * rowq[4 + d, pl.ds(i, 1), :]).astype(BF16), vaug)
        return 0

    lax.fori_loop(0, nc, prep, 0, unroll=True)

    def visit(d, j):
        if d == 0:
            return jnp.where(j < ncc, ncl + j, j - ncc)
        return nc - 1 - j

    for d in range(2):
        cst[d, visit(d, 0)] = jnp.zeros((L, 2 * L), F32)

    def scan_step(j, carry):
        new = []
        for d in range(2):
            m = carry[d]
            i = visit(d, j)
            i_next = jnp.where(j + 1 < nc, visit(d, j + 1), nc)
            mst[d, pl.ds(i, 1), :] = jnp.broadcast_to(m, (1, L))
            bl = stats[d, pl.ds(i, 1), 0:1]
            mx = stats[2 + d, pl.ds(i, 1), 0:1]
            m_new = jnp.maximum(bl + m, mx)
            cst[d, i_next] = jnp.exp(bl + m - m_new) * cst[d, i] + jnp.exp(mx - m_new) * kvb[d, i]
            new.append(m_new)
        return tuple(new)

    m0 = jnp.zeros((1, 1), F32)
    lax.fori_loop(0, nc, scan_step, (m0, m0), unroll=True)

    for d in range(2):
        g = jnp.maximum(mst[d, 0:nc], rowq[6 + d, 0:nc])
        rowg[d, 0:nc] = g * LOG2_E
        rowg[2 + d, 0:nc] = jnp.exp(-rowq[d, 0:nc] - g)
        rowq[2 + d, 0:nc] = rowq[2 + d, 0:nc] * LOG2_E

    def to_cols(i, _):
        r8 = jnp.concatenate([rowg[n, pl.ds(i, 1), :] for n in range(4)] + [jnp.zeros((4, L), F32)], axis=0)
        cols[chunk(i), :] = jnp.concatenate([r8, jnp.zeros((L - 8, L), F32)], axis=0).T
        return 0

    lax.fori_loop(0, nc, to_cols, 0, unroll=True)

    ng = ng_ref[...]

    def emit(i, oa, out_ref, out_rows):
        q = qs[chunk(i), :]
        vaug = jnp.concatenate([vs[chunk(i), :], ones_blk], axis=1)
        qk = _dot(q, kts[:, chunk(i)])
        colblk = cols[chunk(i), :]
        h = None
        for d in range(2):
            g = jnp.broadcast_to(colblk[:, d:d + 1], (L, L))
            floor = colblk[:, 2 + d:3 + d]
            rb = rowq[2 + d, pl.ds(i, 1), :]
            mask = lower if d == 0 else upper
            dm = jnp.where(mask, jnp.exp2(rb - g), 0.0)
            s = (qk * dm).astype(BF16)
            iw = jnp.exp2(mst[d, pl.ds(i, 1), 0:1] * LOG2_E - g)
            nd = jnp.concatenate([iw, iw], axis=1) * _dot(q, cst[d, i].astype(BF16)) + _dot(s, vaug)
            hd = nd[:, 0:L] / jnp.maximum(jnp.abs(nd[:, L:2 * L]), floor)
            h = hd if h is None else h + hd
        ms = jnp.mean(h * h, axis=-1, keepdims=True)
        y = (h * lax.rsqrt(ms + EPS) * ng) * jax.nn.sigmoid(oa)
        out_ref[0, out_rows, :] = y.astype(BF16)

    def emit_ctx(i, _):
        emit(i + ncl, oac_ref[0, chunk(i), :], hc_ref, chunk(i))
        return 0

    def emit_lat(i, _):
        emit(i, oal_ref[0, chunk(i), :], hl_ref, chunk(i))
        return 0

    lax.fori_loop(0, ncc, emit_ctx, 0, unroll=2)
    lax.fori_loop(0, ncl, emit_lat, 0, unroll=4)


def _mlstm(qa_l, kat_l, va_l, oa_l, gr_l, qa_c, kat_c, va_c, oa_c, gr_c, norm_g):
    batch, t_lat, _ = qa_l.shape
    t_ctx = qa_c.shape[1]
    t_all = t_ctx + t_lat
    nc = t_all // BLOCK
    nc_pad = -(-nc // 8) * 8
    head = lambda t: pl.BlockSpec((1, t, A_HEAD_DIM), lambda b, h: (b, 0, h))
    head_t = lambda t: pl.BlockSpec((1, A_HEAD_DIM, t), lambda b, h: (b, h, 0))
    gates = lambda t: pl.BlockSpec((1, GATE_ROWS, t // BLOCK, BLOCK), lambda b, h: (b, h, 0, 0))
    chunked = lambda g: g.reshape(batch, A_HEADS * GATE_ROWS, g.shape[2] // BLOCK, BLOCK)
    return pl.pallas_call(
        functools.partial(_mlstm_kernel, t_ctx=t_ctx, t_lat=t_lat),
        grid=(batch, A_HEADS),
        in_specs=[head(t_lat), head_t(t_lat), head(t_lat), head(t_lat), gates(t_lat),
                  head(t_ctx), head_t(t_ctx), head(t_ctx), head(t_ctx), gates(t_ctx),
                  pl.BlockSpec((1, A_HEAD_DIM), lambda b, h: (0, h))],
        out_specs=[head(t_lat), head(t_ctx)],
        out_shape=[jax.ShapeDtypeStruct((batch, t_lat, A_WIDTH), BF16),
                   jax.ShapeDtypeStruct((batch, t_ctx, A_WIDTH), BF16)],
        scratch_shapes=[
            pltpu.VMEM((t_all, A_HEAD_DIM), BF16),
            pltpu.VMEM((A_HEAD_DIM, t_all), BF16),
            pltpu.VMEM((t_all, A_HEAD_DIM), BF16),
            pltpu.VMEM((4, nc_pad, BLOCK), F32),
            pltpu.VMEM((8, nc_pad, BLOCK), F32),
            pltpu.VMEM((4, nc_pad, BLOCK), F32),
            pltpu.VMEM((t_all, LANES), F32),
            pltpu.VMEM((4, nc_pad, LANES), F32),
            pltpu.VMEM((2, nc, A_HEAD_DIM, 2 * A_HEAD_DIM), F32),
            pltpu.VMEM((2, nc + 1, A_HEAD_DIM, 2 * A_HEAD_DIM), F32),
            pltpu.VMEM((2, nc_pad, LANES), F32),
        ],
        compiler_params=_params("arbitrary", "arbitrary"),
        name="mlstm",
    )(qa_l, kat_l, va_l, oa_l, chunked(gr_l), qa_c, kat_c, va_c, oa_c, chunked(gr_c),
      norm_g.reshape(1, A_WIDTH))


def _swa_kernel(sink_ref, q_ref, *refs, t_lat, has_window, blocks):
    if has_window:
        bias_ref, k_ref, v_ref, kx_ref, vx_ref, o_ref, s_scr = refs
    else:
        kx_ref, vx_ref, o_ref, s_scr = refs
    L = BLOCK
    nb = t_lat // L
    group = B_HEADS // B_KV_HEADS
    lane = lax.broadcasted_iota(jnp.int32, (L, LANES), 1)
    lo = lane < B_HEAD_DIM
    zero = jnp.zeros((L, LANES), BF16)
    def window(u):
        qblk = pl.program_id(1) * blocks + u
        start = pl.multiple_of(jnp.clip((qblk - 1) * L, 0, t_lat - 3 * L), L)
        return qblk, pl.ds(start, 3 * L)

    def logits(c, u, kvh):
        sl = slice(kvh * LANES, (kvh + 1) * LANES)
        q = q_ref[0, u * L:(u + 1) * L, :]
        qa = q[:, (2 * kvh) * LANES:(2 * kvh + 1) * LANES]
        qb = q[:, (2 * kvh + 1) * LANES:(2 * kvh + 2) * LANES]
        q4 = jnp.concatenate([jnp.where(lo, qa, zero), jnp.where(lo, zero, qa),
                              jnp.where(lo, qb, zero), jnp.where(lo, zero, qb)], axis=0)
        sink = jnp.concatenate([jnp.full((L, 1), sink_ref[kvh * group + g] * LOG2_E, F32)
                                for g in range(group)], axis=0)
        if has_window:
            qblk, win = window(u)
            bias = bias_ref[jnp.where(qblk == 0, 0, jnp.where(qblk == nb - 1, 2, 1))]
            keys = jnp.concatenate([k_ref[0, win, sl], kx_ref[0, :, sl]], axis=0)
            s = _dot_nt(q4, keys)
            s = (s.reshape(group, L, s.shape[1]) + bias[None]).reshape(s.shape)
        else:
            s = _dot_nt(q4, kx_ref[0, :, sl])
        s_scr[c] = s
        return jnp.maximum(jnp.max(s, axis=1, keepdims=True), sink), sink

    def attend(c, u, kvh, m, sink):
        sl = slice(kvh * LANES, (kvh + 1) * LANES)
        if has_window:
            vals = jnp.concatenate([v_ref[0, window(u)[1], sl], vx_ref[0, :, sl]], axis=0)
        else:
            vals = vx_ref[0, :, sl]
        e = jnp.exp2(s_scr[c] - m)
        denom = jnp.sum(e, axis=1, keepdims=True) + jnp.exp2(sink - m)
        o4 = _dot(e.astype(BF16), vals) / denom
        rows = slice(u * L, (u + 1) * L)
        o_ref[0, rows, (2 * kvh) * LANES:(2 * kvh + 1) * LANES] = (
            jnp.where(lo, o4[0:L], o4[L:2 * L]).astype(BF16))
        o_ref[0, rows, (2 * kvh + 1) * LANES:(2 * kvh + 2) * LANES] = (
            jnp.where(lo, o4[2 * L:3 * L], o4[3 * L:4 * L]).astype(BF16))

    chains = [(u, kvh) for u in range(blocks) for kvh in range(B_KV_HEADS)]
    stats = [logits(0, *chains[0])]
    for c, (u, kvh) in enumerate(chains):
        if c + 1 < len(chains):
            stats.append(logits(c + 1, *chains[c + 1]))
        attend(c, u, kvh, *stats[c])


def _band_bias(t_ctx):
    L = BLOCK
    t = np.arange(L)[:, None]
    j = np.arange(3 * L)[None, :]
    tables = []
    for rel in (0, -L, -2 * L):
        ok = np.abs(j - t + rel) <= WINDOW
        tables.append(np.concatenate([np.where(ok, 0.0, -np.inf), np.zeros((L, t_ctx))], axis=1))
    return np.stack(tables).astype(np.float32)


def _swa(sink, q, k, v, k_ctx, v_ctx, has_window):
    batch, t, _ = q.shape
    t_ctx = k_ctx.shape[1]
    nb = t // BLOCK
    blocks = min(SWA_BLOCKS, nb)
    assert nb % blocks == 0 and (nb >= 3 or not has_window)
    kvw = B_KV_HEADS * LANES
    cur = lambda b, i: (b, i, 0)
    whole = lambda b, i: (b, 0, 0)
    ctx_spec = pl.BlockSpec((1, t_ctx, kvw), whole)
    in_specs = [pl.BlockSpec(memory_space=pltpu.SMEM), pl.BlockSpec((1, blocks * BLOCK, B_WIDTH), cur)]
    args = [sink, q]
    if has_window:
        bias = _band_bias(t_ctx)
        in_specs += [pl.BlockSpec(bias.shape, lambda b, i: (0, 0, 0)),
                     pl.BlockSpec((1, t, kvw), whole), pl.BlockSpec((1, t, kvw), whole)]
        args += [bias, k, v]
    in_specs += [ctx_spec, ctx_spec]
    args += [k_ctx, v_ctx]
    return pl.pallas_call(
        functools.partial(_swa_kernel, t_lat=t, has_window=has_window, blocks=blocks),
        grid=(batch, nb // blocks),
        in_specs=in_specs,
        out_specs=pl.BlockSpec((1, blocks * BLOCK, B_WIDTH), cur),
        scratch_shapes=[pltpu.VMEM((blocks * B_KV_HEADS, (B_HEADS // B_KV_HEADS) * BLOCK,
                                    (3 * BLOCK if has_window else 0) + t_ctx), F32)],
        out_shape=jax.ShapeDtypeStruct((batch, t, B_WIDTH), BF16),
        compiler_params=_params("arbitrary", "arbitrary"),
        name="swa" if has_window else "swa_ctx",
    )(*args)


def _flash_kernel(q_ref, k_ref, v_ref, o_ref, s0_scr, s1_scr, m0_scr, m1_scr, vaug_scr, *, chunks, tq, unroll):
    group = C_HEADS // C_KV_HEADS
    nq = q_ref.shape[1] // tq
    vaug_scr[:, 0:LANES] = v_ref[0]
    vaug_scr[:, LANES:] = jnp.ones((v_ref.shape[1], LANES), BF16)

    def q_rows(t):
        return pl.ds(pl.multiple_of(t * tq, tq), tq)

    def load_q(t):
        q = q_ref[0, q_rows(t), :]
        return jnp.concatenate([q[:, g * LANES:(g + 1) * LANES] for g in range(group)], axis=0)

    def stage_a(q4, s_ref, m, start, size):
        s = _dot_nt(q4, k_ref[0, start:start + size, :])
        s_ref[:, start:start + size] = s
        cm = jnp.max(s, axis=1, keepdims=True)
        return cm if m is None else jnp.maximum(m, cm)

    def stage_b(s_ref, m, acc, start, size):
        p = jnp.exp2(s_ref[:, start:start + size] - m)
        pv = _dot(p.astype(BF16), vaug_scr[start:start + size, :])
        return pv if acc is None else acc + pv

    def step(t, cur, nxt):
        q4 = load_q(jnp.minimum(t + 1, nq - 1))
        m_cur = cur[1][...]
        m_next, acc = None, None
        for start, size in chunks:
            m_next = stage_a(q4, nxt[0], m_next, start, size)
            acc = stage_b(cur[0], m_cur, acc, start, size)
        nxt[1][...] = m_next
        out = acc[:, 0:LANES] / acc[:, LANES:2 * LANES]
        for g in range(group):
            o_ref[0, q_rows(t), g * LANES:(g + 1) * LANES] = out[g * tq:(g + 1) * tq].astype(BF16)

    bufs = ((s0_scr, m0_scr), (s1_scr, m1_scr))
    q4 = load_q(0)
    m = None
    for start, size in chunks:
        m = stage_a(q4, s0_scr, m, start, size)
    m0_scr[...] = m

    def body(u, _):
        for k in range(unroll):
            step(unroll * u + k, bufs[k % 2], bufs[(k + 1) % 2])
        return 0

    lax.fori_loop(0, nq // unroll, body, 0)


def _key_chunks(total, target):
    chunks = []
    start = 0
    while start < total:
        size = min(target, total - start)
        chunks.append((start, size))
        start += size
    return tuple(chunks)


def _flash(q, k, v, t_ctx):
    batch, t, _ = q.shape
    tk_all = k.shape[1]
    tq = _row_tile(t, FLASH_ROWS)
    unroll = FLASH_UNROLL if (t // tq) % FLASH_UNROLL == 0 else 2
    assert unroll % 2 == 0 and (t // tq) % unroll == 0
    group = C_HEADS // C_KV_HEADS
    gw = group * C_HEAD_DIM
    chunks = (_key_chunks(t_ctx, FLASH_KEYS)
              + tuple((t_ctx + a, n) for a, n in _key_chunks(tk_all - t_ctx, FLASH_KEYS)))
    return pl.pallas_call(
        functools.partial(_flash_kernel, chunks=chunks, tq=tq, unroll=unroll),
        grid=(batch, C_KV_HEADS),
        in_specs=[pl.BlockSpec((1, t, gw), lambda b, h: (b, 0, h)),
                  pl.BlockSpec((1, tk_all, C_HEAD_DIM), lambda b, h: (b, 0, h)),
                  pl.BlockSpec((1, tk_all, C_HEAD_DIM), lambda b, h: (b, 0, h))],
        out_specs=pl.BlockSpec((1, t, gw), lambda b, h: (b, 0, h)),
        out_shape=jax.ShapeDtypeStruct((batch, t, C_WIDTH), BF16),
        scratch_shapes=[pltpu.VMEM((group * tq, tk_all), F32), pltpu.VMEM((group * tq, tk_all), F32),
                        pltpu.VMEM((group * tq, 1), F32), pltpu.VMEM((group * tq, 1), F32),
                        pltpu.VMEM((tk_all, 2 * C_HEAD_DIM), BF16)],
        compiler_params=_params("arbitrary", "arbitrary"),
        name="flash_c",
    )(q, k, v)


def _out_mlp_kernel(*refs, n_mix, ff_chunk):
    x_ref, mod_ref, g_ref = refs[0:3]
    mix_refs = refs[3:3 + n_mix]
    wo_refs = refs[3 + n_mix:3 + 2 * n_mix]
    w1_ref, w2_ref, o_ref = refs[3 + 2 * n_mix:]
    mod = mod_ref[0]
    y = None
    for a_ref, w_ref in zip(mix_refs, wo_refs):
        part = _dot(a_ref[0], w_ref[...])
        y = part if y is None else y + part
    x1 = x_ref[0] + mod[2:3] * y
    h = _modulated_norm(x1, g_ref[...], mod[3:4], mod[4:5]).astype(BF16)
    d_ff = w1_ref.shape[1]
    acc = None
    for c in range(d_ff // ff_chunk):
        a = _dot(h, w1_ref[:, c * ff_chunk:(c + 1) * ff_chunk])
        a = jnp.square(jnp.maximum(a, 0.0)).astype(BF16)
        part = _dot(a, w2_ref[c * ff_chunk:(c + 1) * ff_chunk, :])
        acc = part if acc is None else acc + part
    o_ref[0] = x1 + mod[5:6] * acc


def _out_mlp(x, mod, g, mixes, w_outs, w1, w2):
    batch, t, d = x.shape
    tm = _row_tile(t, MLP_ROWS)
    mod_map = (lambda b, i: (b, 0, 0)) if mod.shape[0] == batch else (lambda b, i: (0, 0, 0))
    const = lambda b, i: (0, 0)
    resident = lambda a: pl.BlockSpec(a.shape, const, pipeline_mode=pl.Buffered(1))
    row = lambda width: pl.BlockSpec((1, tm, width), lambda b, i: (b, i, 0))
    return pl.pallas_call(
        functools.partial(_out_mlp_kernel, n_mix=len(mixes), ff_chunk=1024),
        grid=(batch, t // tm),
        in_specs=([row(d), pl.BlockSpec((1, MOD_ROWS, d), mod_map), pl.BlockSpec((1, d), const)]
                  + [row(a.shape[2]) for a in mixes] + [resident(w) for w in w_outs]
                  + [resident(w1), resident(w2)]),
        out_specs=row(d),
        out_shape=jax.ShapeDtypeStruct((batch, t, d), F32),
        compiler_params=_params("arbitrary", "arbitrary"),
        name="out_mlp",
    )(x, mod, g, *mixes, *w_outs, w1, w2)


def kernel(x, c, ctx, c_ctx, ada_w, ada_b, norm1_g, norm2_g, ab_w_in, ab_gate_b, mlstm_norm_g, swa_q_norm_g,
           swa_k_norm_g, swa_sink, ab_w_out, c_w_in, c_q_norm_g, c_k_norm_g, c_w_out, mlp_w1, mlp_w2):
    depth = ada_w.shape[0]
    batch, _, d = x.shape
    mods = _mods(c, c_ctx, ada_w, ada_b)
    for layer in range(depth):
        last = layer == depth - 1
        mod_l = mods[layer, :batch]
        mod_c = mods[layer, batch:batch + 1]
        g1 = norm1_g[layer].reshape(1, d)
        g2 = norm2_g[layer].reshape(1, d)
        w1 = mlp_w1[layer].astype(BF16)
        w2 = mlp_w2[layer].astype(BF16)
        j = layer // 2
        if layer % 2 == 0:
            w, wt, gb = _ab_weights(ab_w_in[j], ab_gate_b[j])
            qn = jnp.tile(swa_q_norm_g[j], LANES // B_HEAD_DIM).reshape(1, LANES)
            kn = jnp.tile(swa_k_norm_g[j], LANES // B_HEAD_DIM).reshape(1, LANES)
            qa_l, kat_l, va_l, oa_l, qb_l, kb_l, vb_l, gr_l = _proj_ab(x, mod_l, g1, w, wt, gb, qn, kn, True)
            qa_c, kat_c, va_c, oa_c, qb_c, kb_c, vb_c, gr_c = _proj_ab(ctx, mod_c, g1, w, wt, gb, qn, kn, False)
            ha_l, ha_c = _mlstm(qa_l, kat_l, va_l, oa_l, gr_l, qa_c, kat_c, va_c, oa_c, gr_c, mlstm_norm_g[j])
            ob_l = _swa(swa_sink[j], qb_l, kb_l, vb_l, kb_c, vb_c, True)
            w_out = ab_w_out[j].astype(BF16)
            w_outs = [w_out[:A_WIDTH], w_out[A_WIDTH:]]
            x = _out_mlp(x, mod_l, g2, [ha_l, ob_l], w_outs, w1, w2)
            if not last:
                ob_c = _swa(swa_sink[j], qb_c, kb_c, vb_c, kb_c, vb_c, False)
                ctx = _out_mlp(ctx, mod_c, g2, [ha_c, ob_c], w_outs, w1, w2)
        else:
            w = c_w_in[j].astype(BF16)
            qn = c_q_norm_g[j].reshape(1, LANES)
            kn = c_k_norm_g[j].reshape(1, LANES)
            q_l, k_l, v_l = _proj_c(x, mod_l, g1, w, qn, kn, True)
            q_c, k_c, v_c = _proj_c(ctx, mod_c, g1, w, qn, kn, False)
            k_all = jnp.concatenate([k_c, k_l], axis=1)
            v_all = jnp.concatenate([v_c, v_l], axis=1)
            w_out = c_w_out[j].astype(BF16)
            o_l = _flash(q_l, k_all, v_all, k_c.shape[1])
            x = _out_mlp(x, mod_l, g2, [o_l], [w_out], w1, w2)
            if not last:
                o_c = _flash(q_c, k_c, v_c, 0)
                ctx = _out_mlp(ctx, mod_c, g2, [o_c], [w_out], w1, w2)
    return x
```

```python
import functools

import numpy as np
import jax
import jax.numpy as jnp
from jax import lax
from jax.experimental import pallas as pl
from jax.experimental.pallas import tpu as pltpu

F32 = jnp.float32
BF16 = jnp.bfloat16

GRID_W = 64
BLOCK = 128
WINDOW = 128
ROPE_BASE = 10000.0
EPS = 1e-6
LOG2_E = 1.4426950408889634
PROJ_ROWS = 512
MLP_ROWS = 1024
SWA_BLOCKS = 8
FLASH_KEYS = 512
FLASH_UNROLL = 4
FLASH_ROWS = 128
N_MOD = 6
MOD_ROWS = 8
LANES = 128
MXU_WIDTH = 256

A_HEADS = 4
A_HEAD_DIM = 128
A_WIDTH = A_HEADS * A_HEAD_DIM
A_GATES = 4 * A_HEADS
GATE_ROWS = 8
B_HEADS = 8
B_KV_HEADS = 2
B_HEAD_DIM = 64
B_WIDTH = B_HEADS * B_HEAD_DIM
B_KV_WIDTH = B_KV_HEADS * B_HEAD_DIM
C_HEADS = 8
C_KV_HEADS = 2
C_HEAD_DIM = 128
C_WIDTH = C_HEADS * C_HEAD_DIM
C_KV_WIDTH = C_KV_HEADS * C_HEAD_DIM

VMEM_LIMIT = 56 * 1024 * 1024

NT_DIMS = (((1,), (1,)), ((), ()))


def _dot(a, b):
    return jnp.dot(a, b, preferred_element_type=F32)


def _dot_nt(a, b):
    return lax.dot_general(a, b, NT_DIMS, preferred_element_type=F32)


def _params(*sem):
    return pltpu.CompilerParams(dimension_semantics=sem, vmem_limit_bytes=VMEM_LIMIT)


def _mods_kernel(c_ref, w_ref, b_ref, o_ref):
    cf = c_ref[...]
    s = (cf * jax.nn.sigmoid(cf)).astype(BF16)
    o_ref[0] = _dot(s, w_ref[0].astype(BF16)) + b_ref[0]


def _mods(c, c_ctx, ada_w, ada_b):
    depth, d, _ = ada_w.shape
    batch = c.shape[0]
    rows = -(-(batch + 1) // 8) * 8
    cc = jnp.zeros((rows, d), F32).at[:batch].set(c).at[batch].set(c_ctx)
    out = pl.pallas_call(
        _mods_kernel,
        grid=(depth, N_MOD),
        in_specs=[
            pl.BlockSpec((rows, d), lambda l, j: (0, 0)),
            pl.BlockSpec((1, d, d), lambda l, j: (l, 0, j)),
            pl.BlockSpec((1, 1, d), lambda l, j: (l, 0, j)),
        ],
        out_specs=pl.BlockSpec((1, rows, d), lambda l, j: (l, 0, j)),
        out_shape=jax.ShapeDtypeStruct((depth, rows, N_MOD * d), F32),
        compiler_params=_params("arbitrary", "arbitrary"),
        name="ada_mods",
    )(cc, ada_w, ada_b.reshape(depth, 1, N_MOD * d))
    out = out.reshape(depth, rows, N_MOD, d)
    return jnp.pad(out, ((0, 0), (0, 0), (0, MOD_ROWS - N_MOD), (0, 0)))


def _modulated_norm(x, g, shift, scale):
    ms = jnp.mean(x * x, axis=-1, keepdims=True)
    return (x * lax.rsqrt(ms + EPS) * g) * (1.0 + scale) + shift


def _split3(a):
    hi = a.astype(BF16)
    r = a - hi.astype(F32)
    mid = r.astype(BF16)
    lo = (r - mid.astype(F32)).astype(BF16)
    return hi, mid, lo


def _head_norm_rope(xh, gain, cos, sin, group, out_scale):
    sq = xh * xh
    half = group // 2
    if group == LANES:
        ssq = jnp.sum(sq, axis=-1, keepdims=True)
    else:
        assert 2 * group == LANES
        lane = lax.broadcasted_iota(jnp.int32, xh.shape, 1)
        left = lane < group
        ssq = jnp.where(left, jnp.sum(jnp.where(left, sq, 0.0), axis=-1, keepdims=True),
                        jnp.sum(jnp.where(left, 0.0, sq), axis=-1, keepdims=True))
    xn = xh * lax.rsqrt(ssq * (1.0 / group) + EPS) * gain
    if group == LANES:
        rot = pltpu.roll(xn, half, 1)
    else:
        first = (lane & (group - 1)) < half
        rot = jnp.where(first, pltpu.roll(xn, LANES - half, 1), pltpu.roll(xn, half, 1))
    y = xn * cos + rot * sin
    if out_scale != 1.0:
        y = y * out_scale
    return y


def _rope_tables(n_tokens, head_dim, use_rope):
    reps = LANES // head_dim
    if not use_rope:
        return np.ones((n_tokens, LANES), np.float32), np.zeros((n_tokens, LANES), np.float32)
    rows = n_tokens // GRID_W
    row = np.repeat(np.arange(rows, dtype=np.float64), GRID_W)
    col = np.tile(np.arange(GRID_W, dtype=np.float64), rows)
    pairs = head_dim // 4
    inv_freq = ROPE_BASE ** (-np.arange(pairs, dtype=np.float64) / pairs)
    ang = np.concatenate([row[:, None] * inv_freq, col[:, None] * inv_freq], axis=-1)
    cos, sin = np.cos(ang), np.sin(ang)
    cos_full = np.concatenate([cos, cos], axis=-1)
    sin_signed = np.concatenate([-sin, sin], axis=-1)
    return (np.tile(cos_full, (1, reps)).astype(np.float32), np.tile(sin_signed, (1, reps)).astype(np.float32))


def _row_tile(t, target):
    tm = min(t, target)
    assert t % tm == 0
    return tm


def _proj_ab_kernel(x_ref, mod_ref, g_ref, w_ref, wt_ref, gb_ref, cos_ref, sin_ref, qn_ref, kn_ref,
                    qa_ref, kat_ref, va_ref, oa_ref, qb_ref, kb_ref, vb_ref, gr_ref, acc_scr):
    @pl.when(pl.program_id(0) == 0)
    def _():
        acc_scr[...] = jnp.zeros_like(acc_scr)

    cos, sin = cos_ref[...], sin_ref[...]
    for j in range(B_WIDTH // LANES):
        y = _head_norm_rope(acc_scr[:, LANES * j:LANES * (j + 1)], qn_ref[...], cos, sin, B_HEAD_DIM,
                            B_HEAD_DIM ** -0.5 * LOG2_E)
        qb_ref[0, :, LANES * j:LANES * (j + 1)] = y.astype(BF16)
    for j in range(B_KV_HEADS):
        y = _head_norm_rope(acc_scr[:, B_WIDTH + LANES * j:B_WIDTH + LANES * (j + 1)], kn_ref[...], cos, sin,
                            B_HEAD_DIM, 1.0)
        kb_ref[0, :, LANES * j:LANES * (j + 1)] = y.astype(BF16)

    mod = mod_ref[0]
    h = _modulated_norm(x_ref[0], g_ref[...], mod[0:1], mod[1:2]).astype(BF16)

    def mm(lo, n):
        return _dot(h, w_ref[:, lo:lo + n])

    qa_ref[0] = mm(0, A_WIDTH).astype(BF16)
    va_ref[0] = mm(A_WIDTH, A_WIDTH).astype(BF16)
    oa_ref[0] = mm(2 * A_WIDTH, A_WIDTH)
    base = 3 * A_WIDTH
    staged = B_WIDTH + B_KV_HEADS * LANES
    for lo in range(0, staged, MXU_WIDTH):
        acc_scr[:, lo:lo + MXU_WIDTH] = mm(base + lo, MXU_WIDTH)
    base += staged
    vb_ref[0] = mm(base, B_KV_HEADS * LANES).astype(BF16)
    tr = _dot_nt(wt_ref[...], h)
    kat_ref[0] = (tr[0:A_WIDTH] * A_HEAD_DIM ** -0.5).astype(BF16)
    gt = tr[A_WIDTH:] + gb_ref[...]
    typ = lax.broadcasted_iota(jnp.int32, gt.shape, 0) & (GATE_ROWS - 1)
    log_sig = jnp.minimum(gt, 0.0) - jnp.log1p(jnp.exp(-jnp.abs(gt)))
    gr_ref[0] = jnp.where((typ == 1) | (typ == 3), log_sig, gt)


def _proj_ab(x, mod, g, w, wg, gb, qn, kn, use_rope):
    batch, t, d = x.shape
    tm = _row_tile(t, PROJ_ROWS)
    nt = t // tm
    last = batch * nt - 1
    cos, sin = _rope_tables(t, B_HEAD_DIM, use_rope)
    per_batch = mod.shape[0] == batch
    cur = lambda j: jnp.minimum(j, last)
    prev = lambda j: jnp.maximum(j - 1, 0)
    const = lambda j: (0, 0)
    row_at = lambda width, tile: pl.BlockSpec((1, tm, width), lambda j: (tile(j) // nt, tile(j) % nt, 0))
    row = lambda width: row_at(width, cur)
    table = pl.BlockSpec((tm, LANES), lambda j: (prev(j) % nt, 0))
    kvw = B_KV_HEADS * LANES
    out_shape = [
        jax.ShapeDtypeStruct((batch, t, A_WIDTH), BF16),
        jax.ShapeDtypeStruct((batch, A_WIDTH, t), BF16),
        jax.ShapeDtypeStruct((batch, t, A_WIDTH), BF16),
        jax.ShapeDtypeStruct((batch, t, A_WIDTH), F32),
        jax.ShapeDtypeStruct((batch, t, B_WIDTH), BF16),
        jax.ShapeDtypeStruct((batch, t, kvw), BF16),
        jax.ShapeDtypeStruct((batch, t, kvw), BF16),
        jax.ShapeDtypeStruct((batch, A_HEADS * GATE_ROWS, t), F32),
    ]
    col = lambda height: pl.BlockSpec((1, height, tm), lambda j: (cur(j) // nt, 0, cur(j) % nt))
    out_specs = [row(A_WIDTH), col(A_WIDTH), row(A_WIDTH), row(A_WIDTH), row_at(B_WIDTH, prev), row_at(kvw, prev),
                 row(kvw), col(A_HEADS * GATE_ROWS)]
    return pl.pallas_call(
        _proj_ab_kernel,
        grid=(batch * nt + 1,),
        in_specs=[
            row(d),
            pl.BlockSpec((1, MOD_ROWS, d), lambda j: ((cur(j) // nt) if per_batch else 0, 0, 0)),
            pl.BlockSpec((1, d), const),
            pl.BlockSpec(w.shape, const),
            pl.BlockSpec(wg.shape, const),
            pl.BlockSpec(gb.shape, const),
            table, table,
            pl.BlockSpec((1, LANES), const),
            pl.BlockSpec((1, LANES), const),
        ],
        out_specs=out_specs,
        out_shape=out_shape,
        scratch_shapes=[pltpu.VMEM((tm, B_WIDTH + kvw), F32)],
        compiler_params=_params("arbitrary"),
        name="proj_ab",
    )(x, mod, g, w, wg, gb, cos, sin, qn, kn)


def _ab_weights(w_in, gate_b):
    bounds = [A_WIDTH * 4, A_WIDTH * 4 + A_GATES, A_WIDTH * 4 + A_GATES + B_WIDTH,
              A_WIDTH * 4 + A_GATES + B_WIDTH + B_KV_WIDTH]
    wa, wgate, wqb, wkb, wvb = jnp.split(w_in, bounds, axis=1)
    wqa, wka, wva_oa = wa[:, :A_WIDTH], wa[:, A_WIDTH:2 * A_WIDTH], wa[:, 2 * A_WIDTH:]

    def dup(wk):
        parts = []
        for hh in range(B_KV_HEADS):
            blk = wk[:, hh * B_HEAD_DIM:(hh + 1) * B_HEAD_DIM]
            parts += [blk] * (LANES // B_HEAD_DIM)
        return jnp.concatenate(parts, axis=1)

    w = jnp.concatenate([wqa, wva_oa, wqb, dup(wkb), dup(wvb)], axis=1).astype(BF16)
    d = w_in.shape[0]
    wg = wgate.reshape(d, 4, A_HEADS).transpose(2, 1, 0)
    wg = jnp.pad(wg, ((0, 0), (0, GATE_ROWS - 4), (0, 0))).reshape(A_HEADS * GATE_ROWS, d)
    wt = jnp.concatenate([wka.T, wg], axis=0).astype(BF16)
    gb = gate_b.reshape(4, A_HEADS).T
    gb = jnp.pad(gb, ((0, 0), (0, GATE_ROWS - 4))).reshape(A_HEADS * GATE_ROWS, 1).astype(F32)
    return w, wt, gb


def _proj_c_kernel(x_ref, mod_ref, g_ref, w_ref, cos_ref, sin_ref, qn_ref, kn_ref, q_ref, k_ref, v_ref, acc_scr):
    @pl.when(pl.program_id(0) == 0)
    def _():
        acc_scr[...] = jnp.zeros_like(acc_scr)

    cos, sin = cos_ref[...], sin_ref[...]
    q_scale = C_HEAD_DIM ** -0.5 * LOG2_E
    for j in range(C_HEADS):
        y = _head_norm_rope(acc_scr[:, LANES * j:LANES * (j + 1)], qn_ref[...], cos, sin, C_HEAD_DIM, q_scale)
        q_ref[0, :, LANES * j:LANES * (j + 1)] = y.astype(BF16)
    for j in range(C_KV_HEADS):
        y = _head_norm_rope(acc_scr[:, C_WIDTH + LANES * j:C_WIDTH + LANES * (j + 1)], kn_ref[...], cos, sin,
                            C_HEAD_DIM, 1.0)
        k_ref[0, :, LANES * j:LANES * (j + 1)] = y.astype(BF16)

    mod = mod_ref[0]
    h = _modulated_norm(x_ref[0], g_ref[...], mod[0:1], mod[1:2]).astype(BF16)
    qk_width = C_WIDTH + C_KV_WIDTH
    for lo in range(0, qk_width, MXU_WIDTH):
        acc_scr[:, lo:lo + MXU_WIDTH] = _dot(h, w_ref[:, lo:lo + MXU_WIDTH])
    v_ref[0] = _dot(h, w_ref[:, qk_width:qk_width + C_KV_WIDTH]).astype(BF16)


def _proj_c(x, mod, g, w, qn, kn, use_rope):
    batch, t, d = x.shape
    tm = _row_tile(t, PROJ_ROWS)
    nt = t // tm
    last = batch * nt - 1
    cos, sin = _rope_tables(t, C_HEAD_DIM, use_rope)
    per_batch = mod.shape[0] == batch
    cur = lambda j: jnp.minimum(j, last)
    prev = lambda j: jnp.maximum(j - 1, 0)
    const = lambda j: (0, 0)
    row_at = lambda width, tile: pl.BlockSpec((1, tm, width), lambda j: (tile(j) // nt, tile(j) % nt, 0))
    table = pl.BlockSpec((tm, LANES), lambda j: (prev(j) % nt, 0))
    return pl.pallas_call(
        _proj_c_kernel,
        grid=(batch * nt + 1,),
        in_specs=[
            row_at(d, cur),
            pl.BlockSpec((1, MOD_ROWS, d), lambda j: ((cur(j) // nt) if per_batch else 0, 0, 0)),
            pl.BlockSpec((1, d), const),
            pl.BlockSpec(w.shape, const),
            table, table,
            pl.BlockSpec((1, LANES), const),
            pl.BlockSpec((1, LANES), const),
        ],
        out_specs=[row_at(C_WIDTH, prev), row_at(C_KV_WIDTH, prev), row_at(C_KV_WIDTH, cur)],
        out_shape=[jax.ShapeDtypeStruct((batch, t, C_WIDTH), BF16),
                   jax.ShapeDtypeStruct((batch, t, C_KV_WIDTH), BF16),
                   jax.ShapeDtypeStruct((batch, t, C_KV_WIDTH), BF16)],
        scratch_shapes=[pltpu.VMEM((tm, C_WIDTH + C_KV_WIDTH), F32)],
        compiler_params=_params("arbitrary"),
        name="proj_c",
    )(x, mod, g, w, cos, sin, qn, kn)


def _mlstm_kernel(ql_ref, ktl_ref, vl_ref, oal_ref, gl_ref, qc_ref, ktc_ref, vc_ref, oac_ref, gc_ref, ng_ref,
                  hl_ref, hc_ref,
                  qs, kts, vs, gs, rowq, rowg, cols, stats, kvb, cst, mst, *, t_ctx, t_lat):
    L = BLOCK
    ncc = t_ctx // L
    ncl = t_lat // L
    nc = ncc + ncl

    qs[0:t_lat] = ql_ref[0]
    qs[t_lat:] = qc_ref[0]
    kts[:, 0:t_lat] = ktl_ref[0]
    kts[:, t_lat:] = ktc_ref[0]
    vs[0:t_lat] = vl_ref[0]
    vs[t_lat:] = vc_ref[0]
    for r in range(4):
        gs[r, 0:ncl] = gl_ref[0, r]
        gs[r, ncl:nc] = gc_ref[0, r]

    ri = lax.broadcasted_iota(jnp.int32, (L, L), 0)
    ci = lax.broadcasted_iota(jnp.int32, (L, L), 1)
    lower = ci <= ri
    upper = ci >= ri
    ones_blk = jnp.ones((L, L), BF16)

    def chunk(i):
        return pl.ds(pl.multiple_of(i * L, L), L)

    def cum(x, mat):
        hi, mid, lo = _split3(x)
        return _dot(hi, mat) + _dot(mid, mat) + _dot(lo, mat)

    li_f, lf_f, li_b, lf_b = (gs[r, 0:nc] for r in range(4))
    bcum_f = cum(lf_f, jnp.where(upper, 1.0, 0.0).astype(BF16))
    bsuf_b = cum(lf_b, jnp.where(lower, 1.0, 0.0).astype(BF16))
    bl_f = bcum_f[:, L - 1:L]
    bl_b = bsuf_b[:, 0:1]
    wl_f = bl_f - bcum_f + li_f
    wl_b = bl_b - bsuf_b + li_b
    mx_f = jnp.max(wl_f, axis=1, keepdims=True)
    mx_b = jnp.max(wl_b, axis=1, keepdims=True)
    rb_f = li_f - bcum_f
    rb_b = li_b - bsuf_b
    lane = lax.broadcasted_iota(jnp.int32, (nc, L), 1)
    pm_f, pm_b = rb_f, rb_b
    step = 1
    while step < L:
        pm_f = jnp.where(lane >= step, jnp.maximum(pm_f, pltpu.roll(pm_f, step, 1)), pm_f)
        pm_b = jnp.where(lane < L - step, jnp.maximum(pm_b, pltpu.roll(pm_b, L - step, 1)), pm_b)
        step *= 2
    for n, val in enumerate((bcum_f, bsuf_b, rb_f, rb_b, jnp.exp(wl_f - mx_f), jnp.exp(wl_b - mx_b), pm_f, pm_b)):
        rowq[n, 0:nc] = val
    for n, val in enumerate((bl_f, bl_b, mx_f, mx_b)):
        stats[n, 0:nc] = jnp.broadcast_to(val, (nc, L))

    def prep(i, _):
        kt = kts[:, chunk(i)].astype(F32)
        vaug = jnp.concatenate([vs[chunk(i), :], ones_blk], axis=1)
        for d in range(2):
            kvb[d, i] = _dot((kt * rowq[4 + d, pl.ds(i, 1), :]).astype(BF16), vaug)
        return 0

    lax.fori_loop(0, nc, prep, 0, unroll=True)

    def visit(d, j):
        if d == 0:
            return jnp.where(j < ncc, ncl + j, j - ncc)
        return nc - 1 - j

    for d in range(2):
        cst[d, visit(d, 0)] = jnp.zeros((L, 2 * L), F32)

    def scan_step(j, carry):
        new = []
        for d in range(2):
            m = carry[d]
            i = visit(d, j)
            i_next = jnp.where(j + 1 < nc, visit(d, j + 1), nc)
            mst[d, pl.ds(i, 1), :] = jnp.broadcast_to(m, (1, L))
            bl = stats[d, pl.ds(i, 1), 0:1]
            mx = stats[2 + d, pl.ds(i, 1), 0:1]
            m_new = jnp.maximum(bl + m, mx)
            cst[d, i_next] = jnp.exp(bl + m - m_new) * cst[d, i] + jnp.exp(mx - m_new) * kvb[d, i]
            new.append(m_new)
        return tuple(new)

    m0 = jnp.zeros((1, 1), F32)
    lax.fori_loop(0, nc, scan_step, (m0, m0), unroll=True)

    for d in range(2):
        g = jnp.maximum(mst[d, 0:nc], rowq[6 + d, 0:nc])
        rowg[d, 0:nc] = g * LOG2_E
        rowg[2 + d, 0:nc] = jnp.exp(-rowq[d, 0:nc] - g)
        rowq[2 + d, 0:nc] = rowq[2 + d, 0:nc] * LOG2_E

    def to_cols(i, _):
        r8 = jnp.concatenate([rowg[n, pl.ds(i, 1), :] for n in range(4)] + [jnp.zeros((4, L), F32)], axis=0)
        cols[chunk(i), :] = jnp.concatenate([r8, jnp.zeros((L - 8, L), F32)], axis=0).T
        return 0

    lax.fori_loop(0, nc, to_cols, 0, unroll=True)

    ng = ng_ref[...]

    def emit(i, oa, out_ref, out_rows):
        q = qs[chunk(i), :]
        vaug = jnp.concatenate([vs[chunk(i), :], ones_blk], axis=1)
        qk = _dot(q, kts[:, chunk(i)])
        colblk = cols[chunk(i), :]
        h = None
        for d in range(2):
            g = jnp.broadcast_to(colblk[:, d:d + 1], (L, L))
            floor = colblk[:, 2 + d:3 + d]
            rb = rowq[2 + d, pl.ds(i, 1), :]
            mask = lower if d == 0 else upper
            dm = jnp.where(mask, jnp.exp2(rb - g), 0.0)
            s = (qk * dm).astype(BF16)
            iw = jnp.exp2(mst[d, pl.ds(i, 1), 0:1] * LOG2_E - g)
            nd = jnp.concatenate([iw, iw], axis=1) * _dot(q, cst[d, i].astype(BF16)) + _dot(s, vaug)
            hd = nd[:, 0:L] / jnp.maximum(jnp.abs(nd[:, L:2 * L]), floor)
            h = hd if h is None else h + hd
        ms = jnp.mean(h * h, axis=-1, keepdims=True)
        y = (h * lax.rsqrt(ms + EPS) * ng) * jax.nn.sigmoid(oa)
        out_ref[0, out_rows, :] = y.astype(BF16)

    def emit_ctx(i, _):
        emit(i + ncl, oac_ref[0, chunk(i), :], hc_ref, chunk(i))
        return 0

    def emit_lat(i, _):
        emit(i, oal_ref[0, chunk(i), :], hl_ref, chunk(i))
        return 0

    lax.fori_loop(0, ncc, emit_ctx, 0, unroll=2)
    lax.fori_loop(0, ncl, emit_lat, 0, unroll=4)


def _mlstm(qa_l, kat_l, va_l, oa_l, gr_l, qa_c, kat_c, va_c, oa_c, gr_c, norm_g):
    batch, t_lat, _ = qa_l.shape
    t_ctx = qa_c.shape[1]
    t_all = t_ctx + t_lat
    nc = t_all // BLOCK
    nc_pad = -(-nc // 8) * 8
    head = lambda t: pl.BlockSpec((1, t, A_HEAD_DIM), lambda b, h: (b, 0, h))
    head_t = lambda t: pl.BlockSpec((1, A_HEAD_DIM, t), lambda b, h: (b, h, 0))
    gates = lambda t: pl.BlockSpec((1, GATE_ROWS, t // BLOCK, BLOCK), lambda b, h: (b, h, 0, 0))
    chunked = lambda g: g.reshape(batch, A_HEADS * GATE_ROWS, g.shape[2] // BLOCK, BLOCK)
    return pl.pallas_call(
        functools.partial(_mlstm_kernel, t_ctx=t_ctx, t_lat=t_lat),
        grid=(batch, A_HEADS),
        in_specs=[head(t_lat), head_t(t_lat), head(t_lat), head(t_lat), gates(t_lat),
                  head(t_ctx), head_t(t_ctx), head(t_ctx), head(t_ctx), gates(t_ctx),
                  pl.BlockSpec((1, A_HEAD_DIM), lambda b, h: (0, h))],
        out_specs=[head(t_lat), head(t_ctx)],
        out_shape=[jax.ShapeDtypeStruct((batch, t_lat, A_WIDTH), BF16),
                   jax.ShapeDtypeStruct((batch, t_ctx, A_WIDTH), BF16)],
        scratch_shapes=[
            pltpu.VMEM((t_all, A_HEAD_DIM), BF16),
            pltpu.VMEM((A_HEAD_DIM, t_all), BF16),
            pltpu.VMEM((t_all, A_HEAD_DIM), BF16),
            pltpu.VMEM((4, nc_pad, BLOCK), F32),
            pltpu.VMEM((8, nc_pad, BLOCK), F32),
            pltpu.VMEM((4, nc_pad, BLOCK), F32),
            pltpu.VMEM((t_all, LANES), F32),
            pltpu.VMEM((4, nc_pad, LANES), F32),
            pltpu.VMEM((2, nc, A_HEAD_DIM, 2 * A_HEAD_DIM), F32),
            pltpu.VMEM((2, nc + 1, A_HEAD_DIM, 2 * A_HEAD_DIM), F32),
            pltpu.VMEM((2, nc_pad, LANES), F32),
        ],
        compiler_params=_params("arbitrary", "arbitrary"),
        name="mlstm",
    )(qa_l, kat_l, va_l, oa_l, chunked(gr_l), qa_c, kat_c, va_c, oa_c, chunked(gr_c),
      norm_g.reshape(1, A_WIDTH))


def _swa_kernel(sink_ref, q_ref, *refs, t_lat, has_window, blocks):
    if has_window:
        bias_ref, k_ref, v_ref, kx_ref, vx_ref, o_ref, s_scr = refs
    else:
        kx_ref, vx_ref, o_ref, s_scr = refs
    L = BLOCK
    nb = t_lat // L
    group = B_HEADS // B_KV_HEADS
    lane = lax.broadcasted_iota(jnp.int32, (L, LANES), 1)
    lo = lane < B_HEAD_DIM
    zero = jnp.zeros((L, LANES), BF16)
    def window(u):
        qblk = pl.program_id(1) * blocks + u
        start = pl.multiple_of(jnp.clip((qblk - 1) * L, 0, t_lat - 3 * L), L)
        return qblk, pl.ds(start, 3 * L)

    def logits(c, u, kvh):
        sl = slice(kvh * LANES, (kvh + 1) * LANES)
        q = q_ref[0, u * L:(u + 1) * L, :]
        qa = q[:, (2 * kvh) * LANES:(2 * kvh + 1) * LANES]
        qb = q[:, (2 * kvh + 1) * LANES:(2 * kvh + 2) * LANES]
        q4 = jnp.concatenate([jnp.where(lo, qa, zero), jnp.where(lo, zero, qa),
                              jnp.where(lo, qb, zero), jnp.where(lo, zero, qb)], axis=0)
        sink = jnp.concatenate([jnp.full((L, 1), sink_ref[kvh * group + g] * LOG2_E, F32)
                                for g in range(group)], axis=0)
        if has_window:
            qblk, win = window(u)
            bias = bias_ref[jnp.where(qblk == 0, 0, jnp.where(qblk == nb - 1, 2, 1))]
            keys = jnp.concatenate([k_ref[0, win, sl], kx_ref[0, :, sl]], axis=0)
            s = _dot_nt(q4, keys)
            s = (s.reshape(group, L, s.shape[1]) + bias[None]).reshape(s.shape)
        else:
            s = _dot_nt(q4, kx_ref[0, :, sl])
        s_scr[c] = s
        return jnp.maximum(jnp.max(s, axis=1, keepdims=True), sink), sink

    def attend(c, u, kvh, m, sink):
        sl = slice(kvh * LANES, (kvh + 1) * LANES)
        if has_window:
            vals = jnp.concatenate([v_ref[0, window(u)[1], sl], vx_ref[0, :, sl]], axis=0)
        else:
            vals = vx_ref[0, :, sl]
        e = jnp.exp2(s_scr[c] - m)
        denom = jnp.sum(e, axis=1, keepdims=True) + jnp.exp2(sink - m)
        o4 = _dot(e.astype(BF16), vals) / denom
        rows = slice(u * L, (u + 1) * L)
        o_ref[0, rows, (2 * kvh) * LANES:(2 * kvh + 1) * LANES] = (
            jnp.where(lo, o4[0:L], o4[L:2 * L]).astype(BF16))
        o_ref[0, rows, (2 * kvh + 1) * LANES:(2 * kvh + 2) * LANES] = (
            jnp.where(lo, o4[2 * L:3 * L], o4[3 * L:4 * L]).astype(BF16))

    chains = [(u, kvh) for u in range(blocks) for kvh in range(B_KV_HEADS)]
    stats = [logits(0, *chains[0])]
    for c, (u, kvh) in enumerate(chains):
        if c + 1 < len(chains):
            stats.append(logits(c + 1, *chains[c + 1]))
        attend(c, u, kvh, *stats[c])


def _band_bias(t_ctx):
    L = BLOCK
    t = np.arange(L)[:, None]
    j = np.arange(3 * L)[None, :]
    tables = []
    for rel in (0, -L, -2 * L):
        ok = np.abs(j - t + rel) <= WINDOW
        tables.append(np.concatenate([np.where(ok, 0.0, -np.inf), np.zeros((L, t_ctx))], axis=1))
    return np.stack(tables).astype(np.float32)


def _swa(sink, q, k, v, k_ctx, v_ctx, has_window):
    batch, t, _ = q.shape
    t_ctx = k_ctx.shape[1]
    nb = t // BLOCK
    blocks = min(SWA_BLOCKS, nb)
    assert nb % blocks == 0 and (nb >= 3 or not has_window)
    kvw = B_KV_HEADS * LANES
    cur = lambda b, i: (b, i, 0)
    whole = lambda b, i: (b, 0, 0)
    ctx_spec = pl.BlockSpec((1, t_ctx, kvw), whole)
    in_specs = [pl.BlockSpec(memory_space=pltpu.SMEM), pl.BlockSpec((1, blocks * BLOCK, B_WIDTH), cur)]
    args = [sink, q]
    if has_window:
        bias = _band_bias(t_ctx)
        in_specs += [pl.BlockSpec(bias.shape, lambda b, i: (0, 0, 0)),
                     pl.BlockSpec((1, t, kvw), whole), pl.BlockSpec((1, t, kvw), whole)]
        args += [bias, k, v]
    in_specs += [ctx_spec, ctx_spec]
    args += [k_ctx, v_ctx]
    return pl.pallas_call(
        functools.partial(_swa_kernel, t_lat=t, has_window=has_window, blocks=blocks),
        grid=(batch, nb // blocks),
        in_specs=in_specs,
        out_specs=pl.BlockSpec((1, blocks * BLOCK, B_WIDTH), cur),
        scratch_shapes=[pltpu.VMEM((blocks * B_KV_HEADS, (B_HEADS // B_KV_HEADS) * BLOCK,
                                    (3 * BLOCK if has_window else 0) + t_ctx), F32)],
        out_shape=jax.ShapeDtypeStruct((batch, t, B_WIDTH), BF16),
        compiler_params=_params("arbitrary", "arbitrary"),
        name="swa" if has_window else "swa_ctx",
    )(*args)


def _flash_kernel(q_ref, k_ref, v_ref, o_ref, s0_scr, s1_scr, m0_scr, m1_scr, vaug_scr, *, chunks, tq, unroll):
    group = C_HEADS // C_KV_HEADS
    nq = q_ref.shape[1] // tq
    vaug_scr[:, 0:LANES] = v_ref[0]
    vaug_scr[:, LANES:] = jnp.ones((v_ref.shape[1], LANES), BF16)

    def q_rows(t):
        return pl.ds(pl.multiple_of(t * tq, tq), tq)

    def load_q(t):
        q = q_ref[0, q_rows(t), :]
        return jnp.concatenate([q[:, g * LANES:(g + 1) * LANES] for g in range(group)], axis=0)

    def stage_a(q4, s_ref, m, start, size):
        s = _dot_nt(q4, k_ref[0, start:start + size, :])
        s_ref[:, start:start + size] = s
        cm = jnp.max(s, axis=1, keepdims=True)
        return cm if m is None else jnp.maximum(m, cm)

    def stage_b(s_ref, m, acc, start, size):
        p = jnp.exp2(s_ref[:, start:start + size] - m)
        pv = _dot(p.astype(BF16), vaug_scr[start:start + size, :])
        return pv if acc is None else acc + pv

    def step(t, cur, nxt):
        q4 = load_q(jnp.minimum(t + 1, nq - 1))
        m_cur = cur[1][...]
        m_next, acc = None, None
        for start, size in chunks:
            m_next = stage_a(q4, nxt[0], m_next, start, size)
            acc = stage_b(cur[0], m_cur, acc, start, size)
        nxt[1][...] = m_next
        out = acc[:, 0:LANES] / acc[:, LANES:2 * LANES]
        for g in range(group):
            o_ref[0, q_rows(t), g * LANES:(g + 1) * LANES] = out[g * tq:(g + 1) * tq].astype(BF16)

    bufs = ((s0_scr, m0_scr), (s1_scr, m1_scr))
    q4 = load_q(0)
    m = None
    for start, size in chunks:
        m = stage_a(q4, s0_scr, m, start, size)
    m0_scr[...] = m

    def body(u, _):
        for k in range(unroll):
            step(unroll * u + k, bufs[k % 2], bufs[(k + 1) % 2])
        return 0

    lax.fori_loop(0, nq // unroll, body, 0)


def _key_chunks(total, target):
    chunks = []
    start = 0
    while start < total:
        size = min(target, total - start)
        chunks.append((start, size))
        start += size
    return tuple(chunks)


def _flash(q, k, v, t_ctx):
    batch, t, _ = q.shape
    tk_all = k.shape[1]
    tq = _row_tile(t, FLASH_ROWS)
    unroll = FLASH_UNROLL if (t // tq) % FLASH_UNROLL == 0 else 2
    assert unroll % 2 == 0 and (t // tq) % unroll == 0
    group = C_HEADS // C_KV_HEADS
    gw = group * C_HEAD_DIM
    chunks = (_key_chunks(t_ctx, FLASH_KEYS)
              + tuple((t_ctx + a, n) for a, n in _key_chunks(tk_all - t_ctx, FLASH_KEYS)))
    return pl.pallas_call(
        functools.partial(_flash_kernel, chunks=chunks, tq=tq, unroll=unroll),
        grid=(batch, C_KV_HEADS),
        in_specs=[pl.BlockSpec((1, t, gw), lambda b, h: (b, 0, h)),
                  pl.BlockSpec((1, tk_all, C_HEAD_DIM), lambda b, h: (b, 0, h)),
                  pl.BlockSpec((1, tk_all, C_HEAD_DIM), lambda b, h: (b, 0, h))],
        out_specs=pl.BlockSpec((1, t, gw), lambda b, h: (b, 0, h)),
        out_shape=jax.ShapeDtypeStruct((batch, t, C_WIDTH), BF16),
        scratch_shapes=[pltpu.VMEM((group * tq, tk_all), F32), pltpu.VMEM((group * tq, tk_all), F32),
                        pltpu.VMEM((group * tq, 1), F32), pltpu.VMEM((group * tq, 1), F32),
                        pltpu.VMEM((tk_all, 2 * C_HEAD_DIM), BF16)],
        compiler_params=_params("arbitrary", "arbitrary"),
        name="flash_c",
    )(q, k, v)


def _out_mlp_kernel(*refs, n_mix, ff_chunk):
    x_ref, mod_ref, g_ref = refs[0:3]
    mix_refs = refs[3:3 + n_mix]
    wo_refs = refs[3 + n_mix:3 + 2 * n_mix]
    w1_ref, w2_ref, o_ref = refs[3 + 2 * n_mix:]
    mod = mod_ref[0]
    y = None
    for a_ref, w_ref in zip(mix_refs, wo_refs):
        part = _dot(a_ref[0], w_ref[...])
        y = part if y is None else y + part
    x1 = x_ref[0] + mod[2:3] * y
    h = _modulated_norm(x1, g_ref[...], mod[3:4], mod[4:5]).astype(BF16)
    d_ff = w1_ref.shape[1]
    acc = None
    for c in range(d_ff // ff_chunk):
        a = _dot(h, w1_ref[:, c * ff_chunk:(c + 1) * ff_chunk])
        a = jnp.square(jnp.maximum(a, 0.0)).astype(BF16)
        part = _dot(a, w2_ref[c * ff_chunk:(c + 1) * ff_chunk, :])
        acc = part if acc is None else acc + part
    o_ref[0] = x1 + mod[5:6] * acc


def _out_mlp(x, mod, g, mixes, w_outs, w1, w2):
    batch, t, d = x.shape
    tm = _row_tile(t, MLP_ROWS)
    mod_map = (lambda b, i: (b, 0, 0)) if mod.shape[0] == batch else (lambda b, i: (0, 0, 0))
    const = lambda b, i: (0, 0)
    resident = lambda a: pl.BlockSpec(a.shape, const, pipeline_mode=pl.Buffered(1))
    row = lambda width: pl.BlockSpec((1, tm, width), lambda b, i: (b, i, 0))
    return pl.pallas_call(
        functools.partial(_out_mlp_kernel, n_mix=len(mixes), ff_chunk=1024),
        grid=(batch, t // tm),
        in_specs=([row(d), pl.BlockSpec((1, MOD_ROWS, d), mod_map), pl.BlockSpec((1, d), const)]
                  + [row(a.shape[2]) for a in mixes] + [resident(w) for w in w_outs]
                  + [resident(w1), resident(w2)]),
        out_specs=row(d),
        out_shape=jax.ShapeDtypeStruct((batch, t, d), F32),
        compiler_params=_params("arbitrary", "arbitrary"),
        name="out_mlp",
    )(x, mod, g, *mixes, *w_outs, w1, w2)


def _one_row_range(a):
    return a.reshape(1, a.shape[0] * a.shape[1], a.shape[2])


def kernel(x, c, ctx, c_ctx, ada_w, ada_b, norm1_g, norm2_g, ab_w_in, ab_gate_b, mlstm_norm_g, swa_q_norm_g,
           swa_k_norm_g, swa_sink, ab_w_out, c_w_in, c_q_norm_g, c_k_norm_g, c_w_out, mlp_w1, mlp_w2):
    depth = ada_w.shape[0]
    batch, _, d = x.shape
    mods = _mods(c, c_ctx, ada_w, ada_b)
    for layer in range(depth):
        last = layer == depth - 1
        mod_l = mods[layer, :batch]
        mod_c = mods[layer, batch:batch + 1]
        g1 = norm1_g[layer].reshape(1, d)
        g2 = norm2_g[layer].reshape(1, d)
        w1 = mlp_w1[layer].astype(BF16)
        w2 = mlp_w2[layer].astype(BF16)
        j = layer // 2
        if layer % 2 == 0:
            w, wt, gb = _ab_weights(ab_w_in[j], ab_gate_b[j])
            qn = jnp.tile(swa_q_norm_g[j], LANES // B_HEAD_DIM).reshape(1, LANES)
            kn = jnp.tile(swa_k_norm_g[j], LANES // B_HEAD_DIM).reshape(1, LANES)
            qa_l, kat_l, va_l, oa_l, qb_l, kb_l, vb_l, gr_l = _proj_ab(x, mod_l, g1, w, wt, gb, qn, kn, True)
            qa_c, kat_c, va_c, oa_c, qb_c, kb_c, vb_c, gr_c = _proj_ab(ctx, mod_c, g1, w, wt, gb, qn, kn, False)
            ha_l, ha_c = _mlstm(qa_l, kat_l, va_l, oa_l, gr_l, qa_c, kat_c, va_c, oa_c, gr_c, mlstm_norm_g[j])
            ob_l = _swa(swa_sink[j], qb_l, kb_l, vb_l, kb_c, vb_c, True)
            w_out = ab_w_out[j].astype(BF16)
            w_outs = [w_out[:A_WIDTH], w_out[A_WIDTH:]]
            x = _out_mlp(x, mod_l, g2, [ha_l, ob_l], w_outs, w1, w2)
            if not last:
                ob_c = _swa(swa_sink[j], qb_c, kb_c, vb_c, kb_c, vb_c, False)
                ctx = _out_mlp(_one_row_range(ctx), mod_c, g2, [_one_row_range(ha_c), _one_row_range(ob_c)],
                               w_outs, w1, w2).reshape(ctx.shape)
        else:
            w = c_w_in[j].astype(BF16)
            qn = c_q_norm_g[j].reshape(1, LANES)
            kn = c_k_norm_g[j].reshape(1, LANES)
            q_l, k_l, v_l = _proj_c(x, mod_l, g1, w, qn, kn, True)
            q_c, k_c, v_c = (a.reshape(batch, ctx.shape[1], -1)
                             for a in _proj_c(_one_row_range(ctx), mod_c, g1, w, qn, kn, False))
            k_all = jnp.concatenate([k_c, k_l], axis=1)
            v_all = jnp.concatenate([v_c, v_l], axis=1)
            w_out = c_w_out[j].astype(BF16)
            o_l = _flash(q_l, k_all, v_all, k_c.shape[1])
            x = _out_mlp(x, mod_l, g2, [o_l], [w_out], w1, w2)
            if not last:
                o_c = _flash(q_c, k_c, v_c, 0)
                ctx = _out_mlp(_one_row_range(ctx), mod_c, g2, [_one_row_range(o_c)], [w_out],
                               w1, w2).reshape(ctx.shape)
    return x
```
